```python
import math
import jax
import jax.numpy as jnp
from jax import lax
import numpy as np

D_MODEL = 4096
BATCH = 1
SEQ = 16384
DEPTH = 1

PLE_DIM = 256
D_MIX = D_MODEL
D_ATTN = D_MIX // 2
D_SSM = D_MIX - D_ATTN
HEAD_DIM = 64
N_Q_HEADS = D_ATTN // HEAD_DIM
N_KV_HEADS = N_Q_HEADS // 8
Q_PER_KV = N_Q_HEADS // N_KV_HEADS
WINDOW = 128
ROPE_THETA = 10000.0
SSM_HEAD_DIM = 64
N_SSM_HEADS = D_SSM // SSM_HEAD_DIM
SSM_STATE = 128
SSM_GROUPS = 8
HEADS_PER_GROUP = N_SSM_HEADS // SSM_GROUPS
SSM_CONV = 4
CHUNK = 128
D_FF = ((8 * D_MODEL // 3 + 255) // 256) * 256
FFN_CONV = 3
EPS = 1e-6

Q_COLS = N_Q_HEADS * HEAD_DIM
KV_COLS = N_KV_HEADS * HEAD_DIM
XBC_COLS = D_SSM + 2 * SSM_GROUPS * SSM_STATE
D_IN_PROJ = Q_COLS + 2 * KV_COLS + D_SSM + XBC_COLS + N_SSM_HEADS
SPLITS = [Q_COLS, Q_COLS + KV_COLS, Q_COLS + 2 * KV_COLS, Q_COLS + 2 * KV_COLS + D_SSM,
          Q_COLS + 2 * KV_COLS + D_SSM + XBC_COLS]

kernel_name = 'hymba_swa_sink_ssd_convglu_ple'


def rms_norm(x, g):
    xf = x.astype(jnp.float32)
    y = xf * lax.rsqrt(jnp.mean(xf * xf, axis=-1, keepdims=True) + EPS)
    return (y * g.astype(jnp.float32)).astype(x.dtype)


def causal_dwconv(x, w, b):
    K = w.shape[0]
    S = x.shape[1]
    xp = jnp.pad(x, ((0, 0), (K - 1, 0), (0, 0)))
    w = w.astype(x.dtype)
    y = xp[:, 0:S] * w[0]
    for k in range(1, K):
        y = y + xp[:, k:k + S] * w[k]
    return y + b.astype(x.dtype)


def rope(x, positions):
    half = x.shape[-1] // 2
    inv_freq = ROPE_THETA ** (-jnp.arange(half, dtype=jnp.float32) / half)
    ang = positions.astype(jnp.float32)[..., None] * inv_freq
    cos = jnp.cos(ang)[:, :, None, :]
    sin = jnp.sin(ang)[:, :, None, :]
    xf = x.astype(jnp.float32)
    x1, x2 = xf[..., :half], xf[..., half:]
    return jnp.concatenate([x1 * cos - x2 * sin, x2 * cos + x1 * sin], axis=-1).astype(x.dtype)


def sliding_window_attention(q, k, v, sinks):
    B, S = q.shape[0], q.shape[1]
    nb = S // WINDOW
    qb = q.reshape(B, nb, WINDOW, N_KV_HEADS, Q_PER_KV, HEAD_DIM)

    def band(t):
        tp = jnp.pad(t, ((0, 0), (WINDOW, 0), (0, 0), (0, 0)))
        tp = tp.reshape(B, nb + 1, WINDOW, N_KV_HEADS, HEAD_DIM)
        return jnp.concatenate([tp[:, :-1], tp[:, 1:]], axis=2)

    kb, vb = band(k), band(v)
    scores = jnp.einsum('bnqkgd,bnskd->bnkgqs', qb, kb,
                        preferred_element_type=jnp.float32) * (HEAD_DIM ** -0.5)
    qi = jnp.arange(WINDOW)[:, None]
    si = jnp.arange(2 * WINDOW)[None, :]
    diff = qi + WINDOW - si
    in_band = (diff >= 0) & (diff < WINDOW)
    key_pos = (jnp.arange(nb)[:, None] - 1) * WINDOW + jnp.arange(2 * WINDOW)[None, :]
    valid = in_band[None] & (key_pos >= 0)[:, None, :]
    scores = jnp.where(valid[None, :, None, None], scores, -jnp.inf)
    sink = jnp.broadcast_to(
        sinks.astype(jnp.float32).reshape(1, 1, N_KV_HEADS, Q_PER_KV, 1, 1),
        scores.shape[:-1] + (1,))
    probs = jax.nn.softmax(jnp.concatenate([scores, sink], axis=-1), axis=-1)[..., :-1]
    out = jnp.einsum('bnkgqs,bnskd->bnqkgd', probs.astype(v.dtype), vb)
    return out.reshape(B, S, N_Q_HEADS * HEAD_DIM)


def ssd_mixer(z, xbc, dt, conv_w, conv_b, dt_bias, a_log, d_skip, norm_g):
    B, S = z.shape[0], z.shape[1]
    f32 = jnp.float32
    G, R, P, N, L = SSM_GROUPS, HEADS_PER_GROUP, SSM_HEAD_DIM, SSM_STATE, CHUNK
    nc = S // L
    xbc = jax.nn.silu(causal_dwconv(xbc, conv_w, conv_b)).astype(f32)
    xs = xbc[..., :D_SSM]
    bm = xbc[..., D_SSM:D_SSM + G * N]
    cm = xbc[..., D_SSM + G * N:]
    dt = jax.nn.softplus(dt.astype(f32) + dt_bias.astype(f32))
    a = -jnp.exp(a_log.astype(f32)).reshape(G, R)
    x = xs.reshape(B, nc, L, G, R, P)
    dtc = dt.reshape(B, nc, L, G, R)
    xdt = x * dtc[..., None]
    bmat = bm.reshape(B, nc, L, G, N)
    cmat = cm.reshape(B, nc, L, G, N)
    a_cs = jnp.cumsum(jnp.moveaxis(dtc * a, 2, -1), axis=-1)
    li = jnp.arange(L)
    causal = li[:, None] >= li[None, :]
    seg = a_cs[..., :, None] - a_cs[..., None, :]
    decay = jnp.exp(jnp.where(causal, seg, -jnp.inf))
    cb = jnp.einsum('bclgn,bcsgn->bcgls', cmat, bmat)
    w_intra = cb[:, :, :, None] * decay
    y_diag = jnp.einsum('bcgrls,bcsgrp->bclgrp', w_intra, xdt)
    decay_to_end = jnp.moveaxis(jnp.exp(a_cs[..., -1:] - a_cs), -1, 2)
    chunk_states = jnp.einsum('bclgn,bclgrp->bcgrpn', bmat, xdt * decay_to_end[..., None])
    chunk_decay = jnp.exp(a_cs[..., -1])

    def step(state, inp):
        s_c, d_c = inp
        return state * d_c[..., None, None] + s_c, state

    init = jnp.zeros((B, G, R, P, N), f32)
    _, prev = lax.scan(step, init, (jnp.moveaxis(chunk_states, 1, 0), jnp.moveaxis(chunk_decay, 1, 0)))
    prev = jnp.moveaxis(prev, 0, 1)
    decay_in = jnp.moveaxis(jnp.exp(a_cs), -1, 2)[..., None]
    y_off = jnp.einsum('bclgn,bcgrpn->bclgrp', cmat, prev) * decay_in
    y = y_diag + y_off + x * d_skip.astype(f32).reshape(G, R)[:, :, None]
    y = y.reshape(B, S, D_SSM) * jax.nn.silu(z.astype(f32))
    return rms_norm(y, norm_g).astype(z.dtype)


def conv_glu_ffn(h, w_up, conv_w, conv_b, w_down):
    u = causal_dwconv(h @ w_up, conv_w, conv_b)
    gate, up = u[..., :D_FF], u[..., D_FF:]
    return (jax.nn.silu(gate) * up) @ w_down


def setup_inputs(seed: int = 0) -> dict:
    key = jax.random.key(seed)
    ks = jax.random.split(key, 24)
    f32 = jnp.float32

    def nrm(k, shape, scale):
        return jax.random.normal(k, shape, f32) * scale

    x = nrm(ks[0], (BATCH, SEQ, D_MODEL), 1.0)
    p = nrm(ks[1], (DEPTH, BATCH, SEQ, PLE_DIM), 1.0)
    positions = jnp.broadcast_to(jnp.arange(SEQ, dtype=jnp.int32), (BATCH, SEQ))
    norm_mix_g = 1.0 + nrm(ks[2], (DEPTH, D_MODEL), 0.02)
    w_in = nrm(ks[3], (DEPTH, D_MODEL, D_IN_PROJ), D_MODEL ** -0.5)
    attn_sinks = nrm(ks[4], (DEPTH, N_Q_HEADS), 0.5)
    ssm_conv_w = nrm(ks[5], (DEPTH, SSM_CONV, XBC_COLS), SSM_CONV ** -0.5)
    ssm_conv_b = nrm(ks[6], (DEPTH, XBC_COLS), 0.02)
    dt0 = jnp.exp(jax.random.uniform(ks[7], (DEPTH, N_SSM_HEADS), f32,
                                     math.log(1e-3), math.log(1e-1)))
    ssm_dt_bias = dt0 + jnp.log(-jnp.expm1(-dt0))
    ssm_a_log = jnp.log(jax.random.uniform(ks[8], (DEPTH, N_SSM_HEADS), f32, 1.0, 16.0))
    ssm_d = 1.0 + nrm(ks[9], (DEPTH, N_SSM_HEADS), 0.1)
    ssm_norm_g = 1.0 + nrm(ks[10], (DEPTH, D_SSM), 0.02)
    w_o = nrm(ks[11], (DEPTH, D_MIX, D_MODEL), D_MIX ** -0.5)
    norm_ffn_g = 1.0 + nrm(ks[12], (DEPTH, D_MODEL), 0.02)
    w_up = nrm(ks[13], (DEPTH, D_MODEL, 2 * D_FF), D_MODEL ** -0.5)
    ffn_conv_w = nrm(ks[14], (DEPTH, FFN_CONV, 2 * D_FF), FFN_CONV ** -0.5)
    ffn_conv_b = nrm(ks[15], (DEPTH, 2 * D_FF), 0.02)
    w_down = nrm(ks[16], (DEPTH, D_FF, D_MODEL), D_FF ** -0.5)
    ple_gate_w = nrm(ks[17], (DEPTH, D_MODEL, D_MODEL), D_MODEL ** -0.5)
    ple_gate_b = nrm(ks[18], (DEPTH, D_MODEL), 0.01)
    ple_proj = nrm(ks[19], (DEPTH, PLE_DIM, D_MODEL), PLE_DIM ** -0.5)
    norm_final_g = 1.0 + nrm(ks[20], (D_MODEL,), 0.02)
    return {'x': x, 'p': p, 'positions': positions, 'norm_mix_g': norm_mix_g, 'w_in': w_in,
            'attn_sinks': attn_sinks, 'ssm_conv_w': ssm_conv_w, 'ssm_conv_b': ssm_conv_b,
            'ssm_dt_bias': ssm_dt_bias, 'ssm_a_log': ssm_a_log, 'ssm_d': ssm_d,
            'ssm_norm_g': ssm_norm_g, 'w_o': w_o, 'norm_ffn_g': norm_ffn_g, 'w_up': w_up,
            'ffn_conv_w': ffn_conv_w, 'ffn_conv_b': ffn_conv_b, 'w_down': w_down,
            'ple_gate_w': ple_gate_w, 'ple_gate_b': ple_gate_b, 'ple_proj': ple_proj,
            'norm_final_g': norm_final_g}


def reference(x, p, positions, norm_mix_g, w_in, attn_sinks, ssm_conv_w, ssm_conv_b,
              ssm_dt_bias, ssm_a_log, ssm_d, ssm_norm_g, w_o, norm_ffn_g, w_up,
              ffn_conv_w, ffn_conv_b, w_down, ple_gate_w, ple_gate_b, ple_proj,
              norm_final_g):
    B, S = x.shape[0], x.shape[1]
    h = x
    for i in range(DEPTH):
        hn = rms_norm(h, norm_mix_g[i])
        proj = hn @ w_in[i]
        q, k, v, z, xbc, dt = jnp.split(proj, SPLITS, axis=-1)
        q = rope(q.reshape(B, S, N_Q_HEADS, HEAD_DIM), positions)
        k = rope(k.reshape(B, S, N_KV_HEADS, HEAD_DIM), positions)
        v = v.reshape(B, S, N_KV_HEADS, HEAD_DIM)
        y_attn = sliding_window_attention(q, k, v, attn_sinks[i])
        y_ssm = ssd_mixer(z, xbc, dt, ssm_conv_w[i], ssm_conv_b[i], ssm_dt_bias[i],
                          ssm_a_log[i], ssm_d[i], ssm_norm_g[i])
        h = h + jnp.concatenate([y_attn, y_ssm], axis=-1) @ w_o[i]
        h = h + conv_glu_ffn(rms_norm(h, norm_ffn_g[i]), w_up[i], ffn_conv_w[i],
                             ffn_conv_b[i], w_down[i])
        gate = jax.nn.sigmoid((h @ ple_gate_w[i] + ple_gate_b[i]).astype(jnp.float32)).astype(h.dtype)
        h = h + gate * (p[i] @ ple_proj[i])
    return rms_norm(h, norm_final_g)
```

```python
import functools
import math

import jax
import jax.numpy as jnp
from jax import lax
from jax.experimental import pallas as pl
from jax.experimental.pallas import tpu as pltpu

F32 = jnp.float32
BF16 = jnp.bfloat16

D_MODEL = 4096
SEQ = 16384
PLE_DIM = 256
D_ATTN = 2048
D_SSM = 2048
HEAD_DIM = 64
N_Q_HEADS = 32
N_KV_HEADS = 4
Q_PER_KV = 8
WINDOW = 128
ROPE_THETA = 10000.0
N_SSM_HEADS = 32
SSM_STATE = 128
SSM_GROUPS = 8
HEADS_PER_GROUP = 4
SSM_CONV = 4
CHUNK = 128
D_FF = 11008
FFN_CONV = 3
EPS = 1e-6
KV_COLS = N_KV_HEADS * HEAD_DIM
XBC_COLS = D_SSM + 2 * SSM_GROUPS * SSM_STATE

LANES = 128
SUBLANES = 8
VMEM_LIMIT = 56 * 1024 * 1024

COL_XBC = 0
COL_Q = COL_XBC + XBC_COLS
COL_Z = COL_Q + D_ATTN
COL_K = COL_Z + D_SSM
COL_V = COL_K + KV_COLS
COL_DT = COL_V + KV_COLS
IN_PROJ_TILE = 1280
IN_PROJ_COLS = 7 * IN_PROJ_TILE

FFN_TILE = 256
N_FFN_TILES = D_FF // FFN_TILE


def _params(sem, vmem=VMEM_LIMIT):
    return pltpu.CompilerParams(dimension_semantics=sem, vmem_limit_bytes=vmem)


def _norm_cast_kernel(x_ref, g_ref, o_ref):
    x = x_ref[...]
    ms = jnp.mean(x * x, axis=-1, keepdims=True)
    o_ref[...] = (x * lax.rsqrt(ms + EPS) * g_ref[...]).astype(o_ref.dtype)


def _norm_cast(x, g, bm=512):
    m, d = x.shape
    return pl.pallas_call(
        _norm_cast_kernel,
        grid=(m // bm,),
        in_specs=[pl.BlockSpec((bm, d), lambda i: (i, 0)),
                  pl.BlockSpec((1, d), lambda i: (0, 0))],
        out_specs=pl.BlockSpec((bm, d), lambda i: (i, 0)),
        out_shape=jax.ShapeDtypeStruct((m, d), BF16),
        compiler_params=_params(("arbitrary",)),
        name="norm_cast",
    )(x, g.reshape(1, d))


def _matmul_kernel(x_ref, w_ref, o_ref):
    o_ref[...] = jnp.dot(x_ref[...], w_ref[...], preferred_element_type=F32).astype(o_ref.dtype)


def _in_proj(hn, w, bm=1024, bn=IN_PROJ_TILE):
    m, k = hn.shape
    n = w.shape[1]
    return pl.pallas_call(
        _matmul_kernel,
        grid=(n // bn, m // bm),
        in_specs=[pl.BlockSpec((bm, k), lambda j, i: (i, 0)),
                  pl.BlockSpec((k, bn), lambda j, i: (0, j))],
        out_specs=pl.BlockSpec((bm, bn), lambda j, i: (i, j)),
        out_shape=jax.ShapeDtypeStruct((m, n), F32),
        compiler_params=_params(("arbitrary", "arbitrary")),
        name="in_proj",
    )(hn, w)


def _rope_tile(x, cos, sin_signed, first_half):
    rot = jnp.where(first_half, pltpu.roll(x, LANES - HEAD_DIM // 2, 1),
                    pltpu.roll(x, HEAD_DIM // 2, 1))
    return x * cos + rot * sin_signed


def _swa_kernel(pos_ref, freq_ref, sink_ref, q_ref, k_ref, v_ref, o_ref, kband_ref, vband_ref):
    n = pl.program_id(0)
    w = WINDOW

    @pl.when(n == 0)
    def _():
        kband_ref[...] = jnp.zeros_like(kband_ref)
        vband_ref[...] = jnp.zeros_like(vband_ref)

    lane = lax.broadcasted_iota(jnp.int32, (1, LANES), 1)
    low_half = lane < HEAD_DIM
    first_half = (lane % HEAD_DIM) < (HEAD_DIM // 2)

    ang = pos_ref[...].astype(F32) * freq_ref[...]
    cos = jnp.cos(ang)
    sin_signed = jnp.where(first_half, -jnp.sin(ang), jnp.sin(ang))

    for kh in range(N_KV_HEADS):
        kband_ref[kh, 0:w, :] = kband_ref[kh, w:2 * w, :]
        vband_ref[kh, 0:w, :] = vband_ref[kh, w:2 * w, :]
        vband_ref[kh, 2 * w:3 * w, :] = vband_ref[kh, 3 * w:4 * w, :]
    for t in range(KV_COLS // LANES):
        k_rot = _rope_tile(k_ref[:, t * LANES:(t + 1) * LANES], cos, sin_signed, first_half)
        k_sw = pltpu.roll(k_rot, HEAD_DIM, 1)
        v_t = v_ref[:, t * LANES:(t + 1) * LANES]
        v_sw = pltpu.roll(v_t, HEAD_DIM, 1)
        zero = jnp.zeros_like(v_t)
        kband_ref[2 * t, w:2 * w, :] = jnp.where(low_half, k_rot, k_sw).astype(BF16)
        kband_ref[2 * t + 1, w:2 * w, :] = jnp.where(low_half, k_sw, k_rot).astype(BF16)
        vband_ref[2 * t, w:2 * w, :] = jnp.where(low_half, v_t, zero).astype(BF16)
        vband_ref[2 * t, 3 * w:4 * w, :] = jnp.where(low_half, zero, v_sw).astype(BF16)
        vband_ref[2 * t + 1, w:2 * w, :] = jnp.where(low_half, v_sw, zero).astype(BF16)
        vband_ref[2 * t + 1, 3 * w:4 * w, :] = jnp.where(low_half, zero, v_t).astype(BF16)

    rows = Q_PER_KV * w
    qi = lax.broadcasted_iota(jnp.int32, (rows, 2 * w), 0) % w
    si = lax.broadcasted_iota(jnp.int32, (rows, 2 * w), 1)
    prev_off = jnp.where(n > 0, 0, 2 * w)
    valid = ((si < w) & (si > qi + prev_off)) | ((si >= w) & (si - w <= qi))
    scale = HEAD_DIM ** -0.5

    for kh in range(N_KV_HEADS):
        lhs = []
        sink_rows = []
        for hq in range(Q_PER_KV):
            head = kh * Q_PER_KV + hq
            t = head // 2
            q_rot = _rope_tile(q_ref[:, t * LANES:(t + 1) * LANES], cos, sin_signed, first_half) * scale
            keep = low_half if head % 2 == 0 else jnp.logical_not(low_half)
            lhs.append(jnp.where(keep, q_rot, 0.0).astype(BF16))
            sink_rows.append(jnp.full((w, 1), sink_ref[head], F32))
        lhs = jnp.concatenate(lhs, axis=0)
        sink = jnp.concatenate(sink_rows, axis=0)
        s = lax.dot_general(lhs, kband_ref[kh], (((1,), (1,)), ((), ())),
                            preferred_element_type=F32)
        s = jnp.where(valid, s, -jnp.inf)
        m = jnp.maximum(jnp.max(s, axis=-1, keepdims=True), sink)
        e = jnp.exp(s - m)
        denom = jnp.sum(e, axis=-1, keepdims=True) + jnp.exp(sink - m)
        p = (e / denom).astype(BF16)
        pcat = jnp.concatenate(
            [jnp.concatenate([p[(2 * j) * w:(2 * j + 1) * w, :], p[(2 * j + 1) * w:(2 * j + 2) * w, :]], axis=1)
             for j in range(Q_PER_KV // 2)], axis=0)
        o = jnp.dot(pcat, vband_ref[kh], preferred_element_type=F32)
        for j in range(Q_PER_KV // 2):
            t = kh * (Q_PER_KV // 2) + j
            o_ref[:, t * LANES:(t + 1) * LANES] = o[j * w:(j + 1) * w, :].astype(o_ref.dtype)


def _swa(proj, pos, freq, sinks):
    s = proj.shape[0]
    w = WINDOW
    return pl.pallas_call(
        _swa_kernel,
        grid=(s // w,),
        in_specs=[pl.BlockSpec((w, 1), lambda n: (n, 0)),
                  pl.BlockSpec((1, LANES), lambda n: (0, 0)),
                  pl.BlockSpec(memory_space=pltpu.SMEM),
                  pl.BlockSpec((w, D_ATTN), lambda n: (n, COL_Q // D_ATTN)),
                  pl.BlockSpec((w, KV_COLS), lambda n: (n, COL_K // KV_COLS)),
                  pl.BlockSpec((w, KV_COLS), lambda n: (n, COL_V // KV_COLS))],
        out_specs=pl.BlockSpec((w, D_ATTN), lambda n: (n, 0)),
        out_shape=jax.ShapeDtypeStruct((s, D_ATTN), BF16),
        scratch_shapes=[pltpu.VMEM((N_KV_HEADS, 2 * w, LANES), BF16),
                        pltpu.VMEM((N_KV_HEADS, 4 * w, LANES), BF16)],
        compiler_params=_params(("arbitrary",)),
        name="swa",
    )(pos, freq, sinks, proj, proj, proj)


def _split3(v):
    hi = v.astype(BF16)
    r = v - hi.astype(F32)
    mid = r.astype(BF16)
    lo = (r - mid.astype(F32)).astype(BF16)
    return hi, mid, lo


def _dot_exact_rhs(a, b_parts):
    acc = jnp.dot(a, b_parts[0], preferred_element_type=F32)
    for part in b_parts[1:]:
        acc = acc + jnp.dot(a, part, preferred_element_type=F32)
    return acc


def _dot_exact_lhs(a_parts, b):
    acc = jnp.dot(a_parts[0], b, preferred_element_type=F32)
    for part in a_parts[1:]:
        acc = acc + jnp.dot(part, b, preferred_element_type=F32)
    return acc


def _ssd_kernel(xbc_ref, halo_ref, z_ref, dt_ref, cw_ref, cb_ref, dtb_ref, alog_ref, dexp_ref,
                ng_ref, tri_ref, expand_ref, o_ref, xpad_ref, xc_ref, y_ref, state_ref):
    c = pl.program_id(0)
    L = CHUNK
    G, R, P, N = SSM_GROUPS, HEADS_PER_GROUP, HEAD_DIM, SSM_STATE
    GP = R * P

    @pl.when(c == 0)
    def _():
        state_ref[...] = jnp.zeros_like(state_ref)
        xpad_ref[0:SUBLANES, :] = jnp.zeros((SUBLANES, XBC_COLS), F32)

    @pl.when(c > 0)
    def _():
        xpad_ref[0:SUBLANES, :] = halo_ref[...]

    xpad_ref[SUBLANES:SUBLANES + L, :] = xbc_ref[...]

    cchunk = 512
    for j in range(XBC_COLS // cchunk):
        cs = slice(j * cchunk, (j + 1) * cchunk)
        acc = xpad_ref[SUBLANES - 3:SUBLANES - 3 + L, cs] * cw_ref[0:1, cs]
        for k in range(1, SSM_CONV):
            acc = acc + xpad_ref[SUBLANES - 3 + k:SUBLANES - 3 + k + L, cs] * cw_ref[k:k + 1, cs]
        acc = acc + cb_ref[:, cs]
        xc_ref[:, cs] = acc * jax.nn.sigmoid(acc)

    dt_raw = dt_ref[...] + dtb_ref[...]
    dt = jnp.maximum(dt_raw, 0.0) + jnp.log1p(jnp.exp(-jnp.abs(dt_raw)))
    a = -jnp.exp(alog_ref[...])
    da = dt * a
    a_cs = _dot_exact_rhs(tri_ref[...], _split3(da))
    a_cs_t = a_cs.T
    expand = expand_ref[...]
    acs_x = _dot_exact_lhs(_split3(a_cs), expand)
    dt_x = _dot_exact_lhs(_split3(dt), expand)

    li = lax.broadcasted_iota(jnp.int32, (L, L), 0)
    si = lax.broadcasted_iota(jnp.int32, (L, L), 1)
    causal = li >= si
    lane = lax.broadcasted_iota(jnp.int32, (1, LANES), 1)
    low_half = lane < P

    for g in range(G):
        xs = slice(g * GP, (g + 1) * GP)
        x_g = xc_ref[:, xs]
        b_g = xc_ref[:, D_SSM + g * N:D_SSM + (g + 1) * N].astype(BF16)
        c_g = xc_ref[:, D_SSM + G * N + g * N:D_SSM + G * N + (g + 1) * N].astype(BF16)
        acs_g = acs_x[:, xs]
        xdt_g = x_g * dt_x[:, xs]
        cb = lax.dot_general(c_g, b_g, (((1,), (1,)), ((), ())), preferred_element_type=F32)

        y_pairs = []
        for j in range(R // 2):
            ws = []
            for r in range(2):
                h = g * R + 2 * j + r
                seg = a_cs[:, h:h + 1] - a_cs_t[h:h + 1, :]
                decay = jnp.exp(jnp.where(causal, seg, -jnp.inf))
                ws.append((cb * decay).astype(BF16))
            wcat = jnp.concatenate(ws, axis=1)
            xp = xdt_g[:, j * LANES:(j + 1) * LANES]
            rhs = jnp.concatenate([jnp.where(low_half, xp, 0.0), jnp.where(low_half, 0.0, xp)],
                                  axis=0).astype(BF16)
            y_pairs.append(jnp.dot(wcat, rhs, preferred_element_type=F32))
        y_diag = jnp.concatenate(y_pairs, axis=1)

        a_last = acs_g[L - 1:L, :]
        xdte = (xdt_g * jnp.exp(a_last - acs_g)).astype(BF16)
        new_state = lax.dot_general(b_g, xdte, (((0,), (0,)), ((), ())),
                                    preferred_element_type=F32)
        prev = state_ref[g]
        y_off = jnp.dot(c_g, prev.astype(BF16), preferred_element_type=F32) * jnp.exp(acs_g)
        state_ref[g] = prev * jnp.exp(a_last) + new_state
        y_ref[:, xs] = y_diag + y_off + x_g * dexp_ref[:, xs]

    z = z_ref[...]
    y = y_ref[...] * (z * jax.nn.sigmoid(z))
    ms = jnp.mean(y * y, axis=-1, keepdims=True)
    o_ref[...] = (y * lax.rsqrt(ms + EPS) * ng_ref[...]).astype(o_ref.dtype)


def _ssd(proj, conv_w, conv_b, dt_bias, a_log, d_skip, norm_g):
    s = proj.shape[0]
    L = CHUNK
    pad = LANES - N_SSM_HEADS
    dtb = jnp.pad(dt_bias, (0, pad)).reshape(1, LANES)
    alog = jnp.pad(a_log, (0, pad)).reshape(1, LANES)
    dexp = jnp.repeat(d_skip, HEAD_DIM).reshape(1, D_SSM)
    tri = (jnp.arange(L)[:, None] >= jnp.arange(L)[None, :]).astype(BF16)
    expand = (jnp.arange(LANES)[:, None] == (jnp.arange(D_SSM)[None, :] // HEAD_DIM)).astype(BF16)
    halo_blocks = L // SUBLANES
    const = lambda c: (0, 0)
    return pl.pallas_call(
        _ssd_kernel,
        grid=(s // L,),
        in_specs=[pl.BlockSpec((L, XBC_COLS), lambda c: (c, COL_XBC // XBC_COLS)),
                  pl.BlockSpec((SUBLANES, XBC_COLS),
                               lambda c: (jnp.maximum(c * halo_blocks - 1, 0), COL_XBC // XBC_COLS)),
                  pl.BlockSpec((L, D_SSM), lambda c: (c, COL_Z // D_SSM)),
                  pl.BlockSpec((L, LANES), lambda c: (c, COL_DT // LANES)),
                  pl.BlockSpec((SSM_CONV, XBC_COLS), const),
                  pl.BlockSpec((1, XBC_COLS), const),
                  pl.BlockSpec((1, LANES), const),
                  pl.BlockSpec((1, LANES), const),
                  pl.BlockSpec((1, D_SSM), const),
                  pl.BlockSpec((1, D_SSM), const),
                  pl.BlockSpec((L, L), const),
                  pl.BlockSpec((LANES, D_SSM), const)],
        out_specs=pl.BlockSpec((L, D_SSM), lambda c: (c, 0)),
        out_shape=jax.ShapeDtypeStruct((s, D_SSM), BF16),
        scratch_shapes=[pltpu.VMEM((SUBLANES + L, XBC_COLS), F32),
                        pltpu.VMEM((L, XBC_COLS), F32),
                        pltpu.VMEM((L, D_SSM), F32),
                        pltpu.VMEM((SSM_GROUPS, SSM_STATE, HEADS_PER_GROUP * HEAD_DIM), F32)],
        compiler_params=_params(("arbitrary",)),
        name="ssd",
    )(proj, proj, proj, proj, conv_w, conv_b.reshape(1, XBC_COLS), dtb, alog, dexp,
      norm_g.reshape(1, D_SSM), tri, expand)


def _out_proj_kernel(ya_ref, ys_ref, w_ref, res_ref, o_ref):
    acc = jnp.dot(ya_ref[...], w_ref[0:D_ATTN, :], preferred_element_type=F32)
    acc = acc + jnp.dot(ys_ref[...], w_ref[D_ATTN:D_ATTN + D_SSM, :], preferred_element_type=F32)
    o_ref[...] = res_ref[...] + acc


def _out_proj(ya, ys, w, res, bm=1024, bn=1024):
    m = ya.shape[0]
    k, n = w.shape
    return pl.pallas_call(
        _out_proj_kernel,
        grid=(n // bn, m // bm),
        in_specs=[pl.BlockSpec((bm, D_ATTN), lambda j, i: (i, 0)),
                  pl.BlockSpec((bm, D_SSM), lambda j, i: (i, 0)),
                  pl.BlockSpec((k, bn), lambda j, i: (0, j)),
                  pl.BlockSpec((bm, bn), lambda j, i: (i, j))],
        out_specs=pl.BlockSpec((bm, bn), lambda j, i: (i, j)),
        out_shape=jax.ShapeDtypeStruct((m, n), F32),
        compiler_params=_params(("arbitrary", "arbitrary")),
        name="out_proj",
    )(ya, ys, w, res)


def _ffn_up_kernel(x_ref, w_ref, cw_ref, cb_ref, o_ref, ubuf_ref, carry_ref):
    i = pl.program_id(0)
    f = pl.program_id(1)
    bm = x_ref.shape[0]

    @pl.when(i == 0)
    def _():
        carry_ref[f] = jnp.zeros((SUBLANES, 2 * FFN_TILE), F32)

    u = jnp.dot(x_ref[...], w_ref[...], preferred_element_type=F32)
    ubuf_ref[0:SUBLANES, :] = carry_ref[f]
    ubuf_ref[SUBLANES:SUBLANES + bm, :] = u
    carry_ref[f] = u[bm - SUBLANES:bm, :]
    conv = ubuf_ref[SUBLANES - 2:SUBLANES - 2 + bm, :] * cw_ref[0:1, :]
    conv = conv + ubuf_ref[SUBLANES - 1:SUBLANES - 1 + bm, :] * cw_ref[1:2, :]
    conv = conv + ubuf_ref[SUBLANES:SUBLANES + bm, :] * cw_ref[2:3, :]
    conv = conv + cb_ref[...]
    gate = conv[:, 0:FFN_TILE]
    up = conv[:, FFN_TILE:2 * FFN_TILE]
    o_ref[...] = (gate * jax.nn.sigmoid(gate) * up).astype(o_ref.dtype)


def _ffn_up(hn, w, cw, cb, bm=1024):
    m, k = hn.shape
    tn = 2 * FFN_TILE
    return pl.pallas_call(
        _ffn_up_kernel,
        grid=(m // bm, N_FFN_TILES),
        in_specs=[pl.BlockSpec((bm, k), lambda i, f: (i, 0)),
                  pl.BlockSpec((k, tn), lambda i, f: (0, f)),
                  pl.BlockSpec((FFN_CONV, tn), lambda i, f: (0, f)),
                  pl.BlockSpec((1, tn), lambda i, f: (0, f))],
        out_specs=pl.BlockSpec((bm, FFN_TILE), lambda i, f: (i, f)),
        out_shape=jax.ShapeDtypeStruct((m, D_FF), BF16),
        scratch_shapes=[pltpu.VMEM((SUBLANES + bm, tn), F32),
                        pltpu.VMEM((N_FFN_TILES, SUBLANES, tn), F32)],
        compiler_params=_params(("arbitrary", "arbitrary")),
        name="ffn_up",
    )(hn, w, cw, cb)


def _ffn_down_kernel(a_ref, w_ref, res_ref, o_ref, obf_ref):
    h = res_ref[...] + jnp.dot(a_ref[...], w_ref[...], preferred_element_type=F32)
    o_ref[...] = h
    obf_ref[...] = h.astype(obf_ref.dtype)


def _ffn_down(act, w, res, bm=512, bn=512):
    m, k = act.shape
    n = w.shape[1]
    return pl.pallas_call(
        _ffn_down_kernel,
        grid=(n // bn, m // bm),
        in_specs=[pl.BlockSpec((bm, k), lambda j, i: (i, 0)),
                  pl.BlockSpec((k, bn), lambda j, i: (0, j)),
                  pl.BlockSpec((bm, bn), lambda j, i: (i, j))],
        out_specs=[pl.BlockSpec((bm, bn), lambda j, i: (i, j)),
                   pl.BlockSpec((bm, bn), lambda j, i: (i, j))],
        out_shape=[jax.ShapeDtypeStruct((m, n), F32), jax.ShapeDtypeStruct((m, n), BF16)],
        compiler_params=_params(("arbitrary", "arbitrary")),
        name="ffn_down",
    )(act, w, res)


def _ple_kernel(hb_ref, wg_ref, bg_ref, p_ref, wp_ref, h_ref, gf_ref, o_ref, ss_ref):
    j = pl.program_id(1)
    nj = pl.num_programs(1)
    bn = wg_ref.shape[1]

    @pl.when(j == 0)
    def _():
        ss_ref[...] = jnp.zeros_like(ss_ref)

    gate = jax.nn.sigmoid(jnp.dot(hb_ref[...], wg_ref[...], preferred_element_type=F32) + bg_ref[...])
    emb = jnp.dot(p_ref[...].astype(BF16), wp_ref[...], preferred_element_type=F32)
    h = h_ref[...] + gate * emb
    ss_ref[...] += jnp.sum(h * h, axis=-1, keepdims=True)
    col = pl.multiple_of(j * bn, bn)
    o_ref[:, pl.ds(col, bn)] = h

    @pl.when(j == nj - 1)
    def _():
        inv = lax.rsqrt(ss_ref[...] * (1.0 / D_MODEL) + EPS)
        o_ref[...] = o_ref[...] * inv * gf_ref[...]


def _ple(hb, wg, bg, p, wp, h, gf, bm=512, bn=512):
    m, d = h.shape
    return pl.pallas_call(
        _ple_kernel,
        grid=(m // bm, d // bn),
        in_specs=[pl.BlockSpec((bm, d), lambda i, j: (i, 0)),
                  pl.BlockSpec((d, bn), lambda i, j: (0, j)),
                  pl.BlockSpec((1, bn), lambda i, j: (0, j)),
                  pl.BlockSpec((bm, PLE_DIM), lambda i, j: (i, 0)),
                  pl.BlockSpec((PLE_DIM, bn), lambda i, j: (0, j)),
                  pl.BlockSpec((bm, bn), lambda i, j: (i, j)),
                  pl.BlockSpec((1, d), lambda i, j: (0, 0))],
        out_specs=pl.BlockSpec((bm, d), lambda i, j: (i, 0)),
        out_shape=jax.ShapeDtypeStruct((m, d), F32),
        scratch_shapes=[pltpu.VMEM((bm, 1), F32)],
        compiler_params=_params(("arbitrary", "arbitrary")),
        name="ple",
    )(hb, wg, bg, p, wp, h, gf)


def _regroup_in_proj(w):
    q0, k0, v0, z0, x0, d0 = 0, D_ATTN, D_ATTN + KV_COLS, D_ATTN + 2 * KV_COLS, \
        D_ATTN + 2 * KV_COLS + D_SSM, D_ATTN + 2 * KV_COLS + D_SSM + XBC_COLS
    end = d0 + N_SSM_HEADS
    parts = [w[:, x0:d0], w[:, q0:k0], w[:, z0:x0], w[:, k0:v0], w[:, v0:z0], w[:, d0:end]]
    out = jnp.concatenate(parts, axis=1)
    return jnp.pad(out, ((0, 0), (0, IN_PROJ_COLS - out.shape[1]))).astype(BF16)


def _interleave_gate_up(a):
    lead = a.shape[:-1]
    a = a.reshape(lead + (2, N_FFN_TILES, FFN_TILE))
    a = jnp.swapaxes(a, -3, -2)
    return a.reshape(lead + (2 * D_FF,))


def kernel(x, p, positions, norm_mix_g, w_in, attn_sinks, ssm_conv_w, ssm_conv_b, ssm_dt_bias,
           ssm_a_log, ssm_d, ssm_norm_g, w_o, norm_ffn_g, w_up, ffn_conv_w, ffn_conv_b, w_down,
           ple_gate_w, ple_gate_b, ple_proj, norm_final_g):
    b, s, d = x.shape
    assert (b, s, d) == (1, SEQ, D_MODEL) and w_in.shape[0] == 1
    h = x.reshape(s, d)
    pos = positions.reshape(s, 1)
    half = HEAD_DIM // 2
    inv_freq = ROPE_THETA ** (-jnp.arange(half, dtype=F32) / half)
    freq = jnp.tile(inv_freq, LANES // half).reshape(1, LANES)

    hn = _norm_cast(h, norm_mix_g[0])
    proj = _in_proj(hn, _regroup_in_proj(w_in[0]))
    y_attn = _swa(proj, pos, freq, attn_sinks[0])
    y_ssm = _ssd(proj, ssm_conv_w[0], ssm_conv_b[0], ssm_dt_bias[0], ssm_a_log[0], ssm_d[0],
                 ssm_norm_g[0])
    h1 = _out_proj(y_attn, y_ssm, w_o[0].astype(BF16), h)
    hn2 = _norm_cast(h1, norm_ffn_g[0])
    act = _ffn_up(hn2, _interleave_gate_up(w_up[0]).astype(BF16),
                  _interleave_gate_up(ffn_conv_w[0]),
                  _interleave_gate_up(ffn_conv_b[0]).reshape(1, 2 * D_FF))
    h2, h2b = _ffn_down(act, w_down[0].astype(BF16), h1)
    out = _ple(h2b, ple_gate_w[0].astype(BF16), ple_gate_b[0].reshape(1, d), p[0].reshape(s, PLE_DIM),
               ple_proj[0].astype(BF16), h2, norm_final_g.reshape(1, d))
    return out.reshape(b, s, d)
```

```python
import jax
import jax.numpy as jnp
from jax import lax
from jax.experimental import pallas as pl
from jax.experimental.pallas import tpu as pltpu

F32 = jnp.float32
BF16 = jnp.bfloat16

D_MODEL = 4096
SEQ = 16384
PLE_DIM = 256
D_ATTN = 2048
D_SSM = 2048
HEAD_DIM = 64
N_Q_HEADS = 32
N_KV_HEADS = 4
Q_PER_KV = 8
WINDOW = 128
ROPE_THETA = 10000.0
N_SSM_HEADS = 32
SSM_STATE = 128
SSM_GROUPS = 8
HEADS_PER_GROUP = 4
SSM_CONV = 4
CHUNK = 128
D_FF = 11008
FFN_CONV = 3
EPS = 1e-6
KV_COLS = N_KV_HEADS * HEAD_DIM
QKV_COLS = D_ATTN + 2 * KV_COLS
XBC_COLS = D_SSM + 2 * SSM_GROUPS * SSM_STATE
DT_COL0 = QKV_COLS + D_SSM + XBC_COLS

LANES = 128
SUBLANES = 8
VMEM_LIMIT = 56 * 1024 * 1024

IN_PROJ_TILE = 512
QKV_TILES = QKV_COLS // IN_PROJ_TILE
Z_TILES = D_SSM // IN_PROJ_TILE
XBC_TILES = XBC_COLS // IN_PROJ_TILE
FFN_TILE = 256
N_FFN_TILES = D_FF // FFN_TILE


def _params(sem, vmem=VMEM_LIMIT, flags=None):
    return pltpu.CompilerParams(dimension_semantics=sem, vmem_limit_bytes=vmem, flags=flags)


def _norm_cast_kernel(x_ref, g_ref, o_ref):
    x = x_ref[...]
    ms = jnp.mean(x * x, axis=-1, keepdims=True)
    o_ref[...] = (x * lax.rsqrt(ms + EPS) * g_ref[...]).astype(o_ref.dtype)


def _norm_cast(x, g, bm=512):
    m, d = x.shape
    return pl.pallas_call(
        _norm_cast_kernel,
        grid=(m // bm,),
        in_specs=[pl.BlockSpec((bm, d), lambda i: (i, 0)),
                  pl.BlockSpec((1, d), lambda i: (0, 0))],
        out_specs=pl.BlockSpec((bm, d), lambda i: (i, 0)),
        out_shape=jax.ShapeDtypeStruct((m, d), BF16),
        compiler_params=_params(("arbitrary",)),
        name="norm_cast",
    )(x, g.reshape(1, d))


def _in_proj_kernel(x_ref, w_ref, wdt_ref, qkv_ref, z_ref, xbc_ref, dt_ref):
    j = pl.program_id(1)

    def tile():
        return jnp.dot(x_ref[...], w_ref[...], preferred_element_type=F32)

    @pl.when(j < QKV_TILES)
    def _():
        qkv_ref[...] = tile()

    @pl.when((j >= QKV_TILES) & (j < QKV_TILES + Z_TILES))
    def _():
        z_ref[...] = tile()

    @pl.when(j >= QKV_TILES + Z_TILES)
    def _():
        xbc_ref[...] = tile()

    @pl.when(j == 0)
    def _():
        dt_ref[...] = jnp.dot(x_ref[...], wdt_ref[...], preferred_element_type=F32)


def _in_proj(hn, w, w_dt, bm=1024):
    m, k = hn.shape
    bn = IN_PROJ_TILE
    n_tiles = QKV_TILES + Z_TILES + XBC_TILES
    return pl.pallas_call(
        _in_proj_kernel,
        grid=(m // bm, n_tiles),
        in_specs=[pl.BlockSpec((bm, k), lambda i, j: (i, 0)),
                  pl.BlockSpec((k, bn), lambda i, j: (0, j)),
                  pl.BlockSpec((k, LANES), lambda i, j: (0, 0))],
        out_specs=[pl.BlockSpec((bm, bn), lambda i, j: (i, jnp.minimum(j, QKV_TILES - 1))),
                   pl.BlockSpec((bm, bn), lambda i, j: (i, jnp.clip(j - QKV_TILES, 0, Z_TILES - 1))),
                   pl.BlockSpec((bm, bn), lambda i, j: (i, jnp.clip(j - QKV_TILES - Z_TILES, 0, XBC_TILES - 1))),
                   pl.BlockSpec((bm, LANES), lambda i, j: (i, 0))],
        out_shape=[jax.ShapeDtypeStruct((m, QKV_COLS), F32),
                   jax.ShapeDtypeStruct((m, D_SSM), F32),
                   jax.ShapeDtypeStruct((m, XBC_COLS), F32),
                   jax.ShapeDtypeStruct((m, LANES), F32)],
        compiler_params=_params(("arbitrary", "arbitrary")),
        name="in_proj",
    )(hn, w, w_dt)


def _rope_tile(x, cos, sin_signed, first_half):
    rot = jnp.where(first_half, pltpu.roll(x, LANES - HEAD_DIM // 2, 1),
                    pltpu.roll(x, HEAD_DIM // 2, 1))
    return x * cos + rot * sin_signed


def _swa_kernel(pos_ref, freq_ref, sink_ref, q_ref, k_ref, v_ref, o_ref, kband_ref, vband_ref):
    n = pl.program_id(0)
    w = WINDOW

    lane = lax.broadcasted_iota(jnp.int32, (1, LANES), 1)
    low_half = lane < HEAD_DIM
    first_half = (lane % HEAD_DIM) < (HEAD_DIM // 2)

    @pl.when(n == 0)
    def _():
        kband_ref[...] = jnp.zeros_like(kband_ref)
        ones_lo = jnp.broadcast_to(jnp.where(low_half, 1.0, 0.0), (2 * w, LANES)).astype(BF16)
        ones_hi = jnp.broadcast_to(jnp.where(low_half, 0.0, 1.0), (2 * w, LANES)).astype(BF16)
        for kh in range(N_KV_HEADS):
            vband_ref[kh, :, 0:LANES] = jnp.zeros((4 * w, LANES), BF16)
            vband_ref[kh, 0:2 * w, LANES:2 * LANES] = ones_lo
            vband_ref[kh, 2 * w:4 * w, LANES:2 * LANES] = ones_hi

    ang = pos_ref[...].astype(F32) * freq_ref[...]
    cos = jnp.cos(ang)
    sin_signed = jnp.where(first_half, -jnp.sin(ang), jnp.sin(ang))

    for kh in range(N_KV_HEADS):
        kband_ref[kh, 0:w, :] = kband_ref[kh, w:2 * w, :]
        vband_ref[kh, 0:w, 0:LANES] = vband_ref[kh, w:2 * w, 0:LANES]
        vband_ref[kh, 2 * w:3 * w, 0:LANES] = vband_ref[kh, 3 * w:4 * w, 0:LANES]
    for t in range(KV_COLS // LANES):
        k_rot = _rope_tile(k_ref[:, t * LANES:(t + 1) * LANES], cos, sin_signed, first_half)
        k_sw = pltpu.roll(k_rot, HEAD_DIM, 1)
        v_t = v_ref[:, t * LANES:(t + 1) * LANES]
        v_sw = pltpu.roll(v_t, HEAD_DIM, 1)
        zero = jnp.zeros_like(v_t)
        kband_ref[2 * t, w:2 * w, :] = jnp.where(low_half, k_rot, k_sw).astype(BF16)
        kband_ref[2 * t + 1, w:2 * w, :] = jnp.where(low_half, k_sw, k_rot).astype(BF16)
        vband_ref[2 * t, w:2 * w, 0:LANES] = jnp.where(low_half, v_t, zero).astype(BF16)
        vband_ref[2 * t, 3 * w:4 * w, 0:LANES] = jnp.where(low_half, zero, v_sw).astype(BF16)
        vband_ref[2 * t + 1, w:2 * w, 0:LANES] = jnp.where(low_half, v_sw, zero).astype(BF16)
        vband_ref[2 * t + 1, 3 * w:4 * w, 0:LANES] = jnp.where(low_half, zero, v_t).astype(BF16)

    qi = lax.broadcasted_iota(jnp.int32, (w, 2 * w), 0)
    si = lax.broadcasted_iota(jnp.int32, (w, 2 * w), 1)
    prev_off = jnp.where(n > 0, 0, 2 * w)
    valid = ((si < w) & (si > qi + prev_off)) | ((si >= w) & (si - w <= qi))
    scale = HEAD_DIM ** -0.5

    for kh in range(N_KV_HEADS):
        kb = kband_ref[kh]
        vb = vband_ref[kh]
        for j in range(Q_PER_KV // 2):
            t = kh * (Q_PER_KV // 2) + j
            q_rot = _rope_tile(q_ref[:, t * LANES:(t + 1) * LANES], cos, sin_signed, first_half) * scale
            probs = []
            sink_terms = []
            for r in range(2):
                keep = low_half if r == 0 else jnp.logical_not(low_half)
                lhs = jnp.where(keep, q_rot, 0.0).astype(BF16)
                s = lax.dot_general(lhs, kb, (((1,), (1,)), ((), ())), preferred_element_type=F32)
                s = jnp.where(valid, s, -jnp.inf)
                sink = sink_ref[2 * t + r]
                m = jnp.max(jnp.maximum(s[:, 0:w], s[:, w:2 * w]), axis=-1, keepdims=True)
                m = jnp.maximum(m, sink)
                probs.append(jnp.exp(s - m).astype(BF16))
                sink_terms.append(jnp.exp(sink - m))
            od = jnp.dot(jnp.concatenate(probs, axis=1), vb, preferred_element_type=F32)
            denom = od[:, LANES:2 * LANES] + jnp.where(low_half, sink_terms[0], sink_terms[1])
            o_ref[:, t * LANES:(t + 1) * LANES] = (od[:, 0:LANES] / denom).astype(o_ref.dtype)


def _swa(qkv, pos, freq, sinks):
    s = qkv.shape[0]
    w = WINDOW
    return pl.pallas_call(
        _swa_kernel,
        grid=(s // w,),
        in_specs=[pl.BlockSpec((w, 1), lambda n: (n, 0)),
                  pl.BlockSpec((1, LANES), lambda n: (0, 0)),
                  pl.BlockSpec(memory_space=pltpu.SMEM),
                  pl.BlockSpec((w, D_ATTN), lambda n: (n, 0)),
                  pl.BlockSpec((w, KV_COLS), lambda n: (n, D_ATTN // KV_COLS)),
                  pl.BlockSpec((w, KV_COLS), lambda n: (n, D_ATTN // KV_COLS + 1))],
        out_specs=pl.BlockSpec((w, D_ATTN), lambda n: (n, 0)),
        out_shape=jax.ShapeDtypeStruct((s, D_ATTN), BF16),
        scratch_shapes=[pltpu.VMEM((N_KV_HEADS, 2 * w, LANES), BF16),
                        pltpu.VMEM((N_KV_HEADS, 4 * w, 2 * LANES), BF16)],
        compiler_params=_params(("arbitrary",)),
        name="swa",
    )(pos, freq, sinks, qkv, qkv, qkv)


def _split3(v):
    hi = v.astype(BF16)
    r = v - hi.astype(F32)
    mid = r.astype(BF16)
    lo = (r - mid.astype(F32)).astype(BF16)
    return hi, mid, lo


def _dot_exact_rhs(a, b_parts):
    acc = jnp.dot(a, b_parts[0], preferred_element_type=F32)
    for part in b_parts[1:]:
        acc = acc + jnp.dot(a, part, preferred_element_type=F32)
    return acc


def _dot_exact_lhs(a_parts, b):
    acc = jnp.dot(a_parts[0], b, preferred_element_type=F32)
    for part in a_parts[1:]:
        acc = acc + jnp.dot(part, b, preferred_element_type=F32)
    return acc


def _ssd_kernel(xbc_ref, z_ref, dt_ref, cw_ref, cb_ref, dtb_ref, alog_ref, dexp_ref,
                ng_ref, tri_ref, expand_ref, o_ref, xpad_ref, xc_ref, y_ref, state_ref):
    c = pl.program_id(0)
    L = CHUNK
    G, R, P, N = SSM_GROUPS, HEADS_PER_GROUP, HEAD_DIM, SSM_STATE
    GP = R * P

    @pl.when(c == 0)
    def _():
        state_ref[...] = jnp.zeros_like(state_ref)
        xpad_ref[:, L:L + SUBLANES, :] = jnp.zeros((XBC_COLS // LANES, SUBLANES, LANES), F32)

    xpad_ref[:, 0:SUBLANES, :] = xpad_ref[:, L:L + SUBLANES, :]

    for p in range(XBC_COLS // LANES):
        cs = slice(p * LANES, (p + 1) * LANES)
        xpad_ref[p, SUBLANES:SUBLANES + L, :] = xbc_ref[:, cs]
        acc = xpad_ref[p, SUBLANES - 3:SUBLANES - 3 + L, :] * cw_ref[0:1, cs]
        for k in range(1, SSM_CONV):
            acc = acc + xpad_ref[p, SUBLANES - 3 + k:SUBLANES - 3 + k + L, :] * cw_ref[k:k + 1, cs]
        acc = acc + cb_ref[:, cs]
        xc_ref[:, cs] = acc * jax.nn.sigmoid(acc)

    dt_raw = dt_ref[...] + dtb_ref[...]
    dt = jnp.maximum(dt_raw, 0.0) + jnp.log1p(jnp.exp(-jnp.abs(dt_raw)))
    a = -jnp.exp(alog_ref[...])
    da = dt * a
    a_cs = _dot_exact_rhs(tri_ref[...], _split3(da))
    a_cs_t = a_cs.T
    expand = expand_ref[...]
    acs_x = _dot_exact_lhs(_split3(a_cs), expand)
    dt_x = _dot_exact_lhs(_split3(dt), expand)

    li = lax.broadcasted_iota(jnp.int32, (L, L), 0)
    si = lax.broadcasted_iota(jnp.int32, (L, L), 1)
    causal = li >= si
    lane = lax.broadcasted_iota(jnp.int32, (1, LANES), 1)
    low_half = lane < P

    for g in range(G):
        xs = slice(g * GP, (g + 1) * GP)
        x_g = xc_ref[:, xs]
        b_g = xc_ref[:, D_SSM + g * N:D_SSM + (g + 1) * N].astype(BF16)
        c_g = xc_ref[:, D_SSM + G * N + g * N:D_SSM + G * N + (g + 1) * N].astype(BF16)
        acs_g = acs_x[:, xs]
        xdt_g = x_g * dt_x[:, xs]
        cb = lax.dot_general(c_g, b_g, (((1,), (1,)), ((), ())), preferred_element_type=F32)

        y_pairs = []
        for j in range(R // 2):
            ws = []
            for r in range(2):
                h = g * R + 2 * j + r
                seg = a_cs[:, h:h + 1] - a_cs_t[h:h + 1, :]
                decay = jnp.exp(jnp.where(causal, seg, -jnp.inf))
                ws.append((cb * decay).astype(BF16))
            wcat = jnp.concatenate(ws, axis=1)
            xp = xdt_g[:, j * LANES:(j + 1) * LANES]
            rhs = jnp.concatenate([jnp.where(low_half, xp, 0.0), jnp.where(low_half, 0.0, xp)],
                                  axis=0).astype(BF16)
            y_pairs.append(jnp.dot(wcat, rhs, preferred_element_type=F32))
        y_diag = jnp.concatenate(y_pairs, axis=1)

        a_last = acs_g[L - 1:L, :]
        xdte = (xdt_g * jnp.exp(a_last - acs_g)).astype(BF16)
        new_state = lax.dot_general(b_g, xdte, (((0,), (0,)), ((), ())),
                                    preferred_element_type=F32)
        prev = state_ref[g]
        y_off = jnp.dot(c_g, prev.astype(BF16), preferred_element_type=F32) * jnp.exp(acs_g)
        state_ref[g] = prev * jnp.exp(a_last) + new_state
        y_ref[:, xs] = y_diag + y_off + x_g * dexp_ref[:, xs]

    z = z_ref[...]
    y = y_ref[...] * (z * jax.nn.sigmoid(z))
    ms = jnp.mean(y * y, axis=-1, keepdims=True)
    o_ref[...] = (y * lax.rsqrt(ms + EPS) * ng_ref[...]).astype(o_ref.dtype)


def _ssd(xbc, z, dt, conv_w, conv_b, dt_bias, a_log, d_skip, norm_g):
    s = xbc.shape[0]
    L = CHUNK
    pad = LANES - N_SSM_HEADS
    dtb = jnp.pad(dt_bias, (0, pad)).reshape(1, LANES)
    alog = jnp.pad(a_log, (0, pad)).reshape(1, LANES)
    dexp = jnp.repeat(d_skip, HEAD_DIM).reshape(1, D_SSM)
    tri = (jnp.arange(L)[:, None] >= jnp.arange(L)[None, :]).astype(BF16)
    expand = (jnp.arange(LANES)[:, None] == (jnp.arange(D_SSM)[None, :] // HEAD_DIM)).astype(BF16)
    const = lambda c: (0, 0)
    return pl.pallas_call(
        _ssd_kernel,
        grid=(s // L,),
        in_specs=[pl.BlockSpec((L, XBC_COLS), lambda c: (c, 0)),
                  pl.BlockSpec((L, D_SSM), lambda c: (c, 0)),
                  pl.BlockSpec((L, LANES), lambda c: (c, 0)),
                  pl.BlockSpec((SSM_CONV, XBC_COLS), const),
                  pl.BlockSpec((1, XBC_COLS), const),
                  pl.BlockSpec((1, LANES), const),
                  pl.BlockSpec((1, LANES), const),
                  pl.BlockSpec((1, D_SSM), const),
                  pl.BlockSpec((1, D_SSM), const),
                  pl.BlockSpec((L, L), const),
                  pl.BlockSpec((LANES, D_SSM), const)],
        out_specs=pl.BlockSpec((L, D_SSM), lambda c: (c, 0)),
        out_shape=jax.ShapeDtypeStruct((s, D_SSM), BF16),
        scratch_shapes=[pltpu.VMEM((XBC_COLS // LANES, SUBLANES + L, LANES), F32),
                        pltpu.VMEM((L, XBC_COLS), F32),
                        pltpu.VMEM((L, D_SSM), F32),
                        pltpu.VMEM((SSM_GROUPS, SSM_STATE, HEADS_PER_GROUP * HEAD_DIM), F32)],
        compiler_params=_params(("arbitrary",)),
        name="ssd",
    )(xbc, z, dt, conv_w, conv_b.reshape(1, XBC_COLS), dtb, alog, dexp,
      norm_g.reshape(1, D_SSM), tri, expand)


def _out_proj_kernel(ya_ref, ys_ref, w_ref, res_ref, o_ref):
    acc = jnp.dot(ya_ref[...], w_ref[0:D_ATTN, :], preferred_element_type=F32)
    acc = acc + jnp.dot(ys_ref[...], w_ref[D_ATTN:D_ATTN + D_SSM, :], preferred_element_type=F32)
    o_ref[...] = res_ref[...] + acc


def _out_proj(ya, ys, w, res, bm=1024, bn=1024):
    m = ya.shape[0]
    k, n = w.shape
    return pl.pallas_call(
        _out_proj_kernel,
        grid=(n // bn, m // bm),
        in_specs=[pl.BlockSpec((bm, D_ATTN), lambda j, i: (i, 0)),
                  pl.BlockSpec((bm, D_SSM), lambda j, i: (i, 0)),
                  pl.BlockSpec((k, bn), lambda j, i: (0, j)),
                  pl.BlockSpec((bm, bn), lambda j, i: (i, j))],
        out_specs=pl.BlockSpec((bm, bn), lambda j, i: (i, j)),
        out_shape=jax.ShapeDtypeStruct((m, n), F32),
        compiler_params=_params(("arbitrary", "arbitrary")),
        name="out_proj",
    )(ya, ys, w, res)


def _ffn_up_kernel(x_ref, wg_ref, wu_ref, cw_ref, cb_ref, o_ref, ubuf_ref, carry_ref):
    t = pl.program_id(0)
    n_steps = pl.num_programs(0)
    bm = x_ref.shape[0]
    ft = FFN_TILE
    planes_per_half = ft // LANES

    @pl.when(t == 0)
    def _():
        ubuf_ref[...] = jnp.zeros_like(ubuf_ref)
        carry_ref[...] = jnp.zeros_like(carry_ref)

    f_done = jnp.maximum(t - 1, 0) % N_FFN_TILES
    for q in range(planes_per_half):
        conv = []
        for half in range(2):
            plane = half * planes_per_half + q
            col = pl.multiple_of(half * D_FF + f_done * ft + q * LANES, LANES)
            acc = ubuf_ref[plane, SUBLANES - 2:SUBLANES - 2 + bm, :] * cw_ref[0:1, pl.ds(col, LANES)]
            acc = acc + ubuf_ref[plane, SUBLANES - 1:SUBLANES - 1 + bm, :] * cw_ref[1:2, pl.ds(col, LANES)]
            acc = acc + ubuf_ref[plane, SUBLANES:SUBLANES + bm, :] * cw_ref[2:3, pl.ds(col, LANES)]
            conv.append(acc + cb_ref[:, pl.ds(col, LANES)])
        gate, up = conv
        o_ref[:, q * LANES:(q + 1) * LANES] = (gate * jax.nn.sigmoid(gate) * up).astype(o_ref.dtype)
    carry_ref[f_done] = ubuf_ref[:, bm:bm + SUBLANES, :]

    f_new = jnp.minimum(t, n_steps - 2) % N_FFN_TILES
    x = x_ref[...]
    ubuf_ref[:, 0:SUBLANES, :] = carry_ref[f_new]
    for half, w_ref in enumerate((wg_ref, wu_ref)):
        u = jnp.dot(x, w_ref[...], preferred_element_type=F32)
        for q in range(planes_per_half):
            ubuf_ref[half * planes_per_half + q, SUBLANES:SUBLANES + bm, :] = u[:, q * LANES:(q + 1) * LANES]


def _ffn_up(hn, w, cw, cb, bm=1024):
    m, k = hn.shape
    nf = N_FFN_TILES
    n_tiles = (m // bm) * nf
    n_planes = 2 * FFN_TILE // LANES
    cur = lambda t: jnp.minimum(t, n_tiles - 1)
    done = lambda t: jnp.maximum(t - 1, 0)
    return pl.pallas_call(
        _ffn_up_kernel,
        grid=(n_tiles + 1,),
        in_specs=[pl.BlockSpec((bm, k), lambda t: (cur(t) // nf, 0)),
                  pl.BlockSpec((k, FFN_TILE), lambda t: (0, cur(t) % nf)),
                  pl.BlockSpec((k, FFN_TILE), lambda t: (0, nf + cur(t) % nf)),
                  pl.BlockSpec((FFN_CONV, 2 * D_FF), lambda t: (0, 0)),
                  pl.BlockSpec((1, 2 * D_FF), lambda t: (0, 0))],
        out_specs=pl.BlockSpec((bm, FFN_TILE), lambda t: (done(t) // nf, done(t) % nf)),
        out_shape=jax.ShapeDtypeStruct((m, D_FF), BF16),
        scratch_shapes=[pltpu.VMEM((n_planes, SUBLANES + bm, LANES), F32),
                        pltpu.VMEM((nf, n_planes, SUBLANES, LANES), F32)],
        compiler_params=_params(("arbitrary",)),
        name="ffn_up",
    )(hn, w, w, cw, cb)


def _ffn_down_kernel(a_ref, w_ref, res_ref, o_ref, obf_ref):
    h = res_ref[...] + jnp.dot(a_ref[...], w_ref[...], preferred_element_type=F32)
    o_ref[...] = h
    obf_ref[...] = h.astype(obf_ref.dtype)


def _ffn_down(act, w, res, bm=512, bn=512):
    m, k = act.shape
    n = w.shape[1]
    return pl.pallas_call(
        _ffn_down_kernel,
        grid=(n // bn, m // bm),
        in_specs=[pl.BlockSpec((bm, k), lambda j, i: (i, 0)),
                  pl.BlockSpec((k, bn), lambda j, i: (0, j)),
                  pl.BlockSpec((bm, bn), lambda j, i: (i, j))],
        out_specs=[pl.BlockSpec((bm, bn), lambda j, i: (i, j)),
                   pl.BlockSpec((bm, bn), lambda j, i: (i, j))],
        out_shape=[jax.ShapeDtypeStruct((m, n), F32), jax.ShapeDtypeStruct((m, n), BF16)],
        compiler_params=_params(("arbitrary", "arbitrary")),
        name="ffn_down",
    )(act, w, res)


def _ple_kernel(hb_ref, wg_ref, bg_ref, p_ref, wp_ref, h_ref, gf_ref, o_ref, ss_ref):
    j = pl.program_id(1)
    nj = pl.num_programs(1)
    bn = wg_ref.shape[1]

    @pl.when(j == 0)
    def _():
        ss_ref[...] = jnp.zeros_like(ss_ref)

    gate = jax.nn.sigmoid(jnp.dot(hb_ref[...], wg_ref[...], preferred_element_type=F32) + bg_ref[...])
    emb = jnp.dot(p_ref[...].astype(BF16), wp_ref[...], preferred_element_type=F32)
    h = h_ref[...] + gate * emb
    ss_ref[...] += jnp.sum(h * h, axis=-1, keepdims=True)
    col = pl.multiple_of(j * bn, bn)
    o_ref[:, pl.ds(col, bn)] = h

    @pl.when(j == nj - 1)
    def _():
        inv = lax.rsqrt(ss_ref[...] * (1.0 / D_MODEL) + EPS)
        o_ref[...] = o_ref[...] * inv * gf_ref[...]


def _ple(hb, wg, bg, p, wp, h, gf, bm=512, bn=512):
    m, d = h.shape
    return pl.pallas_call(
        _ple_kernel,
        grid=(m // bm, d // bn),
        in_specs=[pl.BlockSpec((bm, d), lambda i, j: (i, 0)),
                  pl.BlockSpec((d, bn), lambda i, j: (0, j)),
                  pl.BlockSpec((1, bn), lambda i, j: (0, j)),
                  pl.BlockSpec((bm, PLE_DIM), lambda i, j: (i, 0)),
                  pl.BlockSpec((PLE_DIM, bn), lambda i, j: (0, j)),
                  pl.BlockSpec((bm, bn), lambda i, j: (i, j)),
                  pl.BlockSpec((1, d), lambda i, j: (0, 0))],
        out_specs=pl.BlockSpec((bm, d), lambda i, j: (i, 0)),
        out_shape=jax.ShapeDtypeStruct((m, d), F32),
        scratch_shapes=[pltpu.VMEM((bm, 1), F32)],
        compiler_params=_params(("arbitrary", "arbitrary")),
        name="ple",
    )(hb, wg, bg, p, wp, h, gf)


def kernel(x, p, positions, norm_mix_g, w_in, attn_sinks, ssm_conv_w, ssm_conv_b, ssm_dt_bias,
           ssm_a_log, ssm_d, ssm_norm_g, w_o, norm_ffn_g, w_up, ffn_conv_w, ffn_conv_b, w_down,
           ple_gate_w, ple_gate_b, ple_proj, norm_final_g):
    b, s, d = x.shape
    assert (b, s, d) == (1, SEQ, D_MODEL) and w_in.shape[0] == 1
    h = x.reshape(s, d)
    pos = positions.reshape(s, 1)
    half = HEAD_DIM // 2
    inv_freq = ROPE_THETA ** (-jnp.arange(half, dtype=F32) / half)
    freq = jnp.tile(inv_freq, LANES // half).reshape(1, LANES)
    w_dt = jnp.pad(w_in[0][:, DT_COL0:DT_COL0 + N_SSM_HEADS], ((0, 0), (0, LANES - N_SSM_HEADS)))

    hn = _norm_cast(h, norm_mix_g[0])
    qkv, z, xbc, dt = _in_proj(hn, w_in[0].astype(BF16), w_dt.astype(BF16))
    y_attn = _swa(qkv, pos, freq, attn_sinks[0])
    y_ssm = _ssd(xbc, z, dt, ssm_conv_w[0], ssm_conv_b[0], ssm_dt_bias[0], ssm_a_log[0], ssm_d[0],
                 ssm_norm_g[0])
    h1 = _out_proj(y_attn, y_ssm, w_o[0].astype(BF16), h)
    hn2 = _norm_cast(h1, norm_ffn_g[0])
    act = _ffn_up(hn2, w_up[0].astype(BF16), ffn_conv_w[0], ffn_conv_b[0].reshape(1, 2 * D_FF))
    h2, h2b = _ffn_down(act, w_down[0].astype(BF16), h1)
    out = _ple(h2b, ple_gate_w[0].astype(BF16), ple_gate_b[0].reshape(1, d), p[0].reshape(s, PLE_DIM),
               ple_proj[0].astype(BF16), h2, norm_final_g.reshape(1, d))
    return out.reshape(b, s, d)
```

```python
import jax
import jax.numpy as jnp
from jax import lax
from jax.experimental import pallas as pl
from jax.experimental.pallas import tpu as pltpu

F32 = jnp.float32
BF16 = jnp.bfloat16

D_MODEL = 4096
SEQ = 16384
PLE_DIM = 256
D_ATTN = 2048
D_SSM = 2048
HEAD_DIM = 64
N_Q_HEADS = 32
N_KV_HEADS = 4
Q_PER_KV = 8
WINDOW = 128
ROPE_THETA = 10000.0
N_SSM_HEADS = 32
SSM_STATE = 128
SSM_GROUPS = 8
HEADS_PER_GROUP = 4
SSM_CONV = 4
CHUNK = 128
D_FF = 11008
FFN_CONV = 3
EPS = 1e-6
KV_COLS = N_KV_HEADS * HEAD_DIM
QKV_COLS = D_ATTN + 2 * KV_COLS
XBC_COLS = D_SSM + 2 * SSM_GROUPS * SSM_STATE
DT_COL0 = QKV_COLS + D_SSM + XBC_COLS

LANES = 128
SUBLANES = 8
VMEM_LIMIT = 56 * 1024 * 1024
VMEM_LIMIT_HIGH = 62 * 1024 * 1024

IN_PROJ_TILE = 512
QKV_TILES = QKV_COLS // IN_PROJ_TILE
Z_TILES = D_SSM // IN_PROJ_TILE
XBC_TILES = XBC_COLS // IN_PROJ_TILE
FFN_TILE = 256
N_FFN_TILES = D_FF // FFN_TILE


def _params(sem, vmem=VMEM_LIMIT, flags=None):
    return pltpu.CompilerParams(dimension_semantics=sem, vmem_limit_bytes=vmem, flags=flags)


def _norm_cast_kernel(x_ref, g_ref, o_ref):
    x = x_ref[...]
    ms = jnp.mean(x * x, axis=-1, keepdims=True)
    o_ref[...] = (x * lax.rsqrt(ms + EPS) * g_ref[...]).astype(o_ref.dtype)


def _norm_cast(x, g, bm=512):
    m, d = x.shape
    return pl.pallas_call(
        _norm_cast_kernel,
        grid=(m // bm,),
        in_specs=[pl.BlockSpec((bm, d), lambda i: (i, 0)),
                  pl.BlockSpec((1, d), lambda i: (0, 0))],
        out_specs=pl.BlockSpec((bm, d), lambda i: (i, 0)),
        out_shape=jax.ShapeDtypeStruct((m, d), BF16),
        compiler_params=_params(("arbitrary",)),
        name="norm_cast",
    )(x, g.reshape(1, d))


def _in_proj_kernel(x_ref, w_ref, wdt_ref, qkv_ref, z_ref, xbc_ref, dt_ref):
    j = pl.program_id(1)

    def tile():
        return jnp.dot(x_ref[...], w_ref[...], preferred_element_type=F32)

    @pl.when(j < QKV_TILES)
    def _():
        qkv_ref[...] = tile()

    @pl.when((j >= QKV_TILES) & (j < QKV_TILES + Z_TILES))
    def _():
        z_ref[...] = tile()

    @pl.when(j >= QKV_TILES + Z_TILES)
    def _():
        xbc_ref[...] = tile()

    @pl.when(j == 0)
    def _():
        dt_ref[...] = jnp.dot(x_ref[...], wdt_ref[...], preferred_element_type=F32)


def _in_proj(hn, w, w_dt, bm=1024):
    m, k = hn.shape
    bn = IN_PROJ_TILE
    n_tiles = QKV_TILES + Z_TILES + XBC_TILES
    return pl.pallas_call(
        _in_proj_kernel,
        grid=(m // bm, n_tiles),
        in_specs=[pl.BlockSpec((bm, k), lambda i, j: (i, 0)),
                  pl.BlockSpec((k, bn), lambda i, j: (0, j)),
                  pl.BlockSpec((k, LANES), lambda i, j: (0, 0))],
        out_specs=[pl.BlockSpec((bm, bn), lambda i, j: (i, jnp.minimum(j, QKV_TILES - 1))),
                   pl.BlockSpec((bm, bn), lambda i, j: (i, jnp.clip(j - QKV_TILES, 0, Z_TILES - 1))),
                   pl.BlockSpec((bm, bn), lambda i, j: (i, jnp.clip(j - QKV_TILES - Z_TILES, 0, XBC_TILES - 1))),
                   pl.BlockSpec((bm, LANES), lambda i, j: (i, 0))],
        out_shape=[jax.ShapeDtypeStruct((m, QKV_COLS), F32),
                   jax.ShapeDtypeStruct((m, D_SSM), F32),
                   jax.ShapeDtypeStruct((m, XBC_COLS), F32),
                   jax.ShapeDtypeStruct((m, LANES), F32)],
        compiler_params=_params(("arbitrary", "arbitrary")),
        name="in_proj",
    )(hn, w, w_dt)


def _rope_tile(x, cos, sin_signed, first_half):
    rot = jnp.where(first_half, pltpu.roll(x, LANES - HEAD_DIM // 2, 1),
                    pltpu.roll(x, HEAD_DIM // 2, 1))
    return x * cos + rot * sin_signed


def _swa_kernel(pos_ref, freq_ref, sink_ref, q_ref, k_ref, v_ref, o_ref, kband_ref, vband_ref):
    n = pl.program_id(0)
    w = WINDOW

    lane = lax.broadcasted_iota(jnp.int32, (1, LANES), 1)
    low_half = lane < HEAD_DIM
    first_half = (lane % HEAD_DIM) < (HEAD_DIM // 2)

    @pl.when(n == 0)
    def _():
        kband_ref[...] = jnp.zeros_like(kband_ref)
        ones_lo = jnp.broadcast_to(jnp.where(low_half, 1.0, 0.0), (2 * w, LANES)).astype(BF16)
        ones_hi = jnp.broadcast_to(jnp.where(low_half, 0.0, 1.0), (2 * w, LANES)).astype(BF16)
        for kh in range(N_KV_HEADS):
            vband_ref[kh, :, 0:LANES] = jnp.zeros((4 * w, LANES), BF16)
            vband_ref[kh, 0:2 * w, LANES:2 * LANES] = ones_lo
            vband_ref[kh, 2 * w:4 * w, LANES:2 * LANES] = ones_hi

    ang = pos_ref[...].astype(F32) * freq_ref[...]
    cos = jnp.cos(ang)
    sin_signed = jnp.where(first_half, -jnp.sin(ang), jnp.sin(ang))

    for kh in range(N_KV_HEADS):
        kband_ref[kh, 0:w, :] = kband_ref[kh, w:2 * w, :]
        vband_ref[kh, 0:w, 0:LANES] = vband_ref[kh, w:2 * w, 0:LANES]
        vband_ref[kh, 2 * w:3 * w, 0:LANES] = vband_ref[kh, 3 * w:4 * w, 0:LANES]
    for t in range(KV_COLS // LANES):
        k_rot = _rope_tile(k_ref[:, t * LANES:(t + 1) * LANES], cos, sin_signed, first_half)
        k_sw = pltpu.roll(k_rot, HEAD_DIM, 1)
        v_t = v_ref[:, t * LANES:(t + 1) * LANES]
        v_sw = pltpu.roll(v_t, HEAD_DIM, 1)
        zero = jnp.zeros_like(v_t)
        kband_ref[2 * t, w:2 * w, :] = jnp.where(low_half, k_rot, k_sw).astype(BF16)
        kband_ref[2 * t + 1, w:2 * w, :] = jnp.where(low_half, k_sw, k_rot).astype(BF16)
        vband_ref[2 * t, w:2 * w, 0:LANES] = jnp.where(low_half, v_t, zero).astype(BF16)
        vband_ref[2 * t, 3 * w:4 * w, 0:LANES] = jnp.where(low_half, zero, v_sw).astype(BF16)
        vband_ref[2 * t + 1, w:2 * w, 0:LANES] = jnp.where(low_half, v_sw, zero).astype(BF16)
        vband_ref[2 * t + 1, 3 * w:4 * w, 0:LANES] = jnp.where(low_half, zero, v_t).astype(BF16)

    qi = lax.broadcasted_iota(jnp.int32, (w, 2 * w), 0)
    si = lax.broadcasted_iota(jnp.int32, (w, 2 * w), 1)
    prev_off = jnp.where(n > 0, 0, 2 * w)
    valid = ((si < w) & (si > qi + prev_off)) | ((si >= w) & (si - w <= qi))
    scale = HEAD_DIM ** -0.5

    for kh in range(N_KV_HEADS):
        kb = kband_ref[kh]
        vb = vband_ref[kh]
        for j in range(Q_PER_KV // 2):
            t = kh * (Q_PER_KV // 2) + j
            q_rot = _rope_tile(q_ref[:, t * LANES:(t + 1) * LANES], cos, sin_signed, first_half) * scale
            probs = []
            sink_terms = []
            for r in range(2):
                keep = low_half if r == 0 else jnp.logical_not(low_half)
                lhs = jnp.where(keep, q_rot, 0.0).astype(BF16)
                s = lax.dot_general(lhs, kb, (((1,), (1,)), ((), ())), preferred_element_type=F32)
                s = jnp.where(valid, s, -jnp.inf)
                sink = sink_ref[2 * t + r]
                m = jnp.max(jnp.maximum(s[:, 0:w], s[:, w:2 * w]), axis=-1, keepdims=True)
                m = jnp.maximum(m, sink)
                probs.append(jnp.exp(s - m).astype(BF16))
                sink_terms.append(jnp.exp(sink - m))
            od = jnp.dot(jnp.concatenate(probs, axis=1), vb, preferred_element_type=F32)
            denom = od[:, LANES:2 * LANES] + jnp.where(low_half, sink_terms[0], sink_terms[1])
            o_ref[:, t * LANES:(t + 1) * LANES] = (od[:, 0:LANES] / denom).astype(o_ref.dtype)


def _swa(qkv, pos, freq, sinks):
    s = qkv.shape[0]
    w = WINDOW
    return pl.pallas_call(
        _swa_kernel,
        grid=(s // w,),
        in_specs=[pl.BlockSpec((w, 1), lambda n: (n, 0)),
                  pl.BlockSpec((1, LANES), lambda n: (0, 0)),
                  pl.BlockSpec(memory_space=pltpu.SMEM),
                  pl.BlockSpec((w, D_ATTN), lambda n: (n, 0)),
                  pl.BlockSpec((w, KV_COLS), lambda n: (n, D_ATTN // KV_COLS)),
                  pl.BlockSpec((w, KV_COLS), lambda n: (n, D_ATTN // KV_COLS + 1))],
        out_specs=pl.BlockSpec((w, D_ATTN), lambda n: (n, 0)),
        out_shape=jax.ShapeDtypeStruct((s, D_ATTN), BF16),
        scratch_shapes=[pltpu.VMEM((N_KV_HEADS, 2 * w, LANES), BF16),
                        pltpu.VMEM((N_KV_HEADS, 4 * w, 2 * LANES), BF16)],
        compiler_params=_params(("arbitrary",)),
        name="swa",
    )(pos, freq, sinks, qkv, qkv, qkv)


def _split3(v):
    hi = v.astype(BF16)
    r = v - hi.astype(F32)
    mid = r.astype(BF16)
    lo = (r - mid.astype(F32)).astype(BF16)
    return hi, mid, lo


def _dot_exact_rhs(a, b_parts):
    acc = jnp.dot(a, b_parts[0], preferred_element_type=F32)
    for part in b_parts[1:]:
        acc = acc + jnp.dot(a, part, preferred_element_type=F32)
    return acc


def _dot_exact_lhs(a_parts, b):
    acc = jnp.dot(a_parts[0], b, preferred_element_type=F32)
    for part in a_parts[1:]:
        acc = acc + jnp.dot(part, b, preferred_element_type=F32)
    return acc


def _ssd_kernel(xbc_ref, z_ref, dt_ref, cw_ref, cb_ref, dtb_ref, alog_ref, dexp_ref,
                ng_ref, tri_ref, expand_ref, o_ref, xpad_ref, xc_ref, y_ref, state_ref):
    c = pl.program_id(0)
    L = CHUNK
    G, R, P, N = SSM_GROUPS, HEADS_PER_GROUP, HEAD_DIM, SSM_STATE
    GP = R * P

    @pl.when(c == 0)
    def _():
        state_ref[...] = jnp.zeros_like(state_ref)
        xpad_ref[:, L:L + SUBLANES, :] = jnp.zeros((XBC_COLS // LANES, SUBLANES, LANES), F32)

    xpad_ref[:, 0:SUBLANES, :] = xpad_ref[:, L:L + SUBLANES, :]

    for p in range(XBC_COLS // LANES):
        cs = slice(p * LANES, (p + 1) * LANES)
        xpad_ref[p, SUBLANES:SUBLANES + L, :] = xbc_ref[:, cs]
        acc = xpad_ref[p, SUBLANES - 3:SUBLANES - 3 + L, :] * cw_ref[0:1, cs]
        for k in range(1, SSM_CONV):
            acc = acc + xpad_ref[p, SUBLANES - 3 + k:SUBLANES - 3 + k + L, :] * cw_ref[k:k + 1, cs]
        acc = acc + cb_ref[:, cs]
        xc_ref[:, cs] = acc * jax.nn.sigmoid(acc)

    dt_raw = dt_ref[...] + dtb_ref[...]
    dt = jnp.maximum(dt_raw, 0.0) + jnp.log1p(jnp.exp(-jnp.abs(dt_raw)))
    a = -jnp.exp(alog_ref[...])
    da = dt * a
    a_cs = _dot_exact_rhs(tri_ref[...], _split3(da))
    a_cs_t = a_cs.T
    expand = expand_ref[...]
    acs_x = _dot_exact_lhs(_split3(a_cs), expand)
    dt_x = _dot_exact_lhs(_split3(dt), expand)

    li = lax.broadcasted_iota(jnp.int32, (L, L), 0)
    si = lax.broadcasted_iota(jnp.int32, (L, L), 1)
    causal = li >= si
    lane = lax.broadcasted_iota(jnp.int32, (1, LANES), 1)
    low_half = lane < P

    for g in range(G):
        xs = slice(g * GP, (g + 1) * GP)
        x_g = xc_ref[:, xs]
        b_g = xc_ref[:, D_SSM + g * N:D_SSM + (g + 1) * N].astype(BF16)
        c_g = xc_ref[:, D_SSM + G * N + g * N:D_SSM + G * N + (g + 1) * N].astype(BF16)
        acs_g = acs_x[:, xs]
        xdt_g = x_g * dt_x[:, xs]
        cb = lax.dot_general(c_g, b_g, (((1,), (1,)), ((), ())), preferred_element_type=F32)

        y_pairs = []
        for j in range(R // 2):
            ws = []
            for r in range(2):
                h = g * R + 2 * j + r
                seg = a_cs[:, h:h + 1] - a_cs_t[h:h + 1, :]
                decay = jnp.exp(jnp.where(causal, seg, -jnp.inf))
                ws.append((cb * decay).astype(BF16))
            wcat = jnp.concatenate(ws, axis=1)
            xp = xdt_g[:, j * LANES:(j + 1) * LANES]
            rhs = jnp.concatenate([jnp.where(low_half, xp, 0.0), jnp.where(low_half, 0.0, xp)],
                                  axis=0).astype(BF16)
            y_pairs.append(jnp.dot(wcat, rhs, preferred_element_type=F32))
        y_diag = jnp.concatenate(y_pairs, axis=1)

        a_last = acs_g[L - 1:L, :]
        xdte = (xdt_g * jnp.exp(a_last - acs_g)).astype(BF16)
        new_state = lax.dot_general(b_g, xdte, (((0,), (0,)), ((), ())),
                                    preferred_element_type=F32)
        prev = state_ref[g]
        y_off = jnp.dot(c_g, prev.astype(BF16), preferred_element_type=F32) * jnp.exp(acs_g)
        state_ref[g] = prev * jnp.exp(a_last) + new_state
        y_ref[:, xs] = y_diag + y_off + x_g * dexp_ref[:, xs]

    z = z_ref[...]
    y = y_ref[...] * (z * jax.nn.sigmoid(z))
    ms = jnp.mean(y * y, axis=-1, keepdims=True)
    o_ref[...] = (y * lax.rsqrt(ms + EPS) * ng_ref[...]).astype(o_ref.dtype)


def _ssd(xbc, z, dt, conv_w, conv_b, dt_bias, a_log, d_skip, norm_g):
    s = xbc.shape[0]
    L = CHUNK
    pad = LANES - N_SSM_HEADS
    dtb = jnp.pad(dt_bias, (0, pad)).reshape(1, LANES)
    alog = jnp.pad(a_log, (0, pad)).reshape(1, LANES)
    dexp = jnp.repeat(d_skip, HEAD_DIM).reshape(1, D_SSM)
    tri = (jnp.arange(L)[:, None] >= jnp.arange(L)[None, :]).astype(BF16)
    expand = (jnp.arange(LANES)[:, None] == (jnp.arange(D_SSM)[None, :] // HEAD_DIM)).astype(BF16)
    const = lambda c: (0, 0)
    return pl.pallas_call(
        _ssd_kernel,
        grid=(s // L,),
        in_specs=[pl.BlockSpec((L, XBC_COLS), lambda c: (c, 0)),
                  pl.BlockSpec((L, D_SSM), lambda c: (c, 0)),
                  pl.BlockSpec((L, LANES), lambda c: (c, 0)),
                  pl.BlockSpec((SSM_CONV, XBC_COLS), const),
                  pl.BlockSpec((1, XBC_COLS), const),
                  pl.BlockSpec((1, LANES), const),
                  pl.BlockSpec((1, LANES), const),
                  pl.BlockSpec((1, D_SSM), const),
                  pl.BlockSpec((1, D_SSM), const),
                  pl.BlockSpec((L, L), const),
                  pl.BlockSpec((LANES, D_SSM), const)],
        out_specs=pl.BlockSpec((L, D_SSM), lambda c: (c, 0)),
        out_shape=jax.ShapeDtypeStruct((s, D_SSM), BF16),
        scratch_shapes=[pltpu.VMEM((XBC_COLS // LANES, SUBLANES + L, LANES), F32),
                        pltpu.VMEM((L, XBC_COLS), F32),
                        pltpu.VMEM((L, D_SSM), F32),
                        pltpu.VMEM((SSM_GROUPS, SSM_STATE, HEADS_PER_GROUP * HEAD_DIM), F32)],
        compiler_params=_params(("arbitrary",)),
        name="ssd",
    )(xbc, z, dt, conv_w, conv_b.reshape(1, XBC_COLS), dtb, alog, dexp,
      norm_g.reshape(1, D_SSM), tri, expand)


def _out_proj_kernel(ya_ref, ys_ref, w_ref, res_ref, o_ref):
    acc = jnp.dot(ya_ref[...], w_ref[0:D_ATTN, :], preferred_element_type=F32)
    acc = acc + jnp.dot(ys_ref[...], w_ref[D_ATTN:D_ATTN + D_SSM, :], preferred_element_type=F32)
    o_ref[...] = res_ref[...] + acc


def _out_proj(ya, ys, w, res, bm=1024, bn=1024):
    m = ya.shape[0]
    k, n = w.shape
    return pl.pallas_call(
        _out_proj_kernel,
        grid=(n // bn, m // bm),
        in_specs=[pl.BlockSpec((bm, D_ATTN), lambda j, i: (i, 0)),
                  pl.BlockSpec((bm, D_SSM), lambda j, i: (i, 0)),
                  pl.BlockSpec((k, bn), lambda j, i: (0, j)),
                  pl.BlockSpec((bm, bn), lambda j, i: (i, j))],
        out_specs=pl.BlockSpec((bm, bn), lambda j, i: (i, j)),
        out_shape=jax.ShapeDtypeStruct((m, n), F32),
        compiler_params=_params(("arbitrary", "arbitrary")),
        name="out_proj",
    )(ya, ys, w, res)


def _ffn_up_kernel(x_ref, wg_ref, wu_ref, cw_ref, cb_ref, o_ref, ubuf_ref, wbf_ref):
    t = pl.program_id(0)
    n_steps = pl.num_programs(0)
    n_row_blocks = (n_steps - 1) // N_FFN_TILES
    bm = x_ref.shape[0]
    ft = FFN_TILE
    planes_per_half = ft // LANES
    row_block_new = jnp.minimum(t, n_steps - 2) % n_row_blocks

    @pl.when(t == 0)
    def _():
        ubuf_ref[...] = jnp.zeros_like(ubuf_ref)

    @pl.when((row_block_new == 0) & (t < n_steps - 1))
    def _():
        wbf_ref[0] = wg_ref[...].astype(BF16)
        wbf_ref[1] = wu_ref[...].astype(BF16)

    f_done = jnp.maximum(t - 1, 0) // n_row_blocks
    for q in range(planes_per_half):
        conv = []
        for half in range(2):
            plane = half * planes_per_half + q
            col = pl.multiple_of(half * D_FF + f_done * ft + q * LANES, LANES)
            acc = ubuf_ref[plane, SUBLANES - 2:SUBLANES - 2 + bm, :] * cw_ref[0:1, pl.ds(col, LANES)]
            acc = acc + ubuf_ref[plane, SUBLANES - 1:SUBLANES - 1 + bm, :] * cw_ref[1:2, pl.ds(col, LANES)]
            acc = acc + ubuf_ref[plane, SUBLANES:SUBLANES + bm, :] * cw_ref[2:3, pl.ds(col, LANES)]
            conv.append(acc + cb_ref[:, pl.ds(col, LANES)])
        gate, up = conv
        o_ref[:, q * LANES:(q + 1) * LANES] = (gate * jax.nn.sigmoid(gate) * up).astype(o_ref.dtype)
    last_rows = ubuf_ref[:, bm:bm + SUBLANES, :]
    ubuf_ref[:, 0:SUBLANES, :] = jnp.where(row_block_new == 0, jnp.zeros_like(last_rows), last_rows)

    x = x_ref[...]
    for half in range(2):
        u = jnp.dot(x, wbf_ref[half], preferred_element_type=F32)
        for q in range(planes_per_half):
            ubuf_ref[half * planes_per_half + q, SUBLANES:SUBLANES + bm, :] = u[:, q * LANES:(q + 1) * LANES]


def _ffn_up(hn, w, cw, cb, bm=1024):
    m, k = hn.shape
    nf = N_FFN_TILES
    nrb = m // bm
    n_tiles = nrb * nf
    n_planes = 2 * FFN_TILE // LANES
    cur = lambda t: jnp.minimum(t, n_tiles - 1)
    done = lambda t: jnp.maximum(t - 1, 0)
    return pl.pallas_call(
        _ffn_up_kernel,
        grid=(n_tiles + 1,),
        in_specs=[pl.BlockSpec((bm, k), lambda t: (cur(t) % nrb, 0)),
                  pl.BlockSpec((k, FFN_TILE), lambda t: (0, cur(t) // nrb)),
                  pl.BlockSpec((k, FFN_TILE), lambda t: (0, nf + cur(t) // nrb)),
                  pl.BlockSpec((FFN_CONV, 2 * D_FF), lambda t: (0, 0)),
                  pl.BlockSpec((1, 2 * D_FF), lambda t: (0, 0))],
        out_specs=pl.BlockSpec((bm, FFN_TILE), lambda t: (done(t) % nrb, done(t) // nrb)),
        out_shape=jax.ShapeDtypeStruct((m, D_FF), BF16),
        scratch_shapes=[pltpu.VMEM((n_planes, SUBLANES + bm, LANES), F32),
                        pltpu.VMEM((2, k, FFN_TILE), BF16)],
        compiler_params=_params(("arbitrary",)),
        name="ffn_up",
    )(hn, w, w, cw, cb)


def _ffn_down_kernel(a_ref, w_ref, res_ref, o_ref, obf_ref):
    h = res_ref[...] + jnp.dot(a_ref[...], w_ref[...], preferred_element_type=F32)
    o_ref[...] = h
    obf_ref[...] = h.astype(obf_ref.dtype)


def _ffn_down(act, w, res, bm=512, bn=512):
    m, k = act.shape
    n = w.shape[1]
    return pl.pallas_call(
        _ffn_down_kernel,
        grid=(n // bn, m // bm),
        in_specs=[pl.BlockSpec((bm, k), lambda j, i: (i, 0)),
                  pl.BlockSpec((k, bn), lambda j, i: (0, j)),
                  pl.BlockSpec((bm, bn), lambda j, i: (i, j))],
        out_specs=[pl.BlockSpec((bm, bn), lambda j, i: (i, j)),
                   pl.BlockSpec((bm, bn), lambda j, i: (i, j))],
        out_shape=[jax.ShapeDtypeStruct((m, n), F32), jax.ShapeDtypeStruct((m, n), BF16)],
        compiler_params=_params(("arbitrary", "arbitrary")),
        name="ffn_down",
    )(act, w, res)


def _ple_kernel(hb_ref, wg_ref, bg_ref, p_ref, wp_ref, h_ref, gf_ref, o_ref, ss_ref):
    j = pl.program_id(1)
    nj = pl.num_programs(1)
    bn = wg_ref.shape[1]

    @pl.when(j == 0)
    def _():
        ss_ref[...] = jnp.zeros_like(ss_ref)

    gate = jax.nn.sigmoid(jnp.dot(hb_ref[...], wg_ref[...], preferred_element_type=F32) + bg_ref[...])
    emb = jnp.dot(p_ref[...].astype(BF16), wp_ref[...], preferred_element_type=F32)
    h = h_ref[...] + gate * emb
    sq = h * h
    part = sq[:, 0:LANES]
    for c in range(1, bn // LANES):
        part = part + sq[:, c * LANES:(c + 1) * LANES]
    ss_ref[...] += part
    col = pl.multiple_of(j * bn, bn)
    o_ref[:, pl.ds(col, bn)] = h

    @pl.when(j == nj - 1)
    def _():
        ms = jnp.sum(ss_ref[...], axis=-1, keepdims=True) * (1.0 / D_MODEL)
        o_ref[...] = o_ref[...] * lax.rsqrt(ms + EPS) * gf_ref[...]


def _ple(hb, wg, bg, p, wp, h, gf, bm=512, bn=1024):
    m, d = h.shape
    return pl.pallas_call(
        _ple_kernel,
        grid=(m // bm, d // bn),
        in_specs=[pl.BlockSpec((bm, d), lambda i, j: (i, 0)),
                  pl.BlockSpec((d, bn), lambda i, j: (0, j)),
                  pl.BlockSpec((1, bn), lambda i, j: (0, j)),
                  pl.BlockSpec((bm, PLE_DIM), lambda i, j: (i, 0)),
                  pl.BlockSpec((PLE_DIM, bn), lambda i, j: (0, j)),
                  pl.BlockSpec((bm, bn), lambda i, j: (i, j)),
                  pl.BlockSpec((1, d), lambda i, j: (0, 0))],
        out_specs=pl.BlockSpec((bm, d), lambda i, j: (i, 0)),
        out_shape=jax.ShapeDtypeStruct((m, d), F32),
        scratch_shapes=[pltpu.VMEM((bm, LANES), F32)],
        compiler_params=_params(("arbitrary", "arbitrary"), vmem=VMEM_LIMIT_HIGH),
        name="ple",
    )(hb, wg, bg, p, wp, h, gf)


def kernel(x, p, positions, norm_mix_g, w_in, attn_sinks, ssm_conv_w, ssm_conv_b, ssm_dt_bias,
           ssm_a_log, ssm_d, ssm_norm_g, w_o, norm_ffn_g, w_up, ffn_conv_w, ffn_conv_b, w_down,
           ple_gate_w, ple_gate_b, ple_proj, norm_final_g):
    b, s, d = x.shape
    assert (b, s, d) == (1, SEQ, D_MODEL) and w_in.shape[0] == 1
    h = x.reshape(s, d)
    pos = positions.reshape(s, 1)
    half = HEAD_DIM // 2
    inv_freq = ROPE_THETA ** (-jnp.arange(half, dtype=F32) / half)
    freq = jnp.tile(inv_freq, LANES // half).reshape(1, LANES)
    w_dt = jnp.pad(w_in[0][:, DT_COL0:DT_COL0 + N_SSM_HEADS], ((0, 0), (0, LANES - N_SSM_HEADS)))

    hn = _norm_cast(h, norm_mix_g[0])
    qkv, z, xbc, dt = _in_proj(hn, w_in[0].astype(BF16), w_dt.astype(BF16))
    y_attn = _swa(qkv, pos, freq, attn_sinks[0])
    y_ssm = _ssd(xbc, z, dt, ssm_conv_w[0], ssm_conv_b[0], ssm_dt_bias[0], ssm_a_log[0], ssm_d[0],
                 ssm_norm_g[0])
    h1 = _out_proj(y_attn, y_ssm, w_o[0].astype(BF16), h)
    hn2 = _norm_cast(h1, norm_ffn_g[0])
    act = _ffn_up(hn2, w_up[0], ffn_conv_w[0], ffn_conv_b[0].reshape(1, 2 * D_FF))
    h2, h2b = _ffn_down(act, w_down[0].astype(BF16), h1)
    out = _ple(h2b, ple_gate_w[0].astype(BF16), ple_gate_b[0].reshape(1, d), p[0].reshape(s, PLE_DIM),
               ple_proj[0].astype(BF16), h2, norm_final_g.reshape(1, d))
    return out.reshape(b, s, d)
```

```python
import jax
import jax.numpy as jnp
from jax import lax
from jax.experimental import pallas as pl
from jax.experimental.pallas import tpu as pltpu

F32 = jnp.float32
BF16 = jnp.bfloat16

D_MODEL = 4096
SEQ = 16384
PLE_DIM = 256
D_ATTN = 2048
D_SSM = 2048
HEAD_DIM = 64
N_Q_HEADS = 32
N_KV_HEADS = 4
Q_PER_KV = 8
WINDOW = 128
ROPE_THETA = 10000.0
N_SSM_HEADS = 32
SSM_STATE = 128
SSM_GROUPS = 8
HEADS_PER_GROUP = 4
SSM_CONV = 4
CHUNK = 128
D_FF = 11008
FFN_CONV = 3
EPS = 1e-6
KV_COLS = N_KV_HEADS * HEAD_DIM
QKV_COLS = D_ATTN + 2 * KV_COLS
XBC_COLS = D_SSM + 2 * SSM_GROUPS * SSM_STATE
DT_COL0 = QKV_COLS + D_SSM + XBC_COLS

LANES = 128
SUBLANES = 8
VMEM_LIMIT = 56 * 1024 * 1024
VMEM_LIMIT_HIGH = 62 * 1024 * 1024

QKV_TILE = QKV_COLS // 2
SSM_TILE = 1024
FFN_TILE = 256
N_FFN_TILES = D_FF // FFN_TILE


def _params(sem, vmem=VMEM_LIMIT, flags=None):
    return pltpu.CompilerParams(dimension_semantics=sem, vmem_limit_bytes=vmem, flags=flags)


def _norm_cast_kernel(x_ref, g_ref, o_ref):
    x = x_ref[...]
    ms = jnp.mean(x * x, axis=-1, keepdims=True)
    o_ref[...] = (x * lax.rsqrt(ms + EPS) * g_ref[...]).astype(o_ref.dtype)


def _norm_cast(x, g, bm=512):
    m, d = x.shape
    return pl.pallas_call(
        _norm_cast_kernel,
        grid=(m // bm,),
        in_specs=[pl.BlockSpec((bm, d), lambda i: (i, 0)),
                  pl.BlockSpec((1, d), lambda i: (0, 0))],
        out_specs=pl.BlockSpec((bm, d), lambda i: (i, 0)),
        out_shape=jax.ShapeDtypeStruct((m, d), BF16),
        compiler_params=_params(("arbitrary",)),
        name="norm_cast",
    )(x, g.reshape(1, d))


def _proj_kernel(x_ref, w_ref, o_ref):
    o_ref[...] = jnp.dot(x_ref[...], w_ref[...], preferred_element_type=F32)


def _proj_dt_kernel(x_ref, w_ref, wdt_ref, o_ref, dt_ref):
    o_ref[...] = jnp.dot(x_ref[...], w_ref[...], preferred_element_type=F32)

    @pl.when(pl.program_id(1) == 0)
    def _():
        dt_ref[...] = jnp.dot(x_ref[...], wdt_ref[...], preferred_element_type=F32)


def _proj(hn, w, bn, name, w_dt=None, bm=1024):
    m, k = hn.shape
    n = w.shape[1]
    in_specs = [pl.BlockSpec((bm, k), lambda i, j: (i, 0)),
                pl.BlockSpec((k, bn), lambda i, j: (0, j))]
    out_specs = [pl.BlockSpec((bm, bn), lambda i, j: (i, j))]
    out_shape = [jax.ShapeDtypeStruct((m, n), F32)]
    args = [hn, w]
    if w_dt is not None:
        in_specs.append(pl.BlockSpec((k, LANES), lambda i, j: (0, 0)))
        out_specs.append(pl.BlockSpec((bm, LANES), lambda i, j: (i, 0)))
        out_shape.append(jax.ShapeDtypeStruct((m, LANES), F32))
        args.append(w_dt)
    return pl.pallas_call(
        _proj_kernel if w_dt is None else _proj_dt_kernel,
        grid=(m // bm, n // bn),
        in_specs=in_specs,
        out_specs=out_specs,
        out_shape=out_shape,
        compiler_params=_params(("arbitrary", "arbitrary")),
        name=name,
    )(*args)


def _rope_tile(x, cos, sin_signed, first_half):
    rot = jnp.where(first_half, pltpu.roll(x, LANES - HEAD_DIM // 2, 1),
                    pltpu.roll(x, HEAD_DIM // 2, 1))
    return x * cos + rot * sin_signed


def _swa_kernel(pos_ref, freq_ref, sink_ref, q_ref, k_ref, v_ref, o_ref, kband_ref, vband_ref):
    n = pl.program_id(0)
    w = WINDOW

    lane = lax.broadcasted_iota(jnp.int32, (1, LANES), 1)
    low_half = lane < HEAD_DIM
    first_half = (lane % HEAD_DIM) < (HEAD_DIM // 2)

    @pl.when(n == 0)
    def _():
        kband_ref[...] = jnp.zeros_like(kband_ref)
        ones_lo = jnp.broadcast_to(jnp.where(low_half, 1.0, 0.0), (2 * w, LANES)).astype(BF16)
        ones_hi = jnp.broadcast_to(jnp.where(low_half, 0.0, 1.0), (2 * w, LANES)).astype(BF16)
        for kh in range(N_KV_HEADS):
            vband_ref[kh, :, 0:LANES] = jnp.zeros((4 * w, LANES), BF16)
            vband_ref[kh, 0:2 * w, LANES:2 * LANES] = ones_lo
            vband_ref[kh, 2 * w:4 * w, LANES:2 * LANES] = ones_hi

    ang = pos_ref[...].astype(F32) * freq_ref[...]
    cos = jnp.cos(ang)
    sin_signed = jnp.where(first_half, -jnp.sin(ang), jnp.sin(ang))

    for kh in range(N_KV_HEADS):
        kband_ref[kh, 0:w, :] = kband_ref[kh, w:2 * w, :]
        vband_ref[kh, 0:w, 0:LANES] = vband_ref[kh, w:2 * w, 0:LANES]
        vband_ref[kh, 2 * w:3 * w, 0:LANES] = vband_ref[kh, 3 * w:4 * w, 0:LANES]
    for t in range(KV_COLS // LANES):
        k_rot = _rope_tile(k_ref[:, t * LANES:(t + 1) * LANES], cos, sin_signed, first_half)
        k_sw = pltpu.roll(k_rot, HEAD_DIM, 1)
        v_t = v_ref[:, t * LANES:(t + 1) * LANES]
        v_sw = pltpu.roll(v_t, HEAD_DIM, 1)
        zero = jnp.zeros_like(v_t)
        kband_ref[2 * t, w:2 * w, :] = jnp.where(low_half, k_rot, k_sw).astype(BF16)
        kband_ref[2 * t + 1, w:2 * w, :] = jnp.where(low_half, k_sw, k_rot).astype(BF16)
        vband_ref[2 * t, w:2 * w, 0:LANES] = jnp.where(low_half, v_t, zero).astype(BF16)
        vband_ref[2 * t, 3 * w:4 * w, 0:LANES] = jnp.where(low_half, zero, v_sw).astype(BF16)
        vband_ref[2 * t + 1, w:2 * w, 0:LANES] = jnp.where(low_half, v_sw, zero).astype(BF16)
        vband_ref[2 * t + 1, 3 * w:4 * w, 0:LANES] = jnp.where(low_half, zero, v_t).astype(BF16)

    qi = lax.broadcasted_iota(jnp.int32, (w, 2 * w), 0)
    si = lax.broadcasted_iota(jnp.int32, (w, 2 * w), 1)
    prev_off = jnp.where(n > 0, 0, 2 * w)
    valid = ((si < w) & (si > qi + prev_off)) | ((si >= w) & (si - w <= qi))
    scale = HEAD_DIM ** -0.5

    for kh in range(N_KV_HEADS):
        kb = kband_ref[kh]
        vb = vband_ref[kh]
        for j in range(Q_PER_KV // 2):
            t = kh * (Q_PER_KV // 2) + j
            q_rot = _rope_tile(q_ref[:, t * LANES:(t + 1) * LANES], cos, sin_signed, first_half) * scale
            probs = []
            sink_terms = []
            for r in range(2):
                keep = low_half if r == 0 else jnp.logical_not(low_half)
                lhs = jnp.where(keep, q_rot, 0.0).astype(BF16)
                s = lax.dot_general(lhs, kb, (((1,), (1,)), ((), ())), preferred_element_type=F32)
                s = jnp.where(valid, s, -jnp.inf)
                sink = sink_ref[2 * t + r]
                m = jnp.max(jnp.maximum(s[:, 0:w], s[:, w:2 * w]), axis=-1, keepdims=True)
                m = jnp.maximum(m, sink)
                probs.append(jnp.exp(s - m).astype(BF16))
                sink_terms.append(jnp.exp(sink - m))
            od = jnp.dot(jnp.concatenate(probs, axis=1), vb, preferred_element_type=F32)
            denom = od[:, LANES:2 * LANES] + jnp.where(low_half, sink_terms[0], sink_terms[1])
            o_ref[:, t * LANES:(t + 1) * LANES] = (od[:, 0:LANES] / denom).astype(o_ref.dtype)


def _swa(qkv, pos, freq, sinks):
    s = qkv.shape[0]
    w = WINDOW
    return pl.pallas_call(
        _swa_kernel,
        grid=(s // w,),
        in_specs=[pl.BlockSpec((w, 1), lambda n: (n, 0)),
                  pl.BlockSpec((1, LANES), lambda n: (0, 0)),
                  pl.BlockSpec(memory_space=pltpu.SMEM),
                  pl.BlockSpec((w, D_ATTN), lambda n: (n, 0)),
                  pl.BlockSpec((w, KV_COLS), lambda n: (n, D_ATTN // KV_COLS)),
                  pl.BlockSpec((w, KV_COLS), lambda n: (n, D_ATTN // KV_COLS + 1))],
        out_specs=pl.BlockSpec((w, D_ATTN), lambda n: (n, 0)),
        out_shape=jax.ShapeDtypeStruct((s, D_ATTN), BF16),
        scratch_shapes=[pltpu.VMEM((N_KV_HEADS, 2 * w, LANES), BF16),
                        pltpu.VMEM((N_KV_HEADS, 4 * w, 2 * LANES), BF16)],
        compiler_params=_params(("arbitrary",)),
        name="swa",
    )(pos, freq, sinks, qkv, qkv, qkv)


def _split3(v):
    hi = v.astype(BF16)
    r = v - hi.astype(F32)
    mid = r.astype(BF16)
    lo = (r - mid.astype(F32)).astype(BF16)
    return hi, mid, lo


def _dot_exact_rhs(a, b_parts):
    acc = jnp.dot(a, b_parts[0], preferred_element_type=F32)
    for part in b_parts[1:]:
        acc = acc + jnp.dot(a, part, preferred_element_type=F32)
    return acc


def _dot_exact_lhs(a_parts, b):
    acc = jnp.dot(a_parts[0], b, preferred_element_type=F32)
    for part in a_parts[1:]:
        acc = acc + jnp.dot(part, b, preferred_element_type=F32)
    return acc


def _ssd_kernel(xbc_ref, z_ref, dt_ref, cw_ref, cb_ref, dtb_ref, alog_ref, dexp_ref,
                ng_ref, tri_ref, expand_ref, o_ref, xpad_ref, xc_ref, y_ref, state_ref):
    c = pl.program_id(0)
    L = CHUNK
    G, R, P, N = SSM_GROUPS, HEADS_PER_GROUP, HEAD_DIM, SSM_STATE
    GP = R * P

    @pl.when(c == 0)
    def _():
        state_ref[...] = jnp.zeros_like(state_ref)
        xpad_ref[:, L:L + SUBLANES, :] = jnp.zeros((XBC_COLS // LANES, SUBLANES, LANES), F32)

    xpad_ref[:, 0:SUBLANES, :] = xpad_ref[:, L:L + SUBLANES, :]

    for p in range(XBC_COLS // LANES):
        cs = slice(p * LANES, (p + 1) * LANES)
        xpad_ref[p, SUBLANES:SUBLANES + L, :] = xbc_ref[:, cs]
        acc = xpad_ref[p, SUBLANES - 3:SUBLANES - 3 + L, :] * cw_ref[0:1, cs]
        for k in range(1, SSM_CONV):
            acc = acc + xpad_ref[p, SUBLANES - 3 + k:SUBLANES - 3 + k + L, :] * cw_ref[k:k + 1, cs]
        acc = acc + cb_ref[:, cs]
        xc_ref[:, cs] = acc * jax.nn.sigmoid(acc)

    dt_raw = dt_ref[...] + dtb_ref[...]
    dt = jnp.maximum(dt_raw, 0.0) + jnp.log1p(jnp.exp(-jnp.abs(dt_raw)))
    a = -jnp.exp(alog_ref[...])
    da = dt * a
    a_cs = _dot_exact_rhs(tri_ref[...], _split3(da))
    a_cs_t = a_cs.T
    expand = expand_ref[...]
    acs_x = _dot_exact_lhs(_split3(a_cs), expand)
    dt_x = _dot_exact_lhs(_split3(dt), expand)

    li = lax.broadcasted_iota(jnp.int32, (L, L), 0)
    si = lax.broadcasted_iota(jnp.int32, (L, L), 1)
    causal = li >= si
    lane = lax.broadcasted_iota(jnp.int32, (1, LANES), 1)
    low_half = lane < P

    for g in range(G):
        xs = slice(g * GP, (g + 1) * GP)
        x_g = xc_ref[:, xs]
        b_g = xc_ref[:, D_SSM + g * N:D_SSM + (g + 1) * N].astype(BF16)
        c_g = xc_ref[:, D_SSM + G * N + g * N:D_SSM + G * N + (g + 1) * N].astype(BF16)
        acs_g = acs_x[:, xs]
        xdt_g = x_g * dt_x[:, xs]
        cb = lax.dot_general(c_g, b_g, (((1,), (1,)), ((), ())), preferred_element_type=F32)

        y_pairs = []
        for j in range(R // 2):
            ws = []
            for r in range(2):
                h = g * R + 2 * j + r
                seg = a_cs[:, h:h + 1] - a_cs_t[h:h + 1, :]
                decay = jnp.exp(jnp.where(causal, seg, -jnp.inf))
                ws.append((cb * decay).astype(BF16))
            wcat = jnp.concatenate(ws, axis=1)
            xp = xdt_g[:, j * LANES:(j + 1) * LANES]
            rhs = jnp.concatenate([jnp.where(low_half, xp, 0.0), jnp.where(low_half, 0.0, xp)],
                                  axis=0).astype(BF16)
            y_pairs.append(jnp.dot(wcat, rhs, preferred_element_type=F32))
        y_diag = jnp.concatenate(y_pairs, axis=1)

        a_last = acs_g[L - 1:L, :]
        xdte = (xdt_g * jnp.exp(a_last - acs_g)).astype(BF16)
        new_state = lax.dot_general(b_g, xdte, (((0,), (0,)), ((), ())),
                                    preferred_element_type=F32)
        prev = state_ref[g]
        y_off = jnp.dot(c_g, prev.astype(BF16), preferred_element_type=F32) * jnp.exp(acs_g)
        state_ref[g] = prev * jnp.exp(a_last) + new_state
        y_ref[:, xs] = y_diag + y_off + x_g * dexp_ref[:, xs]

    z = z_ref[...]
    y = y_ref[...] * (z * jax.nn.sigmoid(z))
    ms = jnp.mean(y * y, axis=-1, keepdims=True)
    o_ref[...] = (y * lax.rsqrt(ms + EPS) * ng_ref[...]).astype(o_ref.dtype)


def _ssd(xbc, z, dt, conv_w, conv_b, dt_bias, a_log, d_skip, norm_g):
    s = xbc.shape[0]
    L = CHUNK
    pad = LANES - N_SSM_HEADS
    dtb = jnp.pad(dt_bias, (0, pad)).reshape(1, LANES)
    alog = jnp.pad(a_log, (0, pad)).reshape(1, LANES)
    dexp = jnp.repeat(d_skip, HEAD_DIM).reshape(1, D_SSM)
    tri = (jnp.arange(L)[:, None] >= jnp.arange(L)[None, :]).astype(BF16)
    expand = (jnp.arange(LANES)[:, None] == (jnp.arange(D_SSM)[None, :] // HEAD_DIM)).astype(BF16)
    const = lambda c: (0, 0)
    return pl.pallas_call(
        _ssd_kernel,
        grid=(s // L,),
        in_specs=[pl.BlockSpec((L, XBC_COLS), lambda c: (c, 0)),
                  pl.BlockSpec((L, D_SSM), lambda c: (c, 0)),
                  pl.BlockSpec((L, LANES), lambda c: (c, 0)),
                  pl.BlockSpec((SSM_CONV, XBC_COLS), const),
                  pl.BlockSpec((1, XBC_COLS), const),
                  pl.BlockSpec((1, LANES), const),
                  pl.BlockSpec((1, LANES), const),
                  pl.BlockSpec((1, D_SSM), const),
                  pl.BlockSpec((1, D_SSM), const),
                  pl.BlockSpec((L, L), const),
                  pl.BlockSpec((LANES, D_SSM), const)],
        out_specs=pl.BlockSpec((L, D_SSM), lambda c: (c, 0)),
        out_shape=jax.ShapeDtypeStruct((s, D_SSM), BF16),
        scratch_shapes=[pltpu.VMEM((XBC_COLS // LANES, SUBLANES + L, LANES), F32),
                        pltpu.VMEM((L, XBC_COLS), F32),
                        pltpu.VMEM((L, D_SSM), F32),
                        pltpu.VMEM((SSM_GROUPS, SSM_STATE, HEADS_PER_GROUP * HEAD_DIM), F32)],
        compiler_params=_params(("arbitrary",)),
        name="ssd",
    )(xbc, z, dt, conv_w, conv_b.reshape(1, XBC_COLS), dtb, alog, dexp,
      norm_g.reshape(1, D_SSM), tri, expand)


def _out_proj_kernel(ya_ref, ys_ref, w_ref, res_ref, g_ref, o_ref, hg_ref, r_ref, ss_ref):
    j = pl.program_id(1)
    bn = w_ref.shape[1]

    @pl.when(j == 0)
    def _():
        ss_ref[...] = jnp.zeros_like(ss_ref)

    acc = jnp.dot(ya_ref[...], w_ref[0:D_ATTN, :], preferred_element_type=F32)
    acc = acc + jnp.dot(ys_ref[...], w_ref[D_ATTN:D_ATTN + D_SSM, :], preferred_element_type=F32)
    h = res_ref[...] + acc
    o_ref[...] = h
    hg_ref[...] = (h * g_ref[...]).astype(hg_ref.dtype)
    sq = h * h
    part = sq[:, 0:LANES]
    for c in range(1, bn // LANES):
        part = part + sq[:, c * LANES:(c + 1) * LANES]
    ss_ref[...] += part

    @pl.when(j == pl.num_programs(1) - 1)
    def _():
        ms = jnp.sum(ss_ref[...], axis=-1, keepdims=True) * (1.0 / D_MODEL)
        r_ref[...] = lax.rsqrt(ms + EPS)


def _out_proj(ya, ys, w, res, g, bm=512, bn=1024):
    m = ya.shape[0]
    k, n = w.shape
    return pl.pallas_call(
        _out_proj_kernel,
        grid=(m // bm, n // bn),
        in_specs=[pl.BlockSpec((bm, D_ATTN), lambda i, j: (i, 0)),
                  pl.BlockSpec((bm, D_SSM), lambda i, j: (i, 0)),
                  pl.BlockSpec((k, bn), lambda i, j: (0, j)),
                  pl.BlockSpec((bm, bn), lambda i, j: (i, j)),
                  pl.BlockSpec((1, bn), lambda i, j: (0, j))],
        out_specs=[pl.BlockSpec((bm, bn), lambda i, j: (i, j)),
                   pl.BlockSpec((bm, bn), lambda i, j: (i, j)),
                   pl.BlockSpec((bm, 1), lambda i, j: (i, 0))],
        out_shape=[jax.ShapeDtypeStruct((m, n), F32),
                   jax.ShapeDtypeStruct((m, n), BF16),
                   jax.ShapeDtypeStruct((m, 1), F32)],
        scratch_shapes=[pltpu.VMEM((bm, LANES), F32)],
        compiler_params=_params(("arbitrary", "arbitrary")),
        name="out_proj",
    )(ya, ys, w, res, g.reshape(1, n))


def _ffn_up_kernel(x_ref, r_ref, wg_ref, wu_ref, cw_ref, cb_ref, o_ref, ubuf_ref, wbf_ref):
    t = pl.program_id(0)
    n_steps = pl.num_programs(0)
    n_row_blocks = (n_steps - 1) // N_FFN_TILES
    bm = x_ref.shape[0]
    ft = FFN_TILE
    planes_per_half = ft // LANES
    row_block_new = jnp.minimum(t, n_steps - 2) % n_row_blocks

    @pl.when(t == 0)
    def _():
        ubuf_ref[...] = jnp.zeros_like(ubuf_ref)

    @pl.when((row_block_new == 0) & (t < n_steps - 1))
    def _():
        wbf_ref[0] = wg_ref[...].astype(BF16)
        wbf_ref[1] = wu_ref[...].astype(BF16)

    f_done = jnp.maximum(t - 1, 0) // n_row_blocks
    for q in range(planes_per_half):
        conv = []
        for half in range(2):
            plane = half * planes_per_half + q
            col = pl.multiple_of(half * D_FF + f_done * ft + q * LANES, LANES)
            acc = ubuf_ref[plane, SUBLANES - 2:SUBLANES - 2 + bm, :] * cw_ref[0:1, pl.ds(col, LANES)]
            acc = acc + ubuf_ref[plane, SUBLANES - 1:SUBLANES - 1 + bm, :] * cw_ref[1:2, pl.ds(col, LANES)]
            acc = acc + ubuf_ref[plane, SUBLANES:SUBLANES + bm, :] * cw_ref[2:3, pl.ds(col, LANES)]
            conv.append(acc + cb_ref[:, pl.ds(col, LANES)])
        gate, up = conv
        o_ref[:, q * LANES:(q + 1) * LANES] = (gate * jax.nn.sigmoid(gate) * up).astype(o_ref.dtype)
    last_rows = ubuf_ref[:, bm:bm + SUBLANES, :]
    ubuf_ref[:, 0:SUBLANES, :] = jnp.where(row_block_new == 0, jnp.zeros_like(last_rows), last_rows)

    x = x_ref[...]
    r = r_ref[...]
    for half in range(2):
        u = jnp.dot(x, wbf_ref[half], preferred_element_type=F32) * r
        for q in range(planes_per_half):
            ubuf_ref[half * planes_per_half + q, SUBLANES:SUBLANES + bm, :] = u[:, q * LANES:(q + 1) * LANES]


def _ffn_up(hg, r, w, cw, cb, bm=1024):
    m, k = hg.shape
    nf = N_FFN_TILES
    nrb = m // bm
    n_tiles = nrb * nf
    n_planes = 2 * FFN_TILE // LANES
    cur = lambda t: jnp.minimum(t, n_tiles - 1)
    done = lambda t: jnp.maximum(t - 1, 0)
    return pl.pallas_call(
        _ffn_up_kernel,
        grid=(n_tiles + 1,),
        in_specs=[pl.BlockSpec((bm, k), lambda t: (cur(t) % nrb, 0)),
                  pl.BlockSpec((bm, 1), lambda t: (cur(t) % nrb, 0)),
                  pl.BlockSpec((k, FFN_TILE), lambda t: (0, cur(t) // nrb)),
                  pl.BlockSpec((k, FFN_TILE), lambda t: (0, nf + cur(t) // nrb)),
                  pl.BlockSpec((FFN_CONV, 2 * D_FF), lambda t: (0, 0)),
                  pl.BlockSpec((1, 2 * D_FF), lambda t: (0, 0))],
        out_specs=pl.BlockSpec((bm, FFN_TILE), lambda t: (done(t) % nrb, done(t) // nrb)),
        out_shape=jax.ShapeDtypeStruct((m, D_FF), BF16),
        scratch_shapes=[pltpu.VMEM((n_planes, SUBLANES + bm, LANES), F32),
                        pltpu.VMEM((2, k, FFN_TILE), BF16)],
        compiler_params=_params(("arbitrary",)),
        name="ffn_up",
    )(hg, r, w, w, cw, cb)


def _ffn_down_kernel(a_ref, w_ref, res_ref, o_ref, obf_ref):
    h = res_ref[...] + jnp.dot(a_ref[...], w_ref[...], preferred_element_type=F32)
    o_ref[...] = h
    obf_ref[...] = h.astype(obf_ref.dtype)


def _ffn_down(act, w, res, bm=512, bn=512):
    m, k = act.shape
    n = w.shape[1]
    return pl.pallas_call(
        _ffn_down_kernel,
        grid=(n // bn, m // bm),
        in_specs=[pl.BlockSpec((bm, k), lambda j, i: (i, 0)),
                  pl.BlockSpec((k, bn), lambda j, i: (0, j)),
                  pl.BlockSpec((bm, bn), lambda j, i: (i, j))],
        out_specs=[pl.BlockSpec((bm, bn), lambda j, i: (i, j)),
                   pl.BlockSpec((bm, bn), lambda j, i: (i, j))],
        out_shape=[jax.ShapeDtypeStruct((m, n), F32), jax.ShapeDtypeStruct((m, n), BF16)],
        compiler_params=_params(("arbitrary", "arbitrary")),
        name="ffn_down",
    )(act, w, res)


def _ple_kernel(hb_ref, wg_ref, bg_ref, p_ref, wp_ref, h_ref, gf_ref, o_ref, ss_ref):
    j = pl.program_id(1)
    nj = pl.num_programs(1)
    bn = wg_ref.shape[1]

    @pl.when(j == 0)
    def _():
        ss_ref[...] = jnp.zeros_like(ss_ref)

    gate = jax.nn.sigmoid(jnp.dot(hb_ref[...], wg_ref[...], preferred_element_type=F32) + bg_ref[...])
    emb = jnp.dot(p_ref[...].astype(BF16), wp_ref[...], preferred_element_type=F32)
    h = h_ref[...] + gate * emb
    sq = h * h
    part = sq[:, 0:LANES]
    for c in range(1, bn // LANES):
        part = part + sq[:, c * LANES:(c + 1) * LANES]
    ss_ref[...] += part
    col = pl.multiple_of(j * bn, bn)
    o_ref[:, pl.ds(col, bn)] = h

    @pl.when(j == nj - 1)
    def _():
        ms = jnp.sum(ss_ref[...], axis=-1, keepdims=True) * (1.0 / D_MODEL)
        o_ref[...] = o_ref[...] * lax.rsqrt(ms + EPS) * gf_ref[...]


def _ple(hb, wg, bg, p, wp, h, gf, bm=512, bn=1024):
    m, d = h.shape
    return pl.pallas_call(
        _ple_kernel,
        grid=(m // bm, d // bn),
        in_specs=[pl.BlockSpec((bm, d), lambda i, j: (i, 0)),
                  pl.BlockSpec((d, bn), lambda i, j: (0, j)),
                  pl.BlockSpec((1, bn), lambda i, j: (0, j)),
                  pl.BlockSpec((bm, PLE_DIM), lambda i, j: (i, 0)),
                  pl.BlockSpec((PLE_DIM, bn), lambda i, j: (0, j)),
                  pl.BlockSpec((bm, bn), lambda i, j: (i, j)),
                  pl.BlockSpec((1, d), lambda i, j: (0, 0))],
        out_specs=pl.BlockSpec((bm, d), lambda i, j: (i, 0)),
        out_shape=jax.ShapeDtypeStruct((m, d), F32),
        scratch_shapes=[pltpu.VMEM((bm, LANES), F32)],
        compiler_params=_params(("arbitrary", "arbitrary"), vmem=VMEM_LIMIT_HIGH),
        name="ple",
    )(hb, wg, bg, p, wp, h, gf)


def kernel(x, p, positions, norm_mix_g, w_in, attn_sinks, ssm_conv_w, ssm_conv_b, ssm_dt_bias,
           ssm_a_log, ssm_d, ssm_norm_g, w_o, norm_ffn_g, w_up, ffn_conv_w, ffn_conv_b, w_down,
           ple_gate_w, ple_gate_b, ple_proj, norm_final_g):
    b, s, d = x.shape
    assert (b, s, d) == (1, SEQ, D_MODEL) and w_in.shape[0] == 1
    h = x.reshape(s, d)
    pos = positions.reshape(s, 1)
    half = HEAD_DIM // 2
    inv_freq = ROPE_THETA ** (-jnp.arange(half, dtype=F32) / half)
    freq = jnp.tile(inv_freq, LANES // half).reshape(1, LANES)
    w_dt = jnp.pad(w_in[0][:, DT_COL0:DT_COL0 + N_SSM_HEADS], ((0, 0), (0, LANES - N_SSM_HEADS)))

    hn = _norm_cast(h, norm_mix_g[0])
    w_qkv = w_in[0][:, 0:QKV_COLS].astype(BF16)
    w_z = w_in[0][:, QKV_COLS:QKV_COLS + D_SSM].astype(BF16)
    w_xbc = w_in[0][:, QKV_COLS + D_SSM:DT_COL0].astype(BF16)
    qkv, dt = _proj(hn, w_qkv, QKV_TILE, "in_proj_qkv", w_dt=w_dt.astype(BF16))
    z, = _proj(hn, w_z, SSM_TILE, "in_proj_z")
    xbc, = _proj(hn, w_xbc, SSM_TILE, "in_proj_xbc")
    y_attn = _swa(qkv, pos, freq, attn_sinks[0])
    y_ssm = _ssd(xbc, z, dt, ssm_conv_w[0], ssm_conv_b[0], ssm_dt_bias[0], ssm_a_log[0], ssm_d[0],
                 ssm_norm_g[0])
    h1, hg, r = _out_proj(y_attn, y_ssm, w_o[0].astype(BF16), h, norm_ffn_g[0])
    act = _ffn_up(hg, r, w_up[0], ffn_conv_w[0], ffn_conv_b[0].reshape(1, 2 * D_FF))
    h2, h2b = _ffn_down(act, w_down[0].astype(BF16), h1)
    out = _ple(h2b, ple_gate_w[0].astype(BF16), ple_gate_b[0].reshape(1, d), p[0].reshape(s, PLE_DIM),
               ple_proj[0].astype(BF16), h2, norm_final_g.reshape(1, d))
    return out.reshape(b, s, d)
```

```python
import jax
import jax.numpy as jnp
from jax import lax
from jax.experimental import pallas as pl
from jax.experimental.pallas import tpu as pltpu

F32 = jnp.float32
BF16 = jnp.bfloat16

D_MODEL = 4096
SEQ = 16384
PLE_DIM = 256
D_ATTN = 2048
D_SSM = 2048
HEAD_DIM = 64
N_Q_HEADS = 32
N_KV_HEADS = 4
Q_PER_KV = 8
WINDOW = 128
ROPE_THETA = 10000.0
N_SSM_HEADS = 32
SSM_STATE = 128
SSM_GROUPS = 8
HEADS_PER_GROUP = 4
SSM_CONV = 4
CHUNK = 128
D_FF = 11008
FFN_CONV = 3
EPS = 1e-6
KV_COLS = N_KV_HEADS * HEAD_DIM
QKV_COLS = D_ATTN + 2 * KV_COLS
XBC_COLS = D_SSM + 2 * SSM_GROUPS * SSM_STATE
DT_COL0 = QKV_COLS + D_SSM + XBC_COLS

LANES = 128
SUBLANES = 8
VMEM_LIMIT = 56 * 1024 * 1024
VMEM_LIMIT_HIGH = 62 * 1024 * 1024

QKV_TILE = QKV_COLS // 2
SSM_TILE = 1024
FFN_TILE = 256
N_FFN_TILES = D_FF // FFN_TILE


def _params(sem, vmem=VMEM_LIMIT, flags=None):
    return pltpu.CompilerParams(dimension_semantics=sem, vmem_limit_bytes=vmem, flags=flags)


def _norm_cast_kernel(x_ref, g_ref, o_ref):
    x = x_ref[...]
    ms = jnp.mean(x * x, axis=-1, keepdims=True)
    o_ref[...] = (x * lax.rsqrt(ms + EPS) * g_ref[...]).astype(o_ref.dtype)


def _norm_cast(x, g, bm=512):
    m, d = x.shape
    return pl.pallas_call(
        _norm_cast_kernel,
        grid=(m // bm,),
        in_specs=[pl.BlockSpec((bm, d), lambda i: (i, 0)),
                  pl.BlockSpec((1, d), lambda i: (0, 0))],
        out_specs=pl.BlockSpec((bm, d), lambda i: (i, 0)),
        out_shape=jax.ShapeDtypeStruct((m, d), BF16),
        compiler_params=_params(("arbitrary",)),
        name="norm_cast",
    )(x, g.reshape(1, d))


W_IN_CAST_TILE = 512
W_IN_QKV_TILES = QKV_COLS // W_IN_CAST_TILE
W_IN_Z_TILES = D_SSM // W_IN_CAST_TILE
W_IN_XBC_TILES = XBC_COLS // W_IN_CAST_TILE


def _split_w_in_kernel(w_ref, qkv_ref, z_ref, xbc_ref):
    j = pl.program_id(0)

    @pl.when(j < W_IN_QKV_TILES)
    def _():
        qkv_ref[...] = w_ref[...].astype(qkv_ref.dtype)

    @pl.when((j >= W_IN_QKV_TILES) & (j < W_IN_QKV_TILES + W_IN_Z_TILES))
    def _():
        z_ref[...] = w_ref[...].astype(z_ref.dtype)

    @pl.when(j >= W_IN_QKV_TILES + W_IN_Z_TILES)
    def _():
        xbc_ref[...] = w_ref[...].astype(xbc_ref.dtype)


def _split_w_in(w):
    k = w.shape[0]
    bn = W_IN_CAST_TILE
    q, z = W_IN_QKV_TILES, W_IN_Z_TILES
    return pl.pallas_call(
        _split_w_in_kernel,
        grid=(q + z + W_IN_XBC_TILES,),
        in_specs=[pl.BlockSpec((k, bn), lambda j: (0, j))],
        out_specs=[pl.BlockSpec((k, bn), lambda j: (0, jnp.minimum(j, q - 1))),
                   pl.BlockSpec((k, bn), lambda j: (0, jnp.clip(j - q, 0, z - 1))),
                   pl.BlockSpec((k, bn), lambda j: (0, jnp.clip(j - q - z, 0, W_IN_XBC_TILES - 1)))],
        out_shape=[jax.ShapeDtypeStruct((k, QKV_COLS), BF16),
                   jax.ShapeDtypeStruct((k, D_SSM), BF16),
                   jax.ShapeDtypeStruct((k, XBC_COLS), BF16)],
        compiler_params=_params(("arbitrary",)),
        name="split_w_in",
    )(w)


def _proj_kernel(x_ref, w_ref, o_ref):
    o_ref[...] = jnp.dot(x_ref[...], w_ref[...], preferred_element_type=F32)


def _proj_dt_kernel(x_ref, w_ref, wdt_ref, o_ref, dt_ref):
    o_ref[...] = jnp.dot(x_ref[...], w_ref[...], preferred_element_type=F32)

    @pl.when(pl.program_id(1) == 0)
    def _():
        dt_ref[...] = jnp.dot(x_ref[...], wdt_ref[...], preferred_element_type=F32)


def _proj(hn, w, bn, name, w_dt=None, bm=1024):
    m, k = hn.shape
    n = w.shape[1]
    in_specs = [pl.BlockSpec((bm, k), lambda i, j: (i, 0)),
                pl.BlockSpec((k, bn), lambda i, j: (0, j))]
    out_specs = [pl.BlockSpec((bm, bn), lambda i, j: (i, j))]
    out_shape = [jax.ShapeDtypeStruct((m, n), F32)]
    args = [hn, w]
    if w_dt is not None:
        in_specs.append(pl.BlockSpec((k, LANES), lambda i, j: (0, 0)))
        out_specs.append(pl.BlockSpec((bm, LANES), lambda i, j: (i, 0)))
        out_shape.append(jax.ShapeDtypeStruct((m, LANES), F32))
        args.append(w_dt)
    return pl.pallas_call(
        _proj_kernel if w_dt is None else _proj_dt_kernel,
        grid=(m // bm, n // bn),
        in_specs=in_specs,
        out_specs=out_specs,
        out_shape=out_shape,
        compiler_params=_params(("arbitrary", "arbitrary")),
        name=name,
    )(*args)


def _rope_tile(x, cos, sin_signed, first_half):
    rot = jnp.where(first_half, pltpu.roll(x, LANES - HEAD_DIM // 2, 1),
                    pltpu.roll(x, HEAD_DIM // 2, 1))
    return x * cos + rot * sin_signed


def _swa_kernel(pos_ref, freq_ref, sink_ref, q_ref, k_ref, v_ref, o_ref, kband_ref, vband_ref):
    n = pl.program_id(0)
    w = WINDOW

    lane = lax.broadcasted_iota(jnp.int32, (1, LANES), 1)
    low_half = lane < HEAD_DIM
    first_half = (lane % HEAD_DIM) < (HEAD_DIM // 2)

    @pl.when(n == 0)
    def _():
        kband_ref[...] = jnp.zeros_like(kband_ref)
        ones_lo = jnp.broadcast_to(jnp.where(low_half, 1.0, 0.0), (2 * w, LANES)).astype(BF16)
        ones_hi = jnp.broadcast_to(jnp.where(low_half, 0.0, 1.0), (2 * w, LANES)).astype(BF16)
        for kh in range(N_KV_HEADS):
            vband_ref[kh, :, 0:LANES] = jnp.zeros((4 * w, LANES), BF16)
            vband_ref[kh, 0:2 * w, LANES:2 * LANES] = ones_lo
            vband_ref[kh, 2 * w:4 * w, LANES:2 * LANES] = ones_hi

    ang = pos_ref[...].astype(F32) * freq_ref[...]
    cos = jnp.cos(ang)
    sin_signed = jnp.where(first_half, -jnp.sin(ang), jnp.sin(ang))

    for kh in range(N_KV_HEADS):
        kband_ref[kh, 0:w, :] = kband_ref[kh, w:2 * w, :]
        vband_ref[kh, 0:w, 0:LANES] = vband_ref[kh, w:2 * w, 0:LANES]
        vband_ref[kh, 2 * w:3 * w, 0:LANES] = vband_ref[kh, 3 * w:4 * w, 0:LANES]
    for t in range(KV_COLS // LANES):
        k_rot = _rope_tile(k_ref[:, t * LANES:(t + 1) * LANES], cos, sin_signed, first_half)
        k_sw = pltpu.roll(k_rot, HEAD_DIM, 1)
        v_t = v_ref[:, t * LANES:(t + 1) * LANES]
        v_sw = pltpu.roll(v_t, HEAD_DIM, 1)
        zero = jnp.zeros_like(v_t)
        kband_ref[2 * t, w:2 * w, :] = jnp.where(low_half, k_rot, k_sw).astype(BF16)
        kband_ref[2 * t + 1, w:2 * w, :] = jnp.where(low_half, k_sw, k_rot).astype(BF16)
        vband_ref[2 * t, w:2 * w, 0:LANES] = jnp.where(low_half, v_t, zero).astype(BF16)
        vband_ref[2 * t, 3 * w:4 * w, 0:LANES] = jnp.where(low_half, zero, v_sw).astype(BF16)
        vband_ref[2 * t + 1, w:2 * w, 0:LANES] = jnp.where(low_half, v_sw, zero).astype(BF16)
        vband_ref[2 * t + 1, 3 * w:4 * w, 0:LANES] = jnp.where(low_half, zero, v_t).astype(BF16)

    qi = lax.broadcasted_iota(jnp.int32, (w, 2 * w), 0)
    si = lax.broadcasted_iota(jnp.int32, (w, 2 * w), 1)
    prev_off = jnp.where(n > 0, 0, 2 * w)
    valid = ((si < w) & (si > qi + prev_off)) | ((si >= w) & (si - w <= qi))
    scale = HEAD_DIM ** -0.5

    for kh in range(N_KV_HEADS):
        kb = kband_ref[kh]
        vb = vband_ref[kh]
        for j in range(Q_PER_KV // 2):
            t = kh * (Q_PER_KV // 2) + j
            q_rot = _rope_tile(q_ref[:, t * LANES:(t + 1) * LANES], cos, sin_signed, first_half) * scale
            probs = []
            sink_terms = []
            for r in range(2):
                keep = low_half if r == 0 else jnp.logical_not(low_half)
                lhs = jnp.where(keep, q_rot, 0.0).astype(BF16)
                s = lax.dot_general(lhs, kb, (((1,), (1,)), ((), ())), preferred_element_type=F32)
                s = jnp.where(valid, s, -jnp.inf)
                sink = sink_ref[2 * t + r]
                m = jnp.max(jnp.maximum(s[:, 0:w], s[:, w:2 * w]), axis=-1, keepdims=True)
                m = jnp.maximum(m, sink)
                probs.append(jnp.exp(s - m).astype(BF16))
                sink_terms.append(jnp.exp(sink - m))
            od = jnp.dot(jnp.concatenate(probs, axis=1), vb, preferred_element_type=F32)
            denom = od[:, LANES:2 * LANES] + jnp.where(low_half, sink_terms[0], sink_terms[1])
            o_ref[:, t * LANES:(t + 1) * LANES] = (od[:, 0:LANES] / denom).astype(o_ref.dtype)


def _swa(qkv, pos, freq, sinks):
    s = qkv.shape[0]
    w = WINDOW
    return pl.pallas_call(
        _swa_kernel,
        grid=(s // w,),
        in_specs=[pl.BlockSpec((w, 1), lambda n: (n, 0)),
                  pl.BlockSpec((1, LANES), lambda n: (0, 0)),
                  pl.BlockSpec(memory_space=pltpu.SMEM),
                  pl.BlockSpec((w, D_ATTN), lambda n: (n, 0)),
                  pl.BlockSpec((w, KV_COLS), lambda n: (n, D_ATTN // KV_COLS)),
                  pl.BlockSpec((w, KV_COLS), lambda n: (n, D_ATTN // KV_COLS + 1))],
        out_specs=pl.BlockSpec((w, D_ATTN), lambda n: (n, 0)),
        out_shape=jax.ShapeDtypeStruct((s, D_ATTN), BF16),
        scratch_shapes=[pltpu.VMEM((N_KV_HEADS, 2 * w, LANES), BF16),
                        pltpu.VMEM((N_KV_HEADS, 4 * w, 2 * LANES), BF16)],
        compiler_params=_params(("arbitrary",)),
        name="swa",
    )(pos, freq, sinks, qkv, qkv, qkv)


def _split3(v):
    hi = v.astype(BF16)
    r = v - hi.astype(F32)
    mid = r.astype(BF16)
    lo = (r - mid.astype(F32)).astype(BF16)
    return hi, mid, lo


def _dot_exact_rhs(a, b_parts):
    acc = jnp.dot(a, b_parts[0], preferred_element_type=F32)
    for part in b_parts[1:]:
        acc = acc + jnp.dot(a, part, preferred_element_type=F32)
    return acc


def _dot_exact_lhs(a_parts, b):
    acc = jnp.dot(a_parts[0], b, preferred_element_type=F32)
    for part in a_parts[1:]:
        acc = acc + jnp.dot(part, b, preferred_element_type=F32)
    return acc


def _ssd_kernel(xbc_ref, z_ref, dt_ref, cw_ref, cb_ref, dtb_ref, alog_ref, dexp_ref,
                ng_ref, tri_ref, expand_ref, o_ref, xpad_ref, xc_ref, y_ref, state_ref):
    c = pl.program_id(0)
    L = CHUNK
    G, R, P, N = SSM_GROUPS, HEADS_PER_GROUP, HEAD_DIM, SSM_STATE
    GP = R * P

    @pl.when(c == 0)
    def _():
        state_ref[...] = jnp.zeros_like(state_ref)
        xpad_ref[:, L:L + SUBLANES, :] = jnp.zeros((XBC_COLS // LANES, SUBLANES, LANES), F32)

    xpad_ref[:, 0:SUBLANES, :] = xpad_ref[:, L:L + SUBLANES, :]

    for p in range(XBC_COLS // LANES):
        cs = slice(p * LANES, (p + 1) * LANES)
        xpad_ref[p, SUBLANES:SUBLANES + L, :] = xbc_ref[:, cs]
        acc = xpad_ref[p, SUBLANES - 3:SUBLANES - 3 + L, :] * cw_ref[0:1, cs]
        for k in range(1, SSM_CONV):
            acc = acc + xpad_ref[p, SUBLANES - 3 + k:SUBLANES - 3 + k + L, :] * cw_ref[k:k + 1, cs]
        acc = acc + cb_ref[:, cs]
        xc_ref[:, cs] = acc * jax.nn.sigmoid(acc)

    dt_raw = dt_ref[...] + dtb_ref[...]
    dt = jnp.maximum(dt_raw, 0.0) + jnp.log1p(jnp.exp(-jnp.abs(dt_raw)))
    a = -jnp.exp(alog_ref[...])
    da = dt * a
    a_cs = _dot_exact_rhs(tri_ref[...], _split3(da))
    a_cs_t = a_cs.T
    expand = expand_ref[...]
    acs_x = _dot_exact_lhs(_split3(a_cs), expand)
    dt_x = _dot_exact_lhs(_split3(dt), expand)

    li = lax.broadcasted_iota(jnp.int32, (L, L), 0)
    si = lax.broadcasted_iota(jnp.int32, (L, L), 1)
    causal = li >= si
    lane = lax.broadcasted_iota(jnp.int32, (1, LANES), 1)
    low_half = lane < P

    for g in range(G):
        xs = slice(g * GP, (g + 1) * GP)
        x_g = xc_ref[:, xs]
        b_g = xc_ref[:, D_SSM + g * N:D_SSM + (g + 1) * N].astype(BF16)
        c_g = xc_ref[:, D_SSM + G * N + g * N:D_SSM + G * N + (g + 1) * N].astype(BF16)
        acs_g = acs_x[:, xs]
        xdt_g = x_g * dt_x[:, xs]
        cb = lax.dot_general(c_g, b_g, (((1,), (1,)), ((), ())), preferred_element_type=F32)

        y_pairs = []
        for j in range(R // 2):
            ws = []
            for r in range(2):
                h = g * R + 2 * j + r
                seg = a_cs[:, h:h + 1] - a_cs_t[h:h + 1, :]
                decay = jnp.exp(jnp.where(causal, seg, -jnp.inf))
                ws.append((cb * decay).astype(BF16))
            wcat = jnp.concatenate(ws, axis=1)
            xp = xdt_g[:, j * LANES:(j + 1) * LANES]
            rhs = jnp.concatenate([jnp.where(low_half, xp, 0.0), jnp.where(low_half, 0.0, xp)],
                                  axis=0).astype(BF16)
            y_pairs.append(jnp.dot(wcat, rhs, preferred_element_type=F32))
        y_diag = jnp.concatenate(y_pairs, axis=1)

        a_last = acs_g[L - 1:L, :]
        xdte = (xdt_g * jnp.exp(a_last - acs_g)).astype(BF16)
        new_state = lax.dot_general(b_g, xdte, (((0,), (0,)), ((), ())),
                                    preferred_element_type=F32)
        prev = state_ref[g]
        y_off = jnp.dot(c_g, prev.astype(BF16), preferred_element_type=F32) * jnp.exp(acs_g)
        state_ref[g] = prev * jnp.exp(a_last) + new_state
        y_ref[:, xs] = y_diag + y_off + x_g * dexp_ref[:, xs]

    z = z_ref[...]
    y = y_ref[...] * (z * jax.nn.sigmoid(z))
    ms = jnp.mean(y * y, axis=-1, keepdims=True)
    o_ref[...] = (y * lax.rsqrt(ms + EPS) * ng_ref[...]).astype(o_ref.dtype)


def _ssd(xbc, z, dt, conv_w, conv_b, dt_bias, a_log, d_skip, norm_g):
    s = xbc.shape[0]
    L = CHUNK
    pad = LANES - N_SSM_HEADS
    dtb = jnp.pad(dt_bias, (0, pad)).reshape(1, LANES)
    alog = jnp.pad(a_log, (0, pad)).reshape(1, LANES)
    dexp = jnp.repeat(d_skip, HEAD_DIM).reshape(1, D_SSM)
    tri = (jnp.arange(L)[:, None] >= jnp.arange(L)[None, :]).astype(BF16)
    expand = (jnp.arange(LANES)[:, None] == (jnp.arange(D_SSM)[None, :] // HEAD_DIM)).astype(BF16)
    const = lambda c: (0, 0)
    return pl.pallas_call(
        _ssd_kernel,
        grid=(s // L,),
        in_specs=[pl.BlockSpec((L, XBC_COLS), lambda c: (c, 0)),
                  pl.BlockSpec((L, D_SSM), lambda c: (c, 0)),
                  pl.BlockSpec((L, LANES), lambda c: (c, 0)),
                  pl.BlockSpec((SSM_CONV, XBC_COLS), const),
                  pl.BlockSpec((1, XBC_COLS), const),
                  pl.BlockSpec((1, LANES), const),
                  pl.BlockSpec((1, LANES), const),
                  pl.BlockSpec((1, D_SSM), const),
                  pl.BlockSpec((1, D_SSM), const),
                  pl.BlockSpec((L, L), const),
                  pl.BlockSpec((LANES, D_SSM), const)],
        out_specs=pl.BlockSpec((L, D_SSM), lambda c: (c, 0)),
        out_shape=jax.ShapeDtypeStruct((s, D_SSM), BF16),
        scratch_shapes=[pltpu.VMEM((XBC_COLS // LANES, SUBLANES + L, LANES), F32),
                        pltpu.VMEM((L, XBC_COLS), F32),
                        pltpu.VMEM((L, D_SSM), F32),
                        pltpu.VMEM((SSM_GROUPS, SSM_STATE, HEADS_PER_GROUP * HEAD_DIM), F32)],
        compiler_params=_params(("arbitrary",)),
        name="ssd",
    )(xbc, z, dt, conv_w, conv_b.reshape(1, XBC_COLS), dtb, alog, dexp,
      norm_g.reshape(1, D_SSM), tri, expand)


def _out_proj_kernel(ya_ref, ys_ref, w_ref, res_ref, g_ref, o_ref, hg_ref, r_ref, ss_ref):
    j = pl.program_id(1)
    bn = w_ref.shape[1]

    @pl.when(j == 0)
    def _():
        ss_ref[...] = jnp.zeros_like(ss_ref)

    acc = jnp.dot(ya_ref[...], w_ref[0:D_ATTN, :], preferred_element_type=F32)
    acc = acc + jnp.dot(ys_ref[...], w_ref[D_ATTN:D_ATTN + D_SSM, :], preferred_element_type=F32)
    h = res_ref[...] + acc
    o_ref[...] = h
    hg_ref[...] = (h * g_ref[...]).astype(hg_ref.dtype)
    sq = h * h
    part = sq[:, 0:LANES]
    for c in range(1, bn // LANES):
        part = part + sq[:, c * LANES:(c + 1) * LANES]
    ss_ref[...] += part

    @pl.when(j == pl.num_programs(1) - 1)
    def _():
        ms = jnp.sum(ss_ref[...], axis=-1, keepdims=True) * (1.0 / D_MODEL)
        r_ref[...] = lax.rsqrt(ms + EPS)


def _out_proj(ya, ys, w, res, g, bm=1024, bn=1024):
    m = ya.shape[0]
    k, n = w.shape
    return pl.pallas_call(
        _out_proj_kernel,
        grid=(m // bm, n // bn),
        in_specs=[pl.BlockSpec((bm, D_ATTN), lambda i, j: (i, 0)),
                  pl.BlockSpec((bm, D_SSM), lambda i, j: (i, 0)),
                  pl.BlockSpec((k, bn), lambda i, j: (0, j)),
                  pl.BlockSpec((bm, bn), lambda i, j: (i, j)),
                  pl.BlockSpec((1, bn), lambda i, j: (0, j))],
        out_specs=[pl.BlockSpec((bm, bn), lambda i, j: (i, j)),
                   pl.BlockSpec((bm, bn), lambda i, j: (i, j)),
                   pl.BlockSpec((bm, 1), lambda i, j: (i, 0))],
        out_shape=[jax.ShapeDtypeStruct((m, n), F32),
                   jax.ShapeDtypeStruct((m, n), BF16),
                   jax.ShapeDtypeStruct((m, 1), F32)],
        scratch_shapes=[pltpu.VMEM((bm, LANES), F32)],
        compiler_params=_params(("arbitrary", "arbitrary"), vmem=VMEM_LIMIT_HIGH),
        name="out_proj",
    )(ya, ys, w, res, g.reshape(1, n))


def _ffn_up_kernel(x_ref, r_ref, wg_ref, wu_ref, cw_ref, cb_ref, o_ref, ubuf_ref, wbf_ref):
    t = pl.program_id(0)
    n_steps = pl.num_programs(0)
    n_row_blocks = (n_steps - 1) // N_FFN_TILES
    bm = x_ref.shape[0]
    ft = FFN_TILE
    planes_per_half = ft // LANES
    row_block_new = jnp.minimum(t, n_steps - 2) % n_row_blocks

    @pl.when(t == 0)
    def _():
        ubuf_ref[...] = jnp.zeros_like(ubuf_ref)

    @pl.when((row_block_new == 0) & (t < n_steps - 1))
    def _():
        wbf_ref[0] = wg_ref[...].astype(BF16)
        wbf_ref[1] = wu_ref[...].astype(BF16)

    f_done = jnp.maximum(t - 1, 0) // n_row_blocks
    for q in range(planes_per_half):
        conv = []
        for half in range(2):
            plane = half * planes_per_half + q
            col = pl.multiple_of(half * D_FF + f_done * ft + q * LANES, LANES)
            acc = ubuf_ref[plane, SUBLANES - 2:SUBLANES - 2 + bm, :] * cw_ref[0:1, pl.ds(col, LANES)]
            acc = acc + ubuf_ref[plane, SUBLANES - 1:SUBLANES - 1 + bm, :] * cw_ref[1:2, pl.ds(col, LANES)]
            acc = acc + ubuf_ref[plane, SUBLANES:SUBLANES + bm, :] * cw_ref[2:3, pl.ds(col, LANES)]
            conv.append(acc + cb_ref[:, pl.ds(col, LANES)])
        gate, up = conv
        o_ref[:, q * LANES:(q + 1) * LANES] = (gate * jax.nn.sigmoid(gate) * up).astype(o_ref.dtype)
    last_rows = ubuf_ref[:, bm:bm + SUBLANES, :]
    ubuf_ref[:, 0:SUBLANES, :] = jnp.where(row_block_new == 0, jnp.zeros_like(last_rows), last_rows)

    x = x_ref[...]
    r = r_ref[...]
    for half in range(2):
        u = jnp.dot(x, wbf_ref[half], preferred_element_type=F32) * r
        for q in range(planes_per_half):
            ubuf_ref[half * planes_per_half + q, SUBLANES:SUBLANES + bm, :] = u[:, q * LANES:(q + 1) * LANES]


def _ffn_up(hg, r, w, cw, cb, bm=1024):
    m, k = hg.shape
    nf = N_FFN_TILES
    nrb = m // bm
    n_tiles = nrb * nf
    n_planes = 2 * FFN_TILE // LANES
    cur = lambda t: jnp.minimum(t, n_tiles - 1)
    done = lambda t: jnp.maximum(t - 1, 0)
    return pl.pallas_call(
        _ffn_up_kernel,
        grid=(n_tiles + 1,),
        in_specs=[pl.BlockSpec((bm, k), lambda t: (cur(t) % nrb, 0)),
                  pl.BlockSpec((bm, 1), lambda t: (cur(t) % nrb, 0)),
                  pl.BlockSpec((k, FFN_TILE), lambda t: (0, cur(t) // nrb)),
                  pl.BlockSpec((k, FFN_TILE), lambda t: (0, nf + cur(t) // nrb)),
                  pl.BlockSpec((FFN_CONV, 2 * D_FF), lambda t: (0, 0)),
                  pl.BlockSpec((1, 2 * D_FF), lambda t: (0, 0))],
        out_specs=pl.BlockSpec((bm, FFN_TILE), lambda t: (done(t) % nrb, done(t) // nrb)),
        out_shape=jax.ShapeDtypeStruct((m, D_FF), BF16),
        scratch_shapes=[pltpu.VMEM((n_planes, SUBLANES + bm, LANES), F32),
                        pltpu.VMEM((2, k, FFN_TILE), BF16)],
        compiler_params=_params(("arbitrary",)),
        name="ffn_up",
    )(hg, r, w, w, cw, cb)


def _ffn_down_kernel(a_ref, w_ref, res_ref, o_ref, obf_ref):
    h = res_ref[...] + jnp.dot(a_ref[...], w_ref[...], preferred_element_type=F32)
    o_ref[...] = h
    obf_ref[...] = h.astype(obf_ref.dtype)


def _ffn_down(act, w, res, bm=512, bn=512):
    m, k = act.shape
    n = w.shape[1]
    return pl.pallas_call(
        _ffn_down_kernel,
        grid=(n // bn, m // bm),
        in_specs=[pl.BlockSpec((bm, k), lambda j, i: (i, 0)),
                  pl.BlockSpec((k, bn), lambda j, i: (0, j)),
                  pl.BlockSpec((bm, bn), lambda j, i: (i, j))],
        out_specs=[pl.BlockSpec((bm, bn), lambda j, i: (i, j)),
                   pl.BlockSpec((bm, bn), lambda j, i: (i, j))],
        out_shape=[jax.ShapeDtypeStruct((m, n), F32), jax.ShapeDtypeStruct((m, n), BF16)],
        compiler_params=_params(("arbitrary", "arbitrary")),
        name="ffn_down",
    )(act, w, res)


def _ple_kernel(hb_ref, wg_ref, bg_ref, p_ref, wp_ref, h_ref, gf_ref, o_ref, ss_ref):
    j = pl.program_id(1)
    nj = pl.num_programs(1)
    bn = wg_ref.shape[1]

    @pl.when(j == 0)
    def _():
        ss_ref[...] = jnp.zeros_like(ss_ref)

    gate = jax.nn.sigmoid(jnp.dot(hb_ref[...], wg_ref[...], preferred_element_type=F32) + bg_ref[...])
    emb = jnp.dot(p_ref[...].astype(BF16), wp_ref[...], preferred_element_type=F32)
    h = h_ref[...] + gate * emb
    sq = h * h
    part = sq[:, 0:LANES]
    for c in range(1, bn // LANES):
        part = part + sq[:, c * LANES:(c + 1) * LANES]
    ss_ref[...] += part
    col = pl.multiple_of(j * bn, bn)
    o_ref[:, pl.ds(col, bn)] = h

    @pl.when(j == nj - 1)
    def _():
        ms = jnp.sum(ss_ref[...], axis=-1, keepdims=True) * (1.0 / D_MODEL)
        o_ref[...] = o_ref[...] * lax.rsqrt(ms + EPS) * gf_ref[...]


def _ple(hb, wg, bg, p, wp, h, gf, bm=512, bn=1024):
    m, d = h.shape
    return pl.pallas_call(
        _ple_kernel,
        grid=(m // bm, d // bn),
        in_specs=[pl.BlockSpec((bm, d), lambda i, j: (i, 0)),
                  pl.BlockSpec((d, bn), lambda i, j: (0, j)),
                  pl.BlockSpec((1, bn), lambda i, j: (0, j)),
                  pl.BlockSpec((bm, PLE_DIM), lambda i, j: (i, 0)),
                  pl.BlockSpec((PLE_DIM, bn), lambda i, j: (0, j)),
                  pl.BlockSpec((bm, bn), lambda i, j: (i, j)),
                  pl.BlockSpec((1, d), lambda i, j: (0, 0))],
        out_specs=pl.BlockSpec((bm, d), lambda i, j: (i, 0)),
        out_shape=jax.ShapeDtypeStruct((m, d), F32),
        scratch_shapes=[pltpu.VMEM((bm, LANES), F32)],
        compiler_params=_params(("arbitrary", "arbitrary"), vmem=VMEM_LIMIT_HIGH),
        name="ple",
    )(hb, wg, bg, p, wp, h, gf)


def kernel(x, p, positions, norm_mix_g, w_in, attn_sinks, ssm_conv_w, ssm_conv_b, ssm_dt_bias,
           ssm_a_log, ssm_d, ssm_norm_g, w_o, norm_ffn_g, w_up, ffn_conv_w, ffn_conv_b, w_down,
           ple_gate_w, ple_gate_b, ple_proj, norm_final_g):
    b, s, d = x.shape
    assert (b, s, d) == (1, SEQ, D_MODEL) and w_in.shape[0] == 1
    h = x.reshape(s, d)
    pos = positions.reshape(s, 1)
    half = HEAD_DIM // 2
    inv_freq = ROPE_THETA ** (-jnp.arange(half, dtype=F32) / half)
    freq = jnp.tile(inv_freq, LANES // half).reshape(1, LANES)
    w_dt = jnp.pad(w_in[0][:, DT_COL0:DT_COL0 + N_SSM_HEADS], ((0, 0), (0, LANES - N_SSM_HEADS)))

    hn = _norm_cast(h, norm_mix_g[0])
    w_qkv, w_z, w_xbc = _split_w_in(w_in[0])
    qkv, dt = _proj(hn, w_qkv, QKV_TILE, "in_proj_qkv", w_dt=w_dt.astype(BF16))
    z, = _proj(hn, w_z, SSM_TILE, "in_proj_z")
    xbc, = _proj(hn, w_xbc, SSM_TILE, "in_proj_xbc")
    y_attn = _swa(qkv, pos, freq, attn_sinks[0])
    y_ssm = _ssd(xbc, z, dt, ssm_conv_w[0], ssm_conv_b[0], ssm_dt_bias[0], ssm_a_log[0], ssm_d[0],
                 ssm_norm_g[0])
    h1, hg, r = _out_proj(y_attn, y_ssm, w_o[0].astype(BF16), h, norm_ffn_g[0])
    act = _ffn_up(hg, r, w_up[0], ffn_conv_w[0], ffn_conv_b[0].reshape(1, 2 * D_FF))
    h2, h2b = _ffn_down(act, w_down[0].astype(BF16), h1)
    out = _ple(h2b, ple_gate_w[0].astype(BF16), ple_gate_b[0].reshape(1, d), p[0].reshape(s, PLE_DIM),
               ple_proj[0].astype(BF16), h2, norm_final_g.reshape(1, d))
    return out.reshape(b, s, d)
```

```python
import jax
import jax.numpy as jnp
from jax import lax
from jax.experimental import pallas as pl
from jax.experimental.pallas import tpu as pltpu

F32 = jnp.float32
BF16 = jnp.bfloat16

D_MODEL = 4096
SEQ = 16384
PLE_DIM = 256
D_ATTN = 2048
D_SSM = 2048
HEAD_DIM = 64
N_Q_HEADS = 32
N_KV_HEADS = 4
Q_PER_KV = 8
WINDOW = 128
ROPE_THETA = 10000.0
N_SSM_HEADS = 32
SSM_STATE = 128
SSM_GROUPS = 8
HEADS_PER_GROUP = 4
SSM_CONV = 4
CHUNK = 128
D_FF = 11008
FFN_CONV = 3
EPS = 1e-6
KV_COLS = N_KV_HEADS * HEAD_DIM
QKV_COLS = D_ATTN + 2 * KV_COLS
XBC_COLS = D_SSM + 2 * SSM_GROUPS * SSM_STATE
DT_COL0 = QKV_COLS + D_SSM + XBC_COLS

LANES = 128
SUBLANES = 8
VMEM_LIMIT = 56 * 1024 * 1024
VMEM_LIMIT_HIGH = 62 * 1024 * 1024

QKV_TILE = QKV_COLS // 2
SSM_TILE = 1024
FFN_TILE = 256
N_FFN_TILES = D_FF // FFN_TILE


def _params(sem, vmem=VMEM_LIMIT, flags=None):
    return pltpu.CompilerParams(dimension_semantics=sem, vmem_limit_bytes=vmem, flags=flags)


def _norm_cast_kernel(x_ref, g_ref, o_ref):
    x = x_ref[...]
    ms = jnp.mean(x * x, axis=-1, keepdims=True)
    o_ref[...] = (x * lax.rsqrt(ms + EPS) * g_ref[...]).astype(o_ref.dtype)


def _norm_cast(x, g, bm=512):
    m, d = x.shape
    return pl.pallas_call(
        _norm_cast_kernel,
        grid=(m // bm,),
        in_specs=[pl.BlockSpec((bm, d), lambda i: (i, 0)),
                  pl.BlockSpec((1, d), lambda i: (0, 0))],
        out_specs=pl.BlockSpec((bm, d), lambda i: (i, 0)),
        out_shape=jax.ShapeDtypeStruct((m, d), BF16),
        compiler_params=_params(("arbitrary",)),
        name="norm_cast",
    )(x, g.reshape(1, d))


W_IN_CAST_TILE = 512
W_IN_QKV_TILES = QKV_COLS // W_IN_CAST_TILE
W_IN_Z_TILES = D_SSM // W_IN_CAST_TILE
W_IN_XBC_TILES = XBC_COLS // W_IN_CAST_TILE


def _split_w_in_kernel(wt_ref, qkv_ref, z_ref, xbc_ref):
    j = pl.program_id(0)

    @pl.when(j < W_IN_QKV_TILES)
    def _():
        qkv_ref[...] = wt_ref[...].T.astype(qkv_ref.dtype)

    @pl.when((j >= W_IN_QKV_TILES) & (j < W_IN_QKV_TILES + W_IN_Z_TILES))
    def _():
        z_ref[...] = wt_ref[...].T.astype(z_ref.dtype)

    @pl.when(j >= W_IN_QKV_TILES + W_IN_Z_TILES)
    def _():
        xbc_ref[...] = wt_ref[...].T.astype(xbc_ref.dtype)


def _split_w_in(wt):
    k = wt.shape[2]
    bn = W_IN_CAST_TILE
    q, z = W_IN_QKV_TILES, W_IN_Z_TILES
    return pl.pallas_call(
        _split_w_in_kernel,
        grid=(q + z + W_IN_XBC_TILES,),
        in_specs=[pl.BlockSpec((None, bn, k), lambda j: (0, j, 0))],
        out_specs=[pl.BlockSpec((k, bn), lambda j: (0, jnp.minimum(j, q - 1))),
                   pl.BlockSpec((k, bn), lambda j: (0, jnp.clip(j - q, 0, z - 1))),
                   pl.BlockSpec((k, bn), lambda j: (0, jnp.clip(j - q - z, 0, W_IN_XBC_TILES - 1)))],
        out_shape=[jax.ShapeDtypeStruct((k, QKV_COLS), BF16),
                   jax.ShapeDtypeStruct((k, D_SSM), BF16),
                   jax.ShapeDtypeStruct((k, XBC_COLS), BF16)],
        compiler_params=_params(("arbitrary",)),
        name="split_w_in",
    )(wt)


def _proj_kernel(x_ref, w_ref, o_ref):
    o_ref[...] = jnp.dot(x_ref[...], w_ref[...], preferred_element_type=F32)


def _proj_dt_kernel(x_ref, w_ref, wdt_ref, o_ref, dt_ref):
    o_ref[...] = jnp.dot(x_ref[...], w_ref[...], preferred_element_type=F32)

    @pl.when(pl.program_id(1) == 0)
    def _():
        dt_ref[...] = jnp.dot(x_ref[...], wdt_ref[...], preferred_element_type=F32)


def _proj(hn, w, bn, name, w_dt=None, bm=1024):
    m, k = hn.shape
    n = w.shape[1]
    in_specs = [pl.BlockSpec((bm, k), lambda i, j: (i, 0)),
                pl.BlockSpec((k, bn), lambda i, j: (0, j))]
    out_specs = [pl.BlockSpec((bm, bn), lambda i, j: (i, j))]
    out_shape = [jax.ShapeDtypeStruct((m, n), F32)]
    args = [hn, w]
    if w_dt is not None:
        in_specs.append(pl.BlockSpec((k, LANES), lambda i, j: (0, 0)))
        out_specs.append(pl.BlockSpec((bm, LANES), lambda i, j: (i, 0)))
        out_shape.append(jax.ShapeDtypeStruct((m, LANES), F32))
        args.append(w_dt)
    return pl.pallas_call(
        _proj_kernel if w_dt is None else _proj_dt_kernel,
        grid=(m // bm, n // bn),
        in_specs=in_specs,
        out_specs=out_specs,
        out_shape=out_shape,
        compiler_params=_params(("arbitrary", "arbitrary")),
        name=name,
    )(*args)


def _rope_tile(x, cos, sin_signed, first_half):
    rot = jnp.where(first_half, pltpu.roll(x, LANES - HEAD_DIM // 2, 1),
                    pltpu.roll(x, HEAD_DIM // 2, 1))
    return x * cos + rot * sin_signed


def _swa_kernel(pos_ref, freq_ref, sink_ref, q_ref, k_ref, v_ref, o_ref, kband_ref, vband_ref):
    n = pl.program_id(0)
    w = WINDOW

    lane = lax.broadcasted_iota(jnp.int32, (1, LANES), 1)
    low_half = lane < HEAD_DIM
    first_half = (lane % HEAD_DIM) < (HEAD_DIM // 2)

    @pl.when(n == 0)
    def _():
        kband_ref[...] = jnp.zeros_like(kband_ref)
        ones_lo = jnp.broadcast_to(jnp.where(low_half, 1.0, 0.0), (2 * w, LANES)).astype(BF16)
        ones_hi = jnp.broadcast_to(jnp.where(low_half, 0.0, 1.0), (2 * w, LANES)).astype(BF16)
        for kh in range(N_KV_HEADS):
            vband_ref[kh, :, 0:LANES] = jnp.zeros((4 * w, LANES), BF16)
            vband_ref[kh, 0:2 * w, LANES:2 * LANES] = ones_lo
            vband_ref[kh, 2 * w:4 * w, LANES:2 * LANES] = ones_hi

    ang = pos_ref[...].astype(F32) * freq_ref[...]
    cos = jnp.cos(ang)
    sin_signed = jnp.where(first_half, -jnp.sin(ang), jnp.sin(ang))

    for kh in range(N_KV_HEADS):
        kband_ref[kh, 0:w, :] = kband_ref[kh, w:2 * w, :]
        vband_ref[kh, 0:w, 0:LANES] = vband_ref[kh, w:2 * w, 0:LANES]
        vband_ref[kh, 2 * w:3 * w, 0:LANES] = vband_ref[kh, 3 * w:4 * w, 0:LANES]
    for t in range(KV_COLS // LANES):
        k_rot = _rope_tile(k_ref[:, t * LANES:(t + 1) * LANES], cos, sin_signed, first_half)
        k_sw = pltpu.roll(k_rot, HEAD_DIM, 1)
        v_t = v_ref[:, t * LANES:(t + 1) * LANES]
        v_sw = pltpu.roll(v_t, HEAD_DIM, 1)
        zero = jnp.zeros_like(v_t)
        kband_ref[2 * t, w:2 * w, :] = jnp.where(low_half, k_rot, k_sw).astype(BF16)
        kband_ref[2 * t + 1, w:2 * w, :] = jnp.where(low_half, k_sw, k_rot).astype(BF16)
        vband_ref[2 * t, w:2 * w, 0:LANES] = jnp.where(low_half, v_t, zero).astype(BF16)
        vband_ref[2 * t, 3 * w:4 * w, 0:LANES] = jnp.where(low_half, zero, v_sw).astype(BF16)
        vband_ref[2 * t + 1, w:2 * w, 0:LANES] = jnp.where(low_half, v_sw, zero).astype(BF16)
        vband_ref[2 * t + 1, 3 * w:4 * w, 0:LANES] = jnp.where(low_half, zero, v_t).astype(BF16)

    qi = lax.broadcasted_iota(jnp.int32, (w, 2 * w), 0)
    si = lax.broadcasted_iota(jnp.int32, (w, 2 * w), 1)
    prev_off = jnp.where(n > 0, 0, 2 * w)
    valid = ((si < w) & (si > qi + prev_off)) | ((si >= w) & (si - w <= qi))
    scale = HEAD_DIM ** -0.5

    for kh in range(N_KV_HEADS):
        kb = kband_ref[kh]
        vb = vband_ref[kh]
        for j in range(Q_PER_KV // 2):
            t = kh * (Q_PER_KV // 2) + j
            q_rot = _rope_tile(q_ref[:, t * LANES:(t + 1) * LANES], cos, sin_signed, first_half) * scale
            probs = []
            sink_terms = []
            for r in range(2):
                keep = low_half if r == 0 else jnp.logical_not(low_half)
                lhs = jnp.where(keep, q_rot, 0.0).astype(BF16)
                s = lax.dot_general(lhs, kb, (((1,), (1,)), ((), ())), preferred_element_type=F32)
                s = jnp.where(valid, s, -jnp.inf)
                sink = sink_ref[2 * t + r]
                m = jnp.max(jnp.maximum(s[:, 0:w], s[:, w:2 * w]), axis=-1, keepdims=True)
                m = jnp.maximum(m, sink)
                probs.append(jnp.exp(s - m).astype(BF16))
                sink_terms.append(jnp.exp(sink - m))
            od = jnp.dot(jnp.concatenate(probs, axis=1), vb, preferred_element_type=F32)
            denom = od[:, LANES:2 * LANES] + jnp.where(low_half, sink_terms[0], sink_terms[1])
            o_ref[:, t * LANES:(t + 1) * LANES] = (od[:, 0:LANES] / denom).astype(o_ref.dtype)


def _swa(qkv, pos, freq, sinks):
    s = qkv.shape[0]
    w = WINDOW
    return pl.pallas_call(
        _swa_kernel,
        grid=(s // w,),
        in_specs=[pl.BlockSpec((w, 1), lambda n: (n, 0)),
                  pl.BlockSpec((1, LANES), lambda n: (0, 0)),
                  pl.BlockSpec(memory_space=pltpu.SMEM),
                  pl.BlockSpec((w, D_ATTN), lambda n: (n, 0)),
                  pl.BlockSpec((w, KV_COLS), lambda n: (n, D_ATTN // KV_COLS)),
                  pl.BlockSpec((w, KV_COLS), lambda n: (n, D_ATTN // KV_COLS + 1))],
        out_specs=pl.BlockSpec((w, D_ATTN), lambda n: (n, 0)),
        out_shape=jax.ShapeDtypeStruct((s, D_ATTN), BF16),
        scratch_shapes=[pltpu.VMEM((N_KV_HEADS, 2 * w, LANES), BF16),
                        pltpu.VMEM((N_KV_HEADS, 4 * w, 2 * LANES), BF16)],
        compiler_params=_params(("arbitrary",)),
        name="swa",
    )(pos, freq, sinks, qkv, qkv, qkv)


def _split3(v):
    hi = v.astype(BF16)
    r = v - hi.astype(F32)
    mid = r.astype(BF16)
    lo = (r - mid.astype(F32)).astype(BF16)
    return hi, mid, lo


def _dot_exact_rhs(a, b_parts):
    acc = jnp.dot(a, b_parts[0], preferred_element_type=F32)
    for part in b_parts[1:]:
        acc = acc + jnp.dot(a, part, preferred_element_type=F32)
    return acc


def _dot_exact_lhs(a_parts, b):
    acc = jnp.dot(a_parts[0], b, preferred_element_type=F32)
    for part in a_parts[1:]:
        acc = acc + jnp.dot(part, b, preferred_element_type=F32)
    return acc


def _ssd_kernel(xbc_ref, z_ref, dt_ref, cw_ref, cb_ref, dtb_ref, alog_ref, dexp_ref,
                ng_ref, tri_ref, expand_ref, o_ref, xpad_ref, xc_ref, y_ref, state_ref):
    c = pl.program_id(0)
    L = CHUNK
    G, R, P, N = SSM_GROUPS, HEADS_PER_GROUP, HEAD_DIM, SSM_STATE
    GP = R * P

    @pl.when(c == 0)
    def _():
        state_ref[...] = jnp.zeros_like(state_ref)
        xpad_ref[:, L:L + SUBLANES, :] = jnp.zeros((XBC_COLS // LANES, SUBLANES, LANES), F32)

    xpad_ref[:, 0:SUBLANES, :] = xpad_ref[:, L:L + SUBLANES, :]

    for p in range(XBC_COLS // LANES):
        cs = slice(p * LANES, (p + 1) * LANES)
        xpad_ref[p, SUBLANES:SUBLANES + L, :] = xbc_ref[:, cs]
        acc = xpad_ref[p, SUBLANES - 3:SUBLANES - 3 + L, :] * cw_ref[0:1, cs]
        for k in range(1, SSM_CONV):
            acc = acc + xpad_ref[p, SUBLANES - 3 + k:SUBLANES - 3 + k + L, :] * cw_ref[k:k + 1, cs]
        acc = acc + cb_ref[:, cs]
        xc_ref[:, cs] = acc * jax.nn.sigmoid(acc)

    dt_raw = dt_ref[...] + dtb_ref[...]
    dt = jnp.maximum(dt_raw, 0.0) + jnp.log1p(jnp.exp(-jnp.abs(dt_raw)))
    a = -jnp.exp(alog_ref[...])
    da = dt * a
    a_cs = _dot_exact_rhs(tri_ref[...], _split3(da))
    a_cs_t = a_cs.T
    expand = expand_ref[...]
    acs_x = _dot_exact_lhs(_split3(a_cs), expand)
    dt_x = _dot_exact_lhs(_split3(dt), expand)

    li = lax.broadcasted_iota(jnp.int32, (L, L), 0)
    si = lax.broadcasted_iota(jnp.int32, (L, L), 1)
    causal = li >= si
    lane = lax.broadcasted_iota(jnp.int32, (1, LANES), 1)
    low_half = lane < P

    for g in range(G):
        xs = slice(g * GP, (g + 1) * GP)
        x_g = xc_ref[:, xs]
        b_g = xc_ref[:, D_SSM + g * N:D_SSM + (g + 1) * N].astype(BF16)
        c_g = xc_ref[:, D_SSM + G * N + g * N:D_SSM + G * N + (g + 1) * N].astype(BF16)
        acs_g = acs_x[:, xs]
        xdt_g = x_g * dt_x[:, xs]
        cb = lax.dot_general(c_g, b_g, (((1,), (1,)), ((), ())), preferred_element_type=F32)

        y_pairs = []
        for j in range(R // 2):
            ws = []
            for r in range(2):
                h = g * R + 2 * j + r
                seg = a_cs[:, h:h + 1] - a_cs_t[h:h + 1, :]
                decay = jnp.exp(jnp.where(causal, seg, -jnp.inf))
                ws.append((cb * decay).astype(BF16))
            wcat = jnp.concatenate(ws, axis=1)
            xp = xdt_g[:, j * LANES:(j + 1) * LANES]
            rhs = jnp.concatenate([jnp.where(low_half, xp, 0.0), jnp.where(low_half, 0.0, xp)],
                                  axis=0).astype(BF16)
            y_pairs.append(jnp.dot(wcat, rhs, preferred_element_type=F32))
        y_diag = jnp.concatenate(y_pairs, axis=1)

        a_last = acs_g[L - 1:L, :]
        xdte = (xdt_g * jnp.exp(a_last - acs_g)).astype(BF16)
        new_state = lax.dot_general(b_g, xdte, (((0,), (0,)), ((), ())),
                                    preferred_element_type=F32)
        prev = state_ref[g]
        y_off = jnp.dot(c_g, prev.astype(BF16), preferred_element_type=F32) * jnp.exp(acs_g)
        state_ref[g] = prev * jnp.exp(a_last) + new_state
        y_ref[:, xs] = y_diag + y_off + x_g * dexp_ref[:, xs]

    z = z_ref[...]
    y = y_ref[...] * (z * jax.nn.sigmoid(z))
    ms = jnp.mean(y * y, axis=-1, keepdims=True)
    o_ref[...] = (y * lax.rsqrt(ms + EPS) * ng_ref[...]).astype(o_ref.dtype)


def _ssd(xbc, z, dt, conv_w, conv_b, dt_bias, a_log, d_skip, norm_g):
    s = xbc.shape[0]
    L = CHUNK
    pad = LANES - N_SSM_HEADS
    dtb = jnp.pad(dt_bias, (0, pad)).reshape(1, LANES)
    alog = jnp.pad(a_log, (0, pad)).reshape(1, LANES)
    dexp = jnp.repeat(d_skip, HEAD_DIM).reshape(1, D_SSM)
    tri = (jnp.arange(L)[:, None] >= jnp.arange(L)[None, :]).astype(BF16)
    expand = (jnp.arange(LANES)[:, None] == (jnp.arange(D_SSM)[None, :] // HEAD_DIM)).astype(BF16)
    const = lambda c: (0, 0)
    return pl.pallas_call(
        _ssd_kernel,
        grid=(s // L,),
        in_specs=[pl.BlockSpec((L, XBC_COLS), lambda c: (c, 0)),
                  pl.BlockSpec((L, D_SSM), lambda c: (c, 0)),
                  pl.BlockSpec((L, LANES), lambda c: (c, 0)),
                  pl.BlockSpec((SSM_CONV, XBC_COLS), const),
                  pl.BlockSpec((1, XBC_COLS), const),
                  pl.BlockSpec((1, LANES), const),
                  pl.BlockSpec((1, LANES), const),
                  pl.BlockSpec((1, D_SSM), const),
                  pl.BlockSpec((1, D_SSM), const),
                  pl.BlockSpec((L, L), const),
                  pl.BlockSpec((LANES, D_SSM), const)],
        out_specs=pl.BlockSpec((L, D_SSM), lambda c: (c, 0)),
        out_shape=jax.ShapeDtypeStruct((s, D_SSM), BF16),
        scratch_shapes=[pltpu.VMEM((XBC_COLS // LANES, SUBLANES + L, LANES), F32),
                        pltpu.VMEM((L, XBC_COLS), F32),
                        pltpu.VMEM((L, D_SSM), F32),
                        pltpu.VMEM((SSM_GROUPS, SSM_STATE, HEADS_PER_GROUP * HEAD_DIM), F32)],
        compiler_params=_params(("arbitrary",)),
        name="ssd",
    )(xbc, z, dt, conv_w, conv_b.reshape(1, XBC_COLS), dtb, alog, dexp,
      norm_g.reshape(1, D_SSM), tri, expand)


def _out_proj_kernel(ya_ref, ys_ref, w_ref, res_ref, g_ref, o_ref, hg_ref, r_ref, ss_ref):
    j = pl.program_id(1)
    bn = w_ref.shape[1]

    @pl.when(j == 0)
    def _():
        ss_ref[...] = jnp.zeros_like(ss_ref)

    acc = jnp.dot(ya_ref[...], w_ref[0:D_ATTN, :], preferred_element_type=F32)
    acc = acc + jnp.dot(ys_ref[...], w_ref[D_ATTN:D_ATTN + D_SSM, :], preferred_element_type=F32)
    h = res_ref[...] + acc
    o_ref[...] = h
    hg_ref[...] = (h * g_ref[...]).astype(hg_ref.dtype)
    sq = h * h
    part = sq[:, 0:LANES]
    for c in range(1, bn // LANES):
        part = part + sq[:, c * LANES:(c + 1) * LANES]
    ss_ref[...] += part

    @pl.when(j == pl.num_programs(1) - 1)
    def _():
        ms = jnp.sum(ss_ref[...], axis=-1, keepdims=True) * (1.0 / D_MODEL)
        r_ref[...] = lax.rsqrt(ms + EPS)


def _out_proj(ya, ys, w, res, g, bm=1024, bn=1024):
    m = ya.shape[0]
    k, n = w.shape
    return pl.pallas_call(
        _out_proj_kernel,
        grid=(m // bm, n // bn),
        in_specs=[pl.BlockSpec((bm, D_ATTN), lambda i, j: (i, 0)),
                  pl.BlockSpec((bm, D_SSM), lambda i, j: (i, 0)),
                  pl.BlockSpec((k, bn), lambda i, j: (0, j)),
                  pl.BlockSpec((bm, bn), lambda i, j: (i, j)),
                  pl.BlockSpec((1, bn), lambda i, j: (0, j))],
        out_specs=[pl.BlockSpec((bm, bn), lambda i, j: (i, j)),
                   pl.BlockSpec((bm, bn), lambda i, j: (i, j)),
                   pl.BlockSpec((bm, 1), lambda i, j: (i, 0))],
        out_shape=[jax.ShapeDtypeStruct((m, n), F32),
                   jax.ShapeDtypeStruct((m, n), BF16),
                   jax.ShapeDtypeStruct((m, 1), F32)],
        scratch_shapes=[pltpu.VMEM((bm, LANES), F32)],
        compiler_params=_params(("arbitrary", "arbitrary"), vmem=VMEM_LIMIT_HIGH),
        name="out_proj",
    )(ya, ys, w, res, g.reshape(1, n))


def _ffn_up_kernel(x_ref, r_ref, wg_ref, wu_ref, cw_ref, cb_ref, o_ref, ubuf_ref, wbf_ref):
    t = pl.program_id(0)
    n_steps = pl.num_programs(0)
    n_row_blocks = (n_steps - 1) // N_FFN_TILES
    bm = x_ref.shape[0]
    ft = FFN_TILE
    planes_per_half = ft // LANES
    row_block_new = jnp.minimum(t, n_steps - 2) % n_row_blocks

    @pl.when(t == 0)
    def _():
        ubuf_ref[...] = jnp.zeros_like(ubuf_ref)

    @pl.when((row_block_new == 0) & (t < n_steps - 1))
    def _():
        wbf_ref[0] = wg_ref[...].astype(BF16)
        wbf_ref[1] = wu_ref[...].astype(BF16)

    f_done = jnp.maximum(t - 1, 0) // n_row_blocks
    for q in range(planes_per_half):
        conv = []
        for half in range(2):
            plane = half * planes_per_half + q
            col = pl.multiple_of(half * D_FF + f_done * ft + q * LANES, LANES)
            acc = ubuf_ref[plane, SUBLANES - 2:SUBLANES - 2 + bm, :] * cw_ref[0:1, pl.ds(col, LANES)]
            acc = acc + ubuf_ref[plane, SUBLANES - 1:SUBLANES - 1 + bm, :] * cw_ref[1:2, pl.ds(col, LANES)]
            acc = acc + ubuf_ref[plane, SUBLANES:SUBLANES + bm, :] * cw_ref[2:3, pl.ds(col, LANES)]
            conv.append(acc + cb_ref[:, pl.ds(col, LANES)])
        gate, up = conv
        o_ref[:, q * LANES:(q + 1) * LANES] = (gate * jax.nn.sigmoid(gate) * up).astype(o_ref.dtype)
    last_rows = ubuf_ref[:, bm:bm + SUBLANES, :]
    ubuf_ref[:, 0:SUBLANES, :] = jnp.where(row_block_new == 0, jnp.zeros_like(last_rows), last_rows)

    x = x_ref[...]
    r = r_ref[...]
    for half in range(2):
        u = jnp.dot(x, wbf_ref[half], preferred_element_type=F32) * r
        for q in range(planes_per_half):
            ubuf_ref[half * planes_per_half + q, SUBLANES:SUBLANES + bm, :] = u[:, q * LANES:(q + 1) * LANES]


def _ffn_up(hg, r, w, cw, cb, bm=1024):
    m, k = hg.shape
    nf = N_FFN_TILES
    nrb = m // bm
    n_tiles = nrb * nf
    n_planes = 2 * FFN_TILE // LANES
    cur = lambda t: jnp.minimum(t, n_tiles - 1)
    done = lambda t: jnp.maximum(t - 1, 0)
    return pl.pallas_call(
        _ffn_up_kernel,
        grid=(n_tiles + 1,),
        in_specs=[pl.BlockSpec((bm, k), lambda t: (cur(t) % nrb, 0)),
                  pl.BlockSpec((bm, 1), lambda t: (cur(t) % nrb, 0)),
                  pl.BlockSpec((k, FFN_TILE), lambda t: (0, cur(t) // nrb)),
                  pl.BlockSpec((k, FFN_TILE), lambda t: (0, nf + cur(t) // nrb)),
                  pl.BlockSpec((FFN_CONV, 2 * D_FF), lambda t: (0, 0)),
                  pl.BlockSpec((1, 2 * D_FF), lambda t: (0, 0))],
        out_specs=pl.BlockSpec((bm, FFN_TILE), lambda t: (done(t) % nrb, done(t) // nrb)),
        out_shape=jax.ShapeDtypeStruct((m, D_FF), BF16),
        scratch_shapes=[pltpu.VMEM((n_planes, SUBLANES + bm, LANES), F32),
                        pltpu.VMEM((2, k, FFN_TILE), BF16)],
        compiler_params=_params(("arbitrary",)),
        name="ffn_up",
    )(hg, r, w, w, cw, cb)


def _ffn_down_kernel(a_ref, w_ref, res_ref, o_ref, obf_ref):
    h = res_ref[...] + jnp.dot(a_ref[...], w_ref[...], preferred_element_type=F32)
    o_ref[...] = h
    obf_ref[...] = h.astype(obf_ref.dtype)


def _ffn_down(act, w, res, bm=512, bn=1024):
    m, k = act.shape
    n = w.shape[1]
    return pl.pallas_call(
        _ffn_down_kernel,
        grid=(n // bn, m // bm),
        in_specs=[pl.BlockSpec((bm, k), lambda j, i: (i, 0)),
                  pl.BlockSpec((k, bn), lambda j, i: (0, j), pipeline_mode=pl.Buffered(1)),
                  pl.BlockSpec((bm, bn), lambda j, i: (i, j))],
        out_specs=[pl.BlockSpec((bm, bn), lambda j, i: (i, j)),
                   pl.BlockSpec((bm, bn), lambda j, i: (i, j))],
        out_shape=[jax.ShapeDtypeStruct((m, n), F32), jax.ShapeDtypeStruct((m, n), BF16)],
        compiler_params=_params(("arbitrary", "arbitrary"), vmem=VMEM_LIMIT_HIGH),
        name="ffn_down",
    )(act, w, res)


def _ple_kernel(hb_ref, wg_ref, bg_ref, p_ref, wp_ref, h_ref, gf_ref, o_ref, ss_ref):
    j = pl.program_id(1)
    nj = pl.num_programs(1)
    bn = wg_ref.shape[1]

    @pl.when(j == 0)
    def _():
        ss_ref[...] = jnp.zeros_like(ss_ref)

    gate = jax.nn.sigmoid(jnp.dot(hb_ref[...], wg_ref[...], preferred_element_type=F32) + bg_ref[...])
    emb = jnp.dot(p_ref[...].astype(BF16), wp_ref[...], preferred_element_type=F32)
    h = h_ref[...] + gate * emb
    sq = h * h
    part = sq[:, 0:LANES]
    for c in range(1, bn // LANES):
        part = part + sq[:, c * LANES:(c + 1) * LANES]
    ss_ref[...] += part
    col = pl.multiple_of(j * bn, bn)
    o_ref[:, pl.ds(col, bn)] = h

    @pl.when(j == nj - 1)
    def _():
        ms = jnp.sum(ss_ref[...], axis=-1, keepdims=True) * (1.0 / D_MODEL)
        o_ref[...] = o_ref[...] * lax.rsqrt(ms + EPS) * gf_ref[...]


def _ple(hb, wg, bg, p, wp, h, gf, bm=512, bn=1024):
    m, d = h.shape
    return pl.pallas_call(
        _ple_kernel,
        grid=(m // bm, d // bn),
        in_specs=[pl.BlockSpec((bm, d), lambda i, j: (i, 0)),
                  pl.BlockSpec((d, bn), lambda i, j: (0, j)),
                  pl.BlockSpec((1, bn), lambda i, j: (0, j)),
                  pl.BlockSpec((bm, PLE_DIM), lambda i, j: (i, 0)),
                  pl.BlockSpec((PLE_DIM, bn), lambda i, j: (0, j)),
                  pl.BlockSpec((bm, bn), lambda i, j: (i, j)),
                  pl.BlockSpec((1, d), lambda i, j: (0, 0))],
        out_specs=pl.BlockSpec((bm, d), lambda i, j: (i, 0)),
        out_shape=jax.ShapeDtypeStruct((m, d), F32),
        scratch_shapes=[pltpu.VMEM((bm, LANES), F32)],
        compiler_params=_params(("arbitrary", "arbitrary"), vmem=VMEM_LIMIT_HIGH),
        name="ple",
    )(hb, wg, bg, p, wp, h, gf)


def kernel(x, p, positions, norm_mix_g, w_in, attn_sinks, ssm_conv_w, ssm_conv_b, ssm_dt_bias,
           ssm_a_log, ssm_d, ssm_norm_g, w_o, norm_ffn_g, w_up, ffn_conv_w, ffn_conv_b, w_down,
           ple_gate_w, ple_gate_b, ple_proj, norm_final_g):
    b, s, d = x.shape
    assert (b, s, d) == (1, SEQ, D_MODEL) and w_in.shape[0] == 1
    h = x.reshape(s, d)
    pos = positions.reshape(s, 1)
    half = HEAD_DIM // 2
    inv_freq = ROPE_THETA ** (-jnp.arange(half, dtype=F32) / half)
    freq = jnp.tile(inv_freq, LANES // half).reshape(1, LANES)
    w_dt = lax.slice(w_in, (0, 0, DT_COL0), (1, d, DT_COL0 + N_SSM_HEADS)).reshape(d, N_SSM_HEADS)
    w_dt = jnp.pad(w_dt, ((0, 0), (0, LANES - N_SSM_HEADS)))

    hn = _norm_cast(h, norm_mix_g[0])
    w_qkv, w_z, w_xbc = _split_w_in(jnp.swapaxes(w_in, 1, 2))
    qkv, dt = _proj(hn, w_qkv, QKV_TILE, "in_proj_qkv", w_dt=w_dt.astype(BF16))
    z, = _proj(hn, w_z, SSM_TILE, "in_proj_z")
    xbc, = _proj(hn, w_xbc, SSM_TILE, "in_proj_xbc")
    y_attn = _swa(qkv, pos, freq, attn_sinks[0])
    y_ssm = _ssd(xbc, z, dt, ssm_conv_w[0], ssm_conv_b[0], ssm_dt_bias[0], ssm_a_log[0], ssm_d[0],
                 ssm_norm_g[0])
    h1, hg, r = _out_proj(y_attn, y_ssm, w_o[0].astype(BF16), h, norm_ffn_g[0])
    act = _ffn_up(hg, r, w_up[0], ffn_conv_w[0], ffn_conv_b[0].reshape(1, 2 * D_FF))
    h2, h2b = _ffn_down(act, w_down[0].astype(BF16), h1)
    out = _ple(h2b, ple_gate_w[0].astype(BF16), ple_gate_b[0].reshape(1, d), p[0].reshape(s, PLE_DIM),
               ple_proj[0].astype(BF16), h2, norm_final_g.reshape(1, d))
    return out.reshape(b, s, d)
```

```python
import jax
import jax.numpy as jnp
from jax import lax
from jax.experimental import pallas as pl
from jax.experimental.pallas import tpu as pltpu

F32 = jnp.float32
BF16 = jnp.bfloat16

D_MODEL = 4096
SEQ = 16384
PLE_DIM = 256
D_ATTN = 2048
D_SSM = 2048
HEAD_DIM = 64
N_Q_HEADS = 32
N_KV_HEADS = 4
Q_PER_KV = 8
WINDOW = 128
ROPE_THETA = 10000.0
N_SSM_HEADS = 32
SSM_STATE = 128
SSM_GROUPS = 8
HEADS_PER_GROUP = 4
SSM_CONV = 4
CHUNK = 128
D_FF = 11008
FFN_CONV = 3
EPS = 1e-6
KV_COLS = N_KV_HEADS * HEAD_DIM
QKV_COLS = D_ATTN + 2 * KV_COLS
XBC_COLS = D_SSM + 2 * SSM_GROUPS * SSM_STATE
DT_COL0 = QKV_COLS + D_SSM + XBC_COLS

LANES = 128
SUBLANES = 8
VMEM_LIMIT = 56 * 1024 * 1024
VMEM_LIMIT_HIGH = 62 * 1024 * 1024

QKV_TILE = QKV_COLS // 2
SSM_TILE = 1024
FFN_TILE = 256
N_FFN_TILES = D_FF // FFN_TILE


def _params(sem, vmem=VMEM_LIMIT, flags=None):
    return pltpu.CompilerParams(dimension_semantics=sem, vmem_limit_bytes=vmem, flags=flags)


def _norm_cast_kernel(x_ref, g_ref, o_ref):
    x = x_ref[...]
    ms = jnp.mean(x * x, axis=-1, keepdims=True)
    o_ref[...] = (x * lax.rsqrt(ms + EPS) * g_ref[...]).astype(o_ref.dtype)


def _norm_cast(x, g, bm=512):
    m, d = x.shape
    return pl.pallas_call(
        _norm_cast_kernel,
        grid=(m // bm,),
        in_specs=[pl.BlockSpec((bm, d), lambda i: (i, 0)),
                  pl.BlockSpec((1, d), lambda i: (0, 0))],
        out_specs=pl.BlockSpec((bm, d), lambda i: (i, 0)),
        out_shape=jax.ShapeDtypeStruct((m, d), BF16),
        compiler_params=_params(("arbitrary",)),
        name="norm_cast",
    )(x, g.reshape(1, d))


W_IN_CAST_TILE = 512
W_IN_QKV_TILES = QKV_COLS // W_IN_CAST_TILE
W_IN_Z_TILES = D_SSM // W_IN_CAST_TILE
W_IN_XBC_TILES = XBC_COLS // W_IN_CAST_TILE


def _split_w_in_kernel(wt_ref, qkv_ref, z_ref, xbc_ref):
    j = pl.program_id(0)

    @pl.when(j < W_IN_QKV_TILES)
    def _():
        qkv_ref[...] = wt_ref[...].T.astype(qkv_ref.dtype)

    @pl.when((j >= W_IN_QKV_TILES) & (j < W_IN_QKV_TILES + W_IN_Z_TILES))
    def _():
        z_ref[...] = wt_ref[...].T.astype(z_ref.dtype)

    @pl.when(j >= W_IN_QKV_TILES + W_IN_Z_TILES)
    def _():
        xbc_ref[...] = wt_ref[...].T.astype(xbc_ref.dtype)


def _split_w_in(wt):
    k = wt.shape[2]
    bn = W_IN_CAST_TILE
    q, z = W_IN_QKV_TILES, W_IN_Z_TILES
    return pl.pallas_call(
        _split_w_in_kernel,
        grid=(q + z + W_IN_XBC_TILES,),
        in_specs=[pl.BlockSpec((None, bn, k), lambda j: (0, j, 0))],
        out_specs=[pl.BlockSpec((k, bn), lambda j: (0, jnp.minimum(j, q - 1))),
                   pl.BlockSpec((k, bn), lambda j: (0, jnp.clip(j - q, 0, z - 1))),
                   pl.BlockSpec((k, bn), lambda j: (0, jnp.clip(j - q - z, 0, W_IN_XBC_TILES - 1)))],
        out_shape=[jax.ShapeDtypeStruct((k, QKV_COLS), BF16),
                   jax.ShapeDtypeStruct((k, D_SSM), BF16),
                   jax.ShapeDtypeStruct((k, XBC_COLS), BF16)],
        compiler_params=_params(("arbitrary",)),
        name="split_w_in",
    )(wt)


def _proj_kernel(x_ref, w_ref, o_ref):
    o_ref[...] = jnp.dot(x_ref[...], w_ref[...], preferred_element_type=F32)


def _proj_dt_kernel(x_ref, w_ref, wdt_ref, o_ref, dt_ref):
    o_ref[...] = jnp.dot(x_ref[...], w_ref[...], preferred_element_type=F32)

    @pl.when(pl.program_id(1) == 0)
    def _():
        dt_ref[...] = jnp.dot(x_ref[...], wdt_ref[...], preferred_element_type=F32)


def _proj(hn, w, bn, name, w_dt=None, bm=1024):
    m, k = hn.shape
    n = w.shape[1]
    in_specs = [pl.BlockSpec((bm, k), lambda i, j: (i, 0)),
                pl.BlockSpec((k, bn), lambda i, j: (0, j))]
    out_specs = [pl.BlockSpec((bm, bn), lambda i, j: (i, j))]
    out_shape = [jax.ShapeDtypeStruct((m, n), F32)]
    args = [hn, w]
    if w_dt is not None:
        in_specs.append(pl.BlockSpec((k, LANES), lambda i, j: (0, 0)))
        out_specs.append(pl.BlockSpec((bm, LANES), lambda i, j: (i, 0)))
        out_shape.append(jax.ShapeDtypeStruct((m, LANES), F32))
        args.append(w_dt)
    return pl.pallas_call(
        _proj_kernel if w_dt is None else _proj_dt_kernel,
        grid=(m // bm, n // bn),
        in_specs=in_specs,
        out_specs=out_specs,
        out_shape=out_shape,
        compiler_params=_params(("arbitrary", "arbitrary")),
        name=name,
    )(*args)


def _rope_tile(x, cos, sin_signed, first_half):
    rot = jnp.where(first_half, pltpu.roll(x, LANES - HEAD_DIM // 2, 1),
                    pltpu.roll(x, HEAD_DIM // 2, 1))
    return x * cos + rot * sin_signed


LOG2E = 1.4426950408889634
VT_ROWS = 80


def _swa_kernel(pos_ref, freq_ref, sink_ref, q_ref, k_ref, v_ref, o_ref, kband_ref, vt_ref):
    n = pl.program_id(0)
    w = WINDOW

    lane = lax.broadcasted_iota(jnp.int32, (1, LANES), 1)
    low_half = lane < HEAD_DIM
    first_half = (lane % HEAD_DIM) < (HEAD_DIM // 2)

    @pl.when(n == 0)
    def _():
        kband_ref[...] = jnp.zeros_like(kband_ref)
        vt_ref[...] = jnp.zeros_like(vt_ref)
        ones_row = jnp.where(lax.broadcasted_iota(jnp.int32, (VT_ROWS - HEAD_DIM, 2 * w), 0) == 0, 1.0, 0.0)
        for kh in range(N_KV_HEADS):
            vt_ref[kh, HEAD_DIM:VT_ROWS, :] = ones_row.astype(BF16)

    ang = pos_ref[...].astype(F32) * freq_ref[...]
    cos = jnp.cos(ang)
    sin_signed = jnp.where(first_half, -jnp.sin(ang), jnp.sin(ang))

    for kh in range(N_KV_HEADS):
        kband_ref[kh, 0:w, :] = kband_ref[kh, w:2 * w, :]
        vt_ref[kh, 0:HEAD_DIM, 0:w] = vt_ref[kh, 0:HEAD_DIM, w:2 * w]
    for t in range(KV_COLS // LANES):
        k_rot = _rope_tile(k_ref[:, t * LANES:(t + 1) * LANES], cos, sin_signed, first_half)
        k_sw = pltpu.roll(k_rot, HEAD_DIM, 1)
        kband_ref[2 * t, w:2 * w, :] = jnp.where(low_half, k_rot, k_sw).astype(BF16)
        kband_ref[2 * t + 1, w:2 * w, :] = jnp.where(low_half, k_sw, k_rot).astype(BF16)
        v_t = v_ref[:, t * LANES:(t + 1) * LANES].T
        vt_ref[2 * t, 0:HEAD_DIM, w:2 * w] = v_t[0:HEAD_DIM, :].astype(BF16)
        vt_ref[2 * t + 1, 0:HEAD_DIM, w:2 * w] = v_t[HEAD_DIM:2 * HEAD_DIM, :].astype(BF16)

    ki = lax.broadcasted_iota(jnp.int32, (2 * w, w), 0)
    qi = lax.broadcasted_iota(jnp.int32, (2 * w, w), 1)
    prev_off = jnp.where(n > 0, 0, 2 * w)
    valid = ((ki < w) & (ki > qi + prev_off)) | ((ki >= w) & (ki - w <= qi))
    qscale = HEAD_DIM ** -0.5 * LOG2E
    pairs = Q_PER_KV // 2
    eye = (lax.broadcasted_iota(jnp.int32, (LANES, LANES), 0)
           == lax.broadcasted_iota(jnp.int32, (LANES, LANES), 1)).astype(BF16)

    for kh in range(N_KV_HEADS):
        lhs = []
        for j in range(pairs):
            t = kh * pairs + j
            q_rot = _rope_tile(q_ref[:, t * LANES:(t + 1) * LANES], cos, sin_signed, first_half) * qscale
            lhs.append(jnp.where(low_half, q_rot, 0.0).astype(BF16))
            lhs.append(jnp.where(low_half, 0.0, q_rot).astype(BF16))
        st = lax.dot_general(kband_ref[kh], jnp.concatenate(lhs, axis=0), (((1,), (1,)), ((), ())),
                             preferred_element_type=F32)
        probs = []
        sink_terms = []
        for hq in range(Q_PER_KV):
            s = jnp.where(valid, st[:, hq * w:(hq + 1) * w], -jnp.inf)
            sink = sink_ref[kh * Q_PER_KV + hq] * LOG2E
            m = jnp.maximum(jnp.max(s, axis=0, keepdims=True), sink)
            probs.append(jnp.exp2(s - m).astype(BF16))
            sink_terms.append(jnp.exp2(sink - m))
        ot = jnp.dot(vt_ref[kh], jnp.concatenate(probs, axis=1), preferred_element_type=F32)
        for j in range(pairs):
            halves = []
            for r in range(2):
                hq = 2 * j + r
                cols = slice(hq * w, (hq + 1) * w)
                denom = ot[HEAD_DIM:HEAD_DIM + 1, cols] + sink_terms[hq]
                halves.append(ot[0:HEAD_DIM, cols] / denom)
            pair_t = jnp.concatenate(halves, axis=0).astype(BF16)
            out = lax.dot_general(eye, pair_t, (((1,), (1,)), ((), ())), preferred_element_type=F32)
            t = kh * pairs + j
            o_ref[:, t * LANES:(t + 1) * LANES] = out.astype(o_ref.dtype)


def _swa(qkv, pos, freq, sinks):
    s = qkv.shape[0]
    w = WINDOW
    return pl.pallas_call(
        _swa_kernel,
        grid=(s // w,),
        in_specs=[pl.BlockSpec((w, 1), lambda n: (n, 0)),
                  pl.BlockSpec((1, LANES), lambda n: (0, 0)),
                  pl.BlockSpec(memory_space=pltpu.SMEM),
                  pl.BlockSpec((w, D_ATTN), lambda n: (n, 0)),
                  pl.BlockSpec((w, KV_COLS), lambda n: (n, D_ATTN // KV_COLS)),
                  pl.BlockSpec((w, KV_COLS), lambda n: (n, D_ATTN // KV_COLS + 1))],
        out_specs=pl.BlockSpec((w, D_ATTN), lambda n: (n, 0)),
        out_shape=jax.ShapeDtypeStruct((s, D_ATTN), BF16),
        scratch_shapes=[pltpu.VMEM((N_KV_HEADS, 2 * w, LANES), BF16),
                        pltpu.VMEM((N_KV_HEADS, VT_ROWS, 2 * w), BF16)],
        compiler_params=_params(("arbitrary",)),
        name="swa",
    )(pos, freq, sinks, qkv, qkv, qkv)


def _split3(v):
    hi = v.astype(BF16)
    r = v - hi.astype(F32)
    mid = r.astype(BF16)
    lo = (r - mid.astype(F32)).astype(BF16)
    return hi, mid, lo


def _dot_exact_rhs(a, b_parts):
    acc = jnp.dot(a, b_parts[0], preferred_element_type=F32)
    for part in b_parts[1:]:
        acc = acc + jnp.dot(a, part, preferred_element_type=F32)
    return acc


def _dot_exact_lhs(a_parts, b):
    acc = jnp.dot(a_parts[0], b, preferred_element_type=F32)
    for part in a_parts[1:]:
        acc = acc + jnp.dot(part, b, preferred_element_type=F32)
    return acc


def _ssd_kernel(xbc_ref, z_ref, dt_ref, cw_ref, cb_ref, dtb_ref, alog_ref, dexp_ref,
                ng_ref, tri_ref, expand_ref, o_ref, xpad_ref, xc_ref, y_ref, state_ref):
    c = pl.program_id(0)
    L = CHUNK
    G, R, P, N = SSM_GROUPS, HEADS_PER_GROUP, HEAD_DIM, SSM_STATE
    GP = R * P

    @pl.when(c == 0)
    def _():
        state_ref[...] = jnp.zeros_like(state_ref)
        xpad_ref[:, L:L + SUBLANES, :] = jnp.zeros((XBC_COLS // LANES, SUBLANES, LANES), F32)

    xpad_ref[:, 0:SUBLANES, :] = xpad_ref[:, L:L + SUBLANES, :]

    for p in range(XBC_COLS // LANES):
        cs = slice(p * LANES, (p + 1) * LANES)
        xpad_ref[p, SUBLANES:SUBLANES + L, :] = xbc_ref[:, cs]
        acc = xpad_ref[p, SUBLANES - 3:SUBLANES - 3 + L, :] * cw_ref[0:1, cs]
        for k in range(1, SSM_CONV):
            acc = acc + xpad_ref[p, SUBLANES - 3 + k:SUBLANES - 3 + k + L, :] * cw_ref[k:k + 1, cs]
        acc = acc + cb_ref[:, cs]
        xc_ref[:, cs] = acc * jax.nn.sigmoid(acc)

    dt_raw = dt_ref[...] + dtb_ref[...]
    dt = jnp.maximum(dt_raw, 0.0) + jnp.log1p(jnp.exp(-jnp.abs(dt_raw)))
    a = -jnp.exp(alog_ref[...])
    da = dt * a
    a_cs = _dot_exact_rhs(tri_ref[...], _split3(da))
    a_cs_t = a_cs.T
    expand = expand_ref[...]
    acs_x = _dot_exact_lhs(_split3(a_cs), expand)
    dt_x = _dot_exact_lhs(_split3(dt), expand)

    li = lax.broadcasted_iota(jnp.int32, (L, L), 0)
    si = lax.broadcasted_iota(jnp.int32, (L, L), 1)
    causal = li >= si
    lane = lax.broadcasted_iota(jnp.int32, (1, LANES), 1)
    low_half = lane < P

    for g in range(G):
        xs = slice(g * GP, (g + 1) * GP)
        x_g = xc_ref[:, xs]
        b_g = xc_ref[:, D_SSM + g * N:D_SSM + (g + 1) * N].astype(BF16)
        c_g = xc_ref[:, D_SSM + G * N + g * N:D_SSM + G * N + (g + 1) * N].astype(BF16)
        acs_g = acs_x[:, xs]
        xdt_g = x_g * dt_x[:, xs]
        cb = lax.dot_general(c_g, b_g, (((1,), (1,)), ((), ())), preferred_element_type=F32)

        y_pairs = []
        for j in range(R // 2):
            ws = []
            for r in range(2):
                h = g * R + 2 * j + r
                seg = a_cs[:, h:h + 1] - a_cs_t[h:h + 1, :]
                decay = jnp.exp(jnp.where(causal, seg, -jnp.inf))
                ws.append((cb * decay).astype(BF16))
            wcat = jnp.concatenate(ws, axis=1)
            xp = xdt_g[:, j * LANES:(j + 1) * LANES]
            rhs = jnp.concatenate([jnp.where(low_half, xp, 0.0), jnp.where(low_half, 0.0, xp)],
                                  axis=0).astype(BF16)
            y_pairs.append(jnp.dot(wcat, rhs, preferred_element_type=F32))
        y_diag = jnp.concatenate(y_pairs, axis=1)

        a_last = acs_g[L - 1:L, :]
        xdte = (xdt_g * jnp.exp(a_last - acs_g)).astype(BF16)
        new_state = lax.dot_general(b_g, xdte, (((0,), (0,)), ((), ())),
                                    preferred_element_type=F32)
        prev = state_ref[g]
        y_off = jnp.dot(c_g, prev.astype(BF16), preferred_element_type=F32) * jnp.exp(acs_g)
        state_ref[g] = prev * jnp.exp(a_last) + new_state
        y_ref[:, xs] = y_diag + y_off + x_g * dexp_ref[:, xs]

    z = z_ref[...]
    y = y_ref[...] * (z * jax.nn.sigmoid(z))
    ms = jnp.mean(y * y, axis=-1, keepdims=True)
    o_ref[...] = (y * lax.rsqrt(ms + EPS) * ng_ref[...]).astype(o_ref.dtype)


def _ssd(xbc, z, dt, conv_w, conv_b, dt_bias, a_log, d_skip, norm_g):
    s = xbc.shape[0]
    L = CHUNK
    pad = LANES - N_SSM_HEADS
    dtb = jnp.pad(dt_bias, (0, pad)).reshape(1, LANES)
    alog = jnp.pad(a_log, (0, pad)).reshape(1, LANES)
    dexp = jnp.repeat(d_skip, HEAD_DIM).reshape(1, D_SSM)
    tri = (jnp.arange(L)[:, None] >= jnp.arange(L)[None, :]).astype(BF16)
    expand = (jnp.arange(LANES)[:, None] == (jnp.arange(D_SSM)[None, :] // HEAD_DIM)).astype(BF16)
    const = lambda c: (0, 0)
    return pl.pallas_call(
        _ssd_kernel,
        grid=(s // L,),
        in_specs=[pl.BlockSpec((L, XBC_COLS), lambda c: (c, 0)),
                  pl.BlockSpec((L, D_SSM), lambda c: (c, 0)),
                  pl.BlockSpec((L, LANES), lambda c: (c, 0)),
                  pl.BlockSpec((SSM_CONV, XBC_COLS), const),
                  pl.BlockSpec((1, XBC_COLS), const),
                  pl.BlockSpec((1, LANES), const),
                  pl.BlockSpec((1, LANES), const),
                  pl.BlockSpec((1, D_SSM), const),
                  pl.BlockSpec((1, D_SSM), const),
                  pl.BlockSpec((L, L), const),
                  pl.BlockSpec((LANES, D_SSM), const)],
        out_specs=pl.BlockSpec((L, D_SSM), lambda c: (c, 0)),
        out_shape=jax.ShapeDtypeStruct((s, D_SSM), BF16),
        scratch_shapes=[pltpu.VMEM((XBC_COLS // LANES, SUBLANES + L, LANES), F32),
                        pltpu.VMEM((L, XBC_COLS), F32),
                        pltpu.VMEM((L, D_SSM), F32),
                        pltpu.VMEM((SSM_GROUPS, SSM_STATE, HEADS_PER_GROUP * HEAD_DIM), F32)],
        compiler_params=_params(("arbitrary",)),
        name="ssd",
    )(xbc, z, dt, conv_w, conv_b.reshape(1, XBC_COLS), dtb, alog, dexp,
      norm_g.reshape(1, D_SSM), tri, expand)


def _out_proj_kernel(ya_ref, ys_ref, w_ref, res_ref, g_ref, o_ref, hg_ref, r_ref, ss_ref):
    j = pl.program_id(1)
    bn = w_ref.shape[1]

    @pl.when(j == 0)
    def _():
        ss_ref[...] = jnp.zeros_like(ss_ref)

    acc = jnp.dot(ya_ref[...], w_ref[0:D_ATTN, :], preferred_element_type=F32)
    acc = acc + jnp.dot(ys_ref[...], w_ref[D_ATTN:D_ATTN + D_SSM, :], preferred_element_type=F32)
    h = res_ref[...] + acc
    o_ref[...] = h
    hg_ref[...] = (h * g_ref[...]).astype(hg_ref.dtype)
    sq = h * h
    part = sq[:, 0:LANES]
    for c in range(1, bn // LANES):
        part = part + sq[:, c * LANES:(c + 1) * LANES]
    ss_ref[...] += part

    @pl.when(j == pl.num_programs(1) - 1)
    def _():
        ms = jnp.sum(ss_ref[...], axis=-1, keepdims=True) * (1.0 / D_MODEL)
        r_ref[...] = lax.rsqrt(ms + EPS)


def _out_proj(ya, ys, w, res, g, bm=1024, bn=1024):
    m = ya.shape[0]
    k, n = w.shape
    return pl.pallas_call(
        _out_proj_kernel,
        grid=(m // bm, n // bn),
        in_specs=[pl.BlockSpec((bm, D_ATTN), lambda i, j: (i, 0)),
                  pl.BlockSpec((bm, D_SSM), lambda i, j: (i, 0)),
                  pl.BlockSpec((k, bn), lambda i, j: (0, j)),
                  pl.BlockSpec((bm, bn), lambda i, j: (i, j)),
                  pl.BlockSpec((1, bn), lambda i, j: (0, j))],
        out_specs=[pl.BlockSpec((bm, bn), lambda i, j: (i, j)),
                   pl.BlockSpec((bm, bn), lambda i, j: (i, j)),
                   pl.BlockSpec((bm, 1), lambda i, j: (i, 0))],
        out_shape=[jax.ShapeDtypeStruct((m, n), F32),
                   jax.ShapeDtypeStruct((m, n), BF16),
                   jax.ShapeDtypeStruct((m, 1), F32)],
        scratch_shapes=[pltpu.VMEM((bm, LANES), F32)],
        compiler_params=_params(("arbitrary", "arbitrary"), vmem=VMEM_LIMIT_HIGH),
        name="out_proj",
    )(ya, ys, w, res, g.reshape(1, n))


def _ffn_up_kernel(x_ref, r_ref, wg_ref, wu_ref, cw_ref, cb_ref, o_ref, ubuf_ref, wbf_ref):
    t = pl.program_id(0)
    n_steps = pl.num_programs(0)
    n_row_blocks = (n_steps - 1) // N_FFN_TILES
    bm = x_ref.shape[0]
    ft = FFN_TILE
    planes_per_half = ft // LANES
    row_block_new = jnp.minimum(t, n_steps - 2) % n_row_blocks

    @pl.when(t == 0)
    def _():
        ubuf_ref[...] = jnp.zeros_like(ubuf_ref)

    @pl.when((row_block_new == 0) & (t < n_steps - 1))
    def _():
        wbf_ref[0] = wg_ref[...].astype(BF16)
        wbf_ref[1] = wu_ref[...].astype(BF16)

    f_done = jnp.maximum(t - 1, 0) // n_row_blocks
    for q in range(planes_per_half):
        conv = []
        for half in range(2):
            plane = half * planes_per_half + q
            col = pl.multiple_of(half * D_FF + f_done * ft + q * LANES, LANES)
            acc = ubuf_ref[plane, SUBLANES - 2:SUBLANES - 2 + bm, :] * cw_ref[0:1, pl.ds(col, LANES)]
            acc = acc + ubuf_ref[plane, SUBLANES - 1:SUBLANES - 1 + bm, :] * cw_ref[1:2, pl.ds(col, LANES)]
            acc = acc + ubuf_ref[plane, SUBLANES:SUBLANES + bm, :] * cw_ref[2:3, pl.ds(col, LANES)]
            conv.append(acc + cb_ref[:, pl.ds(col, LANES)])
        gate, up = conv
        o_ref[:, q * LANES:(q + 1) * LANES] = (gate * jax.nn.sigmoid(gate) * up).astype(o_ref.dtype)
    last_rows = ubuf_ref[:, bm:bm + SUBLANES, :]
    ubuf_ref[:, 0:SUBLANES, :] = jnp.where(row_block_new == 0, jnp.zeros_like(last_rows), last_rows)

    x = x_ref[...]
    r = r_ref[...]
    for half in range(2):
        u = jnp.dot(x, wbf_ref[half], preferred_element_type=F32) * r
        for q in range(planes_per_half):
            ubuf_ref[half * planes_per_half + q, SUBLANES:SUBLANES + bm, :] = u[:, q * LANES:(q + 1) * LANES]


def _ffn_up(hg, r, w, cw, cb, bm=1024):
    m, k = hg.shape
    nf = N_FFN_TILES
    nrb = m // bm
    n_tiles = nrb * nf
    n_planes = 2 * FFN_TILE // LANES
    cur = lambda t: jnp.minimum(t, n_tiles - 1)
    done = lambda t: jnp.maximum(t - 1, 0)
    return pl.pallas_call(
        _ffn_up_kernel,
        grid=(n_tiles + 1,),
        in_specs=[pl.BlockSpec((bm, k), lambda t: (cur(t) % nrb, 0)),
                  pl.BlockSpec((bm, 1), lambda t: (cur(t) % nrb, 0)),
                  pl.BlockSpec((k, FFN_TILE), lambda t: (0, cur(t) // nrb)),
                  pl.BlockSpec((k, FFN_TILE), lambda t: (0, nf + cur(t) // nrb)),
                  pl.BlockSpec((FFN_CONV, 2 * D_FF), lambda t: (0, 0)),
                  pl.BlockSpec((1, 2 * D_FF), lambda t: (0, 0))],
        out_specs=pl.BlockSpec((bm, FFN_TILE), lambda t: (done(t) % nrb, done(t) // nrb)),
        out_shape=jax.ShapeDtypeStruct((m, D_FF), BF16),
        scratch_shapes=[pltpu.VMEM((n_planes, SUBLANES + bm, LANES), F32),
                        pltpu.VMEM((2, k, FFN_TILE), BF16)],
        compiler_params=_params(("arbitrary",)),
        name="ffn_up",
    )(hg, r, w, w, cw, cb)


def _ffn_down_kernel(a_ref, w_ref, res_ref, o_ref, obf_ref):
    h = res_ref[...] + jnp.dot(a_ref[...], w_ref[...], preferred_element_type=F32)
    o_ref[...] = h
    obf_ref[...] = h.astype(obf_ref.dtype)


def _ffn_down(act, w, res, bm=512, bn=1024):
    m, k = act.shape
    n = w.shape[1]
    return pl.pallas_call(
        _ffn_down_kernel,
        grid=(n // bn, m // bm),
        in_specs=[pl.BlockSpec((bm, k), lambda j, i: (i, 0)),
                  pl.BlockSpec((k, bn), lambda j, i: (0, j), pipeline_mode=pl.Buffered(1)),
                  pl.BlockSpec((bm, bn), lambda j, i: (i, j))],
        out_specs=[pl.BlockSpec((bm, bn), lambda j, i: (i, j)),
                   pl.BlockSpec((bm, bn), lambda j, i: (i, j))],
        out_shape=[jax.ShapeDtypeStruct((m, n), F32), jax.ShapeDtypeStruct((m, n), BF16)],
        compiler_params=_params(("arbitrary", "arbitrary"), vmem=VMEM_LIMIT_HIGH),
        name="ffn_down",
    )(act, w, res)


def _ple_kernel(hb_ref, wg_ref, bg_ref, p_ref, wp_ref, h_ref, gf_ref, o_ref, ss_ref):
    j = pl.program_id(1)
    nj = pl.num_programs(1)
    bn = wg_ref.shape[1]

    @pl.when(j == 0)
    def _():
        ss_ref[...] = jnp.zeros_like(ss_ref)

    gate = jax.nn.sigmoid(jnp.dot(hb_ref[...], wg_ref[...], preferred_element_type=F32) + bg_ref[...])
    emb = jnp.dot(p_ref[...].astype(BF16), wp_ref[...], preferred_element_type=F32)
    h = h_ref[...] + gate * emb
    sq = h * h
    part = sq[:, 0:LANES]
    for c in range(1, bn // LANES):
        part = part + sq[:, c * LANES:(c + 1) * LANES]
    ss_ref[...] += part
    col = pl.multiple_of(j * bn, bn)
    o_ref[:, pl.ds(col, bn)] = h

    @pl.when(j == nj - 1)
    def _():
        ms = jnp.sum(ss_ref[...], axis=-1, keepdims=True) * (1.0 / D_MODEL)
        o_ref[...] = o_ref[...] * lax.rsqrt(ms + EPS) * gf_ref[...]


def _ple(hb, wg, bg, p, wp, h, gf, bm=512, bn=1024):
    m, d = h.shape
    return pl.pallas_call(
        _ple_kernel,
        grid=(m // bm, d // bn),
        in_specs=[pl.BlockSpec((bm, d), lambda i, j: (i, 0)),
                  pl.BlockSpec((d, bn), lambda i, j: (0, j)),
                  pl.BlockSpec((1, bn), lambda i, j: (0, j)),
                  pl.BlockSpec((bm, PLE_DIM), lambda i, j: (i, 0)),
                  pl.BlockSpec((PLE_DIM, bn), lambda i, j: (0, j)),
                  pl.BlockSpec((bm, bn), lambda i, j: (i, j)),
                  pl.BlockSpec((1, d), lambda i, j: (0, 0))],
        out_specs=pl.BlockSpec((bm, d), lambda i, j: (i, 0)),
        out_shape=jax.ShapeDtypeStruct((m, d), F32),
        scratch_shapes=[pltpu.VMEM((bm, LANES), F32)],
        compiler_params=_params(("arbitrary", "arbitrary"), vmem=VMEM_LIMIT_HIGH),
        name="ple",
    )(hb, wg, bg, p, wp, h, gf)


def kernel(x, p, positions, norm_mix_g, w_in, attn_sinks, ssm_conv_w, ssm_conv_b, ssm_dt_bias,
           ssm_a_log, ssm_d, ssm_norm_g, w_o, norm_ffn_g, w_up, ffn_conv_w, ffn_conv_b, w_down,
           ple_gate_w, ple_gate_b, ple_proj, norm_final_g):
    b, s, d = x.shape
    assert (b, s, d) == (1, SEQ, D_MODEL) and w_in.shape[0] == 1
    h = x.reshape(s, d)
    pos = positions.reshape(s, 1)
    half = HEAD_DIM // 2
    inv_freq = ROPE_THETA ** (-jnp.arange(half, dtype=F32) / half)
    freq = jnp.tile(inv_freq, LANES // half).reshape(1, LANES)
    w_dt = lax.slice(w_in, (0, 0, DT_COL0), (1, d, DT_COL0 + N_SSM_HEADS)).reshape(d, N_SSM_HEADS)
    w_dt = jnp.pad(w_dt, ((0, 0), (0, LANES - N_SSM_HEADS)))

    hn = _norm_cast(h, norm_mix_g[0])
    w_qkv, w_z, w_xbc = _split_w_in(jnp.swapaxes(w_in, 1, 2))
    qkv, dt = _proj(hn, w_qkv, QKV_TILE, "in_proj_qkv", w_dt=w_dt.astype(BF16))
    z, = _proj(hn, w_z, SSM_TILE, "in_proj_z")
    xbc, = _proj(hn, w_xbc, SSM_TILE, "in_proj_xbc")
    y_attn = _swa(qkv, pos, freq, attn_sinks[0])
    y_ssm = _ssd(xbc, z, dt, ssm_conv_w[0], ssm_conv_b[0], ssm_dt_bias[0], ssm_a_log[0], ssm_d[0],
                 ssm_norm_g[0])
    h1, hg, r = _out_proj(y_attn, y_ssm, w_o[0].astype(BF16), h, norm_ffn_g[0])
    act = _ffn_up(hg, r, w_up[0], ffn_conv_w[0], ffn_conv_b[0].reshape(1, 2 * D_FF))
    h2, h2b = _ffn_down(act, w_down[0].astype(BF16), h1)
    out = _ple(h2b, ple_gate_w[0].astype(BF16), ple_gate_b[0].reshape(1, d), p[0].reshape(s, PLE_DIM),
               ple_proj[0].astype(BF16), h2, norm_final_g.reshape(1, d))
    return out.reshape(b, s, d)
```

```python
import jax
import jax.numpy as jnp
from jax import lax
from jax.experimental import pallas as pl
from jax.experimental.pallas import tpu as pltpu

F32 = jnp.float32
BF16 = jnp.bfloat16

D_MODEL = 4096
SEQ = 16384
PLE_DIM = 256
D_ATTN = 2048
D_SSM = 2048
HEAD_DIM = 64
N_Q_HEADS = 32
N_KV_HEADS = 4
Q_PER_KV = 8
WINDOW = 128
ROPE_THETA = 10000.0
N_SSM_HEADS = 32
SSM_STATE = 128
SSM_GROUPS = 8
HEADS_PER_GROUP = 4
SSM_CONV = 4
CHUNK = 128
D_FF = 11008
FFN_CONV = 3
EPS = 1e-6
KV_COLS = N_KV_HEADS * HEAD_DIM
QKV_COLS = D_ATTN + 2 * KV_COLS
XBC_COLS = D_SSM + 2 * SSM_GROUPS * SSM_STATE
DT_COL0 = QKV_COLS + D_SSM + XBC_COLS

LANES = 128
SUBLANES = 8
VMEM_LIMIT = 56 * 1024 * 1024
VMEM_LIMIT_HIGH = 62 * 1024 * 1024

QKV_TILE = QKV_COLS // 2
SSM_TILE = 1024
FFN_TILE = 256
N_FFN_TILES = D_FF // FFN_TILE


def _params(sem, vmem=VMEM_LIMIT, flags=None):
    return pltpu.CompilerParams(dimension_semantics=sem, vmem_limit_bytes=vmem, flags=flags)


def _norm_cast_kernel(x_ref, g_ref, o_ref):
    x = x_ref[...]
    ms = jnp.mean(x * x, axis=-1, keepdims=True)
    o_ref[...] = (x * lax.rsqrt(ms + EPS) * g_ref[...]).astype(o_ref.dtype)


def _norm_cast(x, g, bm=512):
    m, d = x.shape
    return pl.pallas_call(
        _norm_cast_kernel,
        grid=(m // bm,),
        in_specs=[pl.BlockSpec((bm, d), lambda i: (i, 0)),
                  pl.BlockSpec((1, d), lambda i: (0, 0))],
        out_specs=pl.BlockSpec((bm, d), lambda i: (i, 0)),
        out_shape=jax.ShapeDtypeStruct((m, d), BF16),
        compiler_params=_params(("arbitrary",)),
        name="norm_cast",
    )(x, g.reshape(1, d))


W_IN_CAST_TILE = 512
W_IN_QKV_TILES = QKV_COLS // W_IN_CAST_TILE
W_IN_Z_TILES = D_SSM // W_IN_CAST_TILE
W_IN_XBC_TILES = XBC_COLS // W_IN_CAST_TILE


def _split_w_in_kernel(wt_ref, qkv_ref, z_ref, xbc_ref):
    j = pl.program_id(0)

    @pl.when(j < W_IN_QKV_TILES)
    def _():
        qkv_ref[...] = wt_ref[...].T.astype(qkv_ref.dtype)

    @pl.when((j >= W_IN_QKV_TILES) & (j < W_IN_QKV_TILES + W_IN_Z_TILES))
    def _():
        z_ref[...] = wt_ref[...].T.astype(z_ref.dtype)

    @pl.when(j >= W_IN_QKV_TILES + W_IN_Z_TILES)
    def _():
        xbc_ref[...] = wt_ref[...].T.astype(xbc_ref.dtype)


def _split_w_in(wt):
    k = wt.shape[2]
    bn = W_IN_CAST_TILE
    q, z = W_IN_QKV_TILES, W_IN_Z_TILES
    return pl.pallas_call(
        _split_w_in_kernel,
        grid=(q + z + W_IN_XBC_TILES,),
        in_specs=[pl.BlockSpec((None, bn, k), lambda j: (0, j, 0))],
        out_specs=[pl.BlockSpec((k, bn), lambda j: (0, jnp.minimum(j, q - 1))),
                   pl.BlockSpec((k, bn), lambda j: (0, jnp.clip(j - q, 0, z - 1))),
                   pl.BlockSpec((k, bn), lambda j: (0, jnp.clip(j - q - z, 0, W_IN_XBC_TILES - 1)))],
        out_shape=[jax.ShapeDtypeStruct((k, QKV_COLS), BF16),
                   jax.ShapeDtypeStruct((k, D_SSM), BF16),
                   jax.ShapeDtypeStruct((k, XBC_COLS), BF16)],
        compiler_params=_params(("arbitrary",)),
        name="split_w_in",
    )(wt)


def _proj_kernel(x_ref, w_ref, o_ref):
    o_ref[...] = jnp.dot(x_ref[...], w_ref[...], preferred_element_type=F32)


def _proj_dt_kernel(x_ref, w_ref, wdt_ref, o_ref, dt_ref):
    o_ref[...] = jnp.dot(x_ref[...], w_ref[...], preferred_element_type=F32)

    @pl.when(pl.program_id(1) == 0)
    def _():
        dt_ref[...] = jnp.dot(x_ref[...], wdt_ref[...], preferred_element_type=F32)


def _proj(hn, w, bn, name, w_dt=None, bm=1024):
    m, k = hn.shape
    n = w.shape[1]
    in_specs = [pl.BlockSpec((bm, k), lambda i, j: (i, 0)),
                pl.BlockSpec((k, bn), lambda i, j: (0, j))]
    out_specs = [pl.BlockSpec((bm, bn), lambda i, j: (i, j))]
    out_shape = [jax.ShapeDtypeStruct((m, n), F32)]
    args = [hn, w]
    if w_dt is not None:
        in_specs.append(pl.BlockSpec((k, LANES), lambda i, j: (0, 0)))
        out_specs.append(pl.BlockSpec((bm, LANES), lambda i, j: (i, 0)))
        out_shape.append(jax.ShapeDtypeStruct((m, LANES), F32))
        args.append(w_dt)
    return pl.pallas_call(
        _proj_kernel if w_dt is None else _proj_dt_kernel,
        grid=(m // bm, n // bn),
        in_specs=in_specs,
        out_specs=out_specs,
        out_shape=out_shape,
        compiler_params=_params(("arbitrary", "arbitrary")),
        name=name,
    )(*args)


def _rope_tile(x, cos, sin_signed, first_half):
    rot = jnp.where(first_half, pltpu.roll(x, LANES - HEAD_DIM // 2, 1),
                    pltpu.roll(x, HEAD_DIM // 2, 1))
    return x * cos + rot * sin_signed


LOG2E = 1.4426950408889634
VT_ROWS = 80


def _swa_kernel(pos_ref, freq_ref, sink_ref, q_ref, k_ref, v_ref, o_ref, kband_ref, vt_ref):
    n = pl.program_id(0)
    w = WINDOW

    lane = lax.broadcasted_iota(jnp.int32, (1, LANES), 1)
    low_half = lane < HEAD_DIM
    first_half = (lane % HEAD_DIM) < (HEAD_DIM // 2)

    @pl.when(n == 0)
    def _():
        kband_ref[...] = jnp.zeros_like(kband_ref)
        vt_ref[...] = jnp.zeros_like(vt_ref)
        ones_row = jnp.where(lax.broadcasted_iota(jnp.int32, (VT_ROWS - HEAD_DIM, 2 * w), 0) == 0, 1.0, 0.0)
        for kh in range(N_KV_HEADS):
            vt_ref[kh, HEAD_DIM:VT_ROWS, :] = ones_row.astype(BF16)

    ang = pos_ref[...].astype(F32) * freq_ref[...]
    cos = jnp.cos(ang)
    sin_signed = jnp.where(first_half, -jnp.sin(ang), jnp.sin(ang))

    for kh in range(N_KV_HEADS):
        kband_ref[kh, 0:w, :] = kband_ref[kh, w:2 * w, :]
        vt_ref[kh, 0:HEAD_DIM, 0:w] = vt_ref[kh, 0:HEAD_DIM, w:2 * w]
    for t in range(KV_COLS // LANES):
        k_rot = _rope_tile(k_ref[:, t * LANES:(t + 1) * LANES], cos, sin_signed, first_half)
        k_sw = pltpu.roll(k_rot, HEAD_DIM, 1)
        kband_ref[2 * t, w:2 * w, :] = jnp.where(low_half, k_rot, k_sw).astype(BF16)
        kband_ref[2 * t + 1, w:2 * w, :] = jnp.where(low_half, k_sw, k_rot).astype(BF16)
        v_t = v_ref[:, t * LANES:(t + 1) * LANES].T
        vt_ref[2 * t, 0:HEAD_DIM, w:2 * w] = v_t[0:HEAD_DIM, :].astype(BF16)
        vt_ref[2 * t + 1, 0:HEAD_DIM, w:2 * w] = v_t[HEAD_DIM:2 * HEAD_DIM, :].astype(BF16)

    ki = lax.broadcasted_iota(jnp.int32, (2 * w, w), 0)
    qi = lax.broadcasted_iota(jnp.int32, (2 * w, w), 1)
    prev_off = jnp.where(n > 0, 0, 2 * w)
    valid = ((ki < w) & (ki > qi + prev_off)) | ((ki >= w) & (ki - w <= qi))
    qscale = HEAD_DIM ** -0.5 * LOG2E
    pairs = Q_PER_KV // 2
    eye = (lax.broadcasted_iota(jnp.int32, (LANES, LANES), 0)
           == lax.broadcasted_iota(jnp.int32, (LANES, LANES), 1)).astype(BF16)

    for kh in range(N_KV_HEADS):
        lhs = []
        for j in range(pairs):
            t = kh * pairs + j
            q_rot = _rope_tile(q_ref[:, t * LANES:(t + 1) * LANES], cos, sin_signed, first_half) * qscale
            lhs.append(jnp.where(low_half, q_rot, 0.0).astype(BF16))
            lhs.append(jnp.where(low_half, 0.0, q_rot).astype(BF16))
        st = lax.dot_general(kband_ref[kh], jnp.concatenate(lhs, axis=0), (((1,), (1,)), ((), ())),
                             preferred_element_type=F32)
        probs = []
        sink_terms = []
        for hq in range(Q_PER_KV):
            s = jnp.where(valid, st[:, hq * w:(hq + 1) * w], -jnp.inf)
            sink = sink_ref[kh * Q_PER_KV + hq] * LOG2E
            m = jnp.maximum(jnp.max(s, axis=0, keepdims=True), sink)
            probs.append(jnp.exp2(s - m).astype(BF16))
            sink_terms.append(jnp.exp2(sink - m))
        ot = jnp.dot(vt_ref[kh], jnp.concatenate(probs, axis=1), preferred_element_type=F32)
        for j in range(pairs):
            halves = []
            for r in range(2):
                hq = 2 * j + r
                cols = slice(hq * w, (hq + 1) * w)
                denom = ot[HEAD_DIM:HEAD_DIM + 1, cols] + sink_terms[hq]
                halves.append(ot[0:HEAD_DIM, cols] / denom)
            pair_t = jnp.concatenate(halves, axis=0).astype(BF16)
            out = lax.dot_general(eye, pair_t, (((1,), (1,)), ((), ())), preferred_element_type=F32)
            t = kh * pairs + j
            o_ref[:, t * LANES:(t + 1) * LANES] = out.astype(o_ref.dtype)


def _swa(qkv, pos, freq, sinks):
    s = qkv.shape[0]
    w = WINDOW
    return pl.pallas_call(
        _swa_kernel,
        grid=(s // w,),
        in_specs=[pl.BlockSpec((w, 1), lambda n: (n, 0)),
                  pl.BlockSpec((1, LANES), lambda n: (0, 0)),
                  pl.BlockSpec(memory_space=pltpu.SMEM),
                  pl.BlockSpec((w, D_ATTN), lambda n: (n, 0)),
                  pl.BlockSpec((w, KV_COLS), lambda n: (n, D_ATTN // KV_COLS)),
                  pl.BlockSpec((w, KV_COLS), lambda n: (n, D_ATTN // KV_COLS + 1))],
        out_specs=pl.BlockSpec((w, D_ATTN), lambda n: (n, 0)),
        out_shape=jax.ShapeDtypeStruct((s, D_ATTN), BF16),
        scratch_shapes=[pltpu.VMEM((N_KV_HEADS, 2 * w, LANES), BF16),
                        pltpu.VMEM((N_KV_HEADS, VT_ROWS, 2 * w), BF16)],
        compiler_params=_params(("arbitrary",)),
        name="swa",
    )(pos, freq, sinks, qkv, qkv, qkv)


def _split3(v):
    hi = v.astype(BF16)
    r = v - hi.astype(F32)
    mid = r.astype(BF16)
    lo = (r - mid.astype(F32)).astype(BF16)
    return hi, mid, lo


def _dot_exact_rhs(a, b_parts):
    acc = jnp.dot(a, b_parts[0], preferred_element_type=F32)
    for part in b_parts[1:]:
        acc = acc + jnp.dot(a, part, preferred_element_type=F32)
    return acc


def _dot_exact_lhs(a_parts, b):
    acc = jnp.dot(a_parts[0], b, preferred_element_type=F32)
    for part in a_parts[1:]:
        acc = acc + jnp.dot(part, b, preferred_element_type=F32)
    return acc


def _ssd_kernel(xbc_ref, z_ref, dt_ref, cw_ref, cb_ref, dtb_ref, alog_ref, dexp_ref,
                ng_ref, tri_ref, expand_ref, o_ref, xpad_ref, xc_ref, y_ref, state_ref):
    c = pl.program_id(0)
    L = CHUNK
    G, R, P, N = SSM_GROUPS, HEADS_PER_GROUP, HEAD_DIM, SSM_STATE
    GP = R * P

    @pl.when(c == 0)
    def _():
        state_ref[...] = jnp.zeros_like(state_ref)
        xpad_ref[:, L:L + SUBLANES, :] = jnp.zeros((XBC_COLS // LANES, SUBLANES, LANES), F32)

    xpad_ref[:, 0:SUBLANES, :] = xpad_ref[:, L:L + SUBLANES, :]

    for p in range(XBC_COLS // LANES):
        cs = slice(p * LANES, (p + 1) * LANES)
        xpad_ref[p, SUBLANES:SUBLANES + L, :] = xbc_ref[:, cs]
        acc = xpad_ref[p, SUBLANES - 3:SUBLANES - 3 + L, :] * cw_ref[0:1, cs]
        for k in range(1, SSM_CONV):
            acc = acc + xpad_ref[p, SUBLANES - 3 + k:SUBLANES - 3 + k + L, :] * cw_ref[k:k + 1, cs]
        acc = acc + cb_ref[:, cs]
        xc_ref[:, cs] = acc * jax.nn.sigmoid(acc)

    dt_raw = dt_ref[...] + dtb_ref[...]
    dt = jnp.maximum(dt_raw, 0.0) + jnp.log1p(jnp.exp(-jnp.abs(dt_raw)))
    a = -jnp.exp(alog_ref[...])
    da = dt * a
    a_cs = _dot_exact_rhs(tri_ref[...], _split3(da))
    a_cs_t = a_cs.T
    expand = expand_ref[...]
    acs_x = _dot_exact_lhs(_split3(a_cs), expand)
    dt_x = _dot_exact_lhs(_split3(dt), expand)

    li = lax.broadcasted_iota(jnp.int32, (L, L), 0)
    si = lax.broadcasted_iota(jnp.int32, (L, L), 1)
    causal = li >= si
    lane = lax.broadcasted_iota(jnp.int32, (1, LANES), 1)
    low_half = lane < P

    for g in range(G):
        xs = slice(g * GP, (g + 1) * GP)
        x_g = xc_ref[:, xs]
        b_g = xc_ref[:, D_SSM + g * N:D_SSM + (g + 1) * N].astype(BF16)
        c_g = xc_ref[:, D_SSM + G * N + g * N:D_SSM + G * N + (g + 1) * N].astype(BF16)
        acs_g = acs_x[:, xs]
        xdt_g = x_g * dt_x[:, xs]
        cb = lax.dot_general(c_g, b_g, (((1,), (1,)), ((), ())), preferred_element_type=F32)

        y_pairs = []
        for j in range(R // 2):
            ws = []
            for r in range(2):
                h = g * R + 2 * j + r
                seg = a_cs[:, h:h + 1] - a_cs_t[h:h + 1, :]
                decay = jnp.exp(jnp.where(causal, seg, -jnp.inf))
                ws.append((cb * decay).astype(BF16))
            wcat = jnp.concatenate(ws, axis=1)
            xp = xdt_g[:, j * LANES:(j + 1) * LANES]
            rhs = jnp.concatenate([jnp.where(low_half, xp, 0.0), jnp.where(low_half, 0.0, xp)],
                                  axis=0).astype(BF16)
            y_pairs.append(jnp.dot(wcat, rhs, preferred_element_type=F32))
        y_diag = jnp.concatenate(y_pairs, axis=1)

        a_last = acs_g[L - 1:L, :]
        xdte = (xdt_g * jnp.exp(a_last - acs_g)).astype(BF16)
        new_state = lax.dot_general(b_g, xdte, (((0,), (0,)), ((), ())),
                                    preferred_element_type=F32)
        prev = state_ref[g]
        y_off = jnp.dot(c_g, prev.astype(BF16), preferred_element_type=F32) * jnp.exp(acs_g)
        state_ref[g] = prev * jnp.exp(a_last) + new_state
        y_ref[:, xs] = y_diag + y_off + x_g * dexp_ref[:, xs]

    z = z_ref[...]
    y = y_ref[...] * (z * jax.nn.sigmoid(z))
    ms = jnp.mean(y * y, axis=-1, keepdims=True)
    o_ref[...] = (y * lax.rsqrt(ms + EPS) * ng_ref[...]).astype(o_ref.dtype)


def _ssd(xbc, z, dt, conv_w, conv_b, dt_bias, a_log, d_skip, norm_g):
    s = xbc.shape[0]
    L = CHUNK
    pad = LANES - N_SSM_HEADS
    dtb = jnp.pad(dt_bias, (0, pad)).reshape(1, LANES)
    alog = jnp.pad(a_log, (0, pad)).reshape(1, LANES)
    dexp = jnp.repeat(d_skip, HEAD_DIM).reshape(1, D_SSM)
    tri = (jnp.arange(L)[:, None] >= jnp.arange(L)[None, :]).astype(BF16)
    expand = (jnp.arange(LANES)[:, None] == (jnp.arange(D_SSM)[None, :] // HEAD_DIM)).astype(BF16)
    const = lambda c: (0, 0)
    return pl.pallas_call(
        _ssd_kernel,
        grid=(s // L,),
        in_specs=[pl.BlockSpec((L, XBC_COLS), lambda c: (c, 0)),
                  pl.BlockSpec((L, D_SSM), lambda c: (c, 0)),
                  pl.BlockSpec((L, LANES), lambda c: (c, 0)),
                  pl.BlockSpec((SSM_CONV, XBC_COLS), const),
                  pl.BlockSpec((1, XBC_COLS), const),
                  pl.BlockSpec((1, LANES), const),
                  pl.BlockSpec((1, LANES), const),
                  pl.BlockSpec((1, D_SSM), const),
                  pl.BlockSpec((1, D_SSM), const),
                  pl.BlockSpec((L, L), const),
                  pl.BlockSpec((LANES, D_SSM), const)],
        out_specs=pl.BlockSpec((L, D_SSM), lambda c: (c, 0)),
        out_shape=jax.ShapeDtypeStruct((s, D_SSM), BF16),
        scratch_shapes=[pltpu.VMEM((XBC_COLS // LANES, SUBLANES + L, LANES), F32),
                        pltpu.VMEM((L, XBC_COLS), F32),
                        pltpu.VMEM((L, D_SSM), F32),
                        pltpu.VMEM((SSM_GROUPS, SSM_STATE, HEADS_PER_GROUP * HEAD_DIM), F32)],
        compiler_params=_params(("arbitrary",)),
        name="ssd",
    )(xbc, z, dt, conv_w, conv_b.reshape(1, XBC_COLS), dtb, alog, dexp,
      norm_g.reshape(1, D_SSM), tri, expand)


def _out_proj_kernel(ya_ref, ys_ref, w_ref, res_ref, g_ref, o_ref, hg_ref, r_ref, ss_ref):
    j = pl.program_id(1)
    bn = w_ref.shape[1]

    @pl.when(j == 0)
    def _():
        ss_ref[...] = jnp.zeros_like(ss_ref)

    acc = jnp.dot(ya_ref[...], w_ref[0:D_ATTN, :], preferred_element_type=F32)
    acc = acc + jnp.dot(ys_ref[...], w_ref[D_ATTN:D_ATTN + D_SSM, :], preferred_element_type=F32)
    h = res_ref[...] + acc
    o_ref[...] = h
    hg_ref[...] = (h * g_ref[...]).astype(hg_ref.dtype)
    sq = h * h
    part = sq[:, 0:LANES]
    for c in range(1, bn // LANES):
        part = part + sq[:, c * LANES:(c + 1) * LANES]
    ss_ref[...] += part

    @pl.when(j == pl.num_programs(1) - 1)
    def _():
        ms = jnp.sum(ss_ref[...], axis=-1, keepdims=True) * (1.0 / D_MODEL)
        r_ref[...] = lax.rsqrt(ms + EPS)


def _out_proj(ya, ys, w, res, g, bm=1024, bn=1024):
    m = ya.shape[0]
    k, n = w.shape
    return pl.pallas_call(
        _out_proj_kernel,
        grid=(m // bm, n // bn),
        in_specs=[pl.BlockSpec((bm, D_ATTN), lambda i, j: (i, 0)),
                  pl.BlockSpec((bm, D_SSM), lambda i, j: (i, 0)),
                  pl.BlockSpec((k, bn), lambda i, j: (0, j)),
                  pl.BlockSpec((bm, bn), lambda i, j: (i, j)),
                  pl.BlockSpec((1, bn), lambda i, j: (0, j))],
        out_specs=[pl.BlockSpec((bm, bn), lambda i, j: (i, j)),
                   pl.BlockSpec((bm, bn), lambda i, j: (i, j)),
                   pl.BlockSpec((bm, 1), lambda i, j: (i, 0))],
        out_shape=[jax.ShapeDtypeStruct((m, n), F32),
                   jax.ShapeDtypeStruct((m, n), BF16),
                   jax.ShapeDtypeStruct((m, 1), F32)],
        scratch_shapes=[pltpu.VMEM((bm, LANES), F32)],
        compiler_params=_params(("arbitrary", "arbitrary"), vmem=VMEM_LIMIT_HIGH),
        name="out_proj",
    )(ya, ys, w, res, g.reshape(1, n))


FFN_SUB_BLOCKS = 8


def _as_zero(v, zero_ref):
    return lax.bitcast_convert_type(jnp.max(v), jnp.int32) & zero_ref[0]


def _ffn_up_kernel(zero_ref, x_ref, r_ref, wg_ref, wu_ref, cw_ref, cb_ref, o_ref, ubuf_ref, wbf_ref):
    t = pl.program_id(0)
    n_steps = pl.num_programs(0)
    n_row_blocks = (n_steps - 1) // N_FFN_TILES
    bm = x_ref.shape[0]
    hb = bm // FFN_SUB_BLOCKS
    ft = FFN_TILE
    planes_per_half = ft // LANES
    row_block_new = jnp.minimum(t, n_steps - 2) % n_row_blocks

    @pl.when(t == 0)
    def _():
        ubuf_ref[...] = jnp.zeros_like(ubuf_ref)

    @pl.when((row_block_new == 0) & (t < n_steps - 1))
    def _():
        wbf_ref[0] = wg_ref[...].astype(BF16)
        wbf_ref[1] = wu_ref[...].astype(BF16)

    f_done = jnp.maximum(t - 1, 0) // n_row_blocks
    for sb in range(1, FFN_SUB_BLOCKS):
        ubuf_ref[sb, :, 0:SUBLANES, :] = ubuf_ref[sb - 1, :, hb:hb + SUBLANES, :]
    last_rows = ubuf_ref[FFN_SUB_BLOCKS - 1, :, hb:hb + SUBLANES, :]

    after_matmul = None
    after_tail = None
    for sb in range(FFN_SUB_BLOCKS):
        top = SUBLANES - 2 if after_matmul is None else SUBLANES - 2 + after_matmul
        folded = None
        for q in range(planes_per_half):
            conv = []
            for half in range(2):
                plane = half * planes_per_half + q
                col = pl.multiple_of(half * D_FF + f_done * ft + q * LANES, LANES)
                acc = ubuf_ref[sb, plane, pl.ds(top, hb), :] * cw_ref[0:1, pl.ds(col, LANES)]
                acc = acc + ubuf_ref[sb, plane, pl.ds(top + 1, hb), :] * cw_ref[1:2, pl.ds(col, LANES)]
                acc = acc + ubuf_ref[sb, plane, pl.ds(top + 2, hb), :] * cw_ref[2:3, pl.ds(col, LANES)]
                conv.append(acc + cb_ref[:, pl.ds(col, LANES)])
            gate, up = conv
            act = gate * jax.nn.sigmoid(gate) * up
            o_ref[sb * hb:(sb + 1) * hb, q * LANES:(q + 1) * LANES] = act.astype(o_ref.dtype)
            part = jnp.max(act.reshape(hb // SUBLANES, SUBLANES, LANES), axis=0)
            folded = part if folded is None else jnp.maximum(folded, part)
        tail_done = _as_zero(folded, zero_ref)
        if sb == 0:
            ubuf_ref[0, :, 0:SUBLANES, :] = jnp.where(row_block_new == 0, jnp.zeros_like(last_rows), last_rows)

        row0 = sb * hb if after_tail is None else pl.multiple_of(sb * hb + after_tail, hb)
        x = x_ref[pl.ds(row0, hb), :]
        r = r_ref[sb * hb:(sb + 1) * hb, :]
        first = None
        for half in range(2):
            u = jnp.dot(x, wbf_ref[half], preferred_element_type=F32) * r
            if first is None:
                first = u[0:SUBLANES, 0:LANES]
            for q in range(planes_per_half):
                ubuf_ref[sb, half * planes_per_half + q, SUBLANES:SUBLANES + hb, :] = u[:, q * LANES:(q + 1) * LANES]
        after_matmul = _as_zero(first, zero_ref)
        after_tail = tail_done


def _ffn_up(hg, r, w, cw, cb, bm=1024):
    m, k = hg.shape
    nf = N_FFN_TILES
    nrb = m // bm
    n_tiles = nrb * nf
    n_planes = 2 * FFN_TILE // LANES
    cur = lambda t: jnp.minimum(t, n_tiles - 1)
    done = lambda t: jnp.maximum(t - 1, 0)
    return pl.pallas_call(
        _ffn_up_kernel,
        grid=(n_tiles + 1,),
        in_specs=[pl.BlockSpec(memory_space=pltpu.SMEM),
                  pl.BlockSpec((bm, k), lambda t: (cur(t) % nrb, 0)),
                  pl.BlockSpec((bm, 1), lambda t: (cur(t) % nrb, 0)),
                  pl.BlockSpec((k, FFN_TILE), lambda t: (0, cur(t) // nrb)),
                  pl.BlockSpec((k, FFN_TILE), lambda t: (0, nf + cur(t) // nrb)),
                  pl.BlockSpec((FFN_CONV, 2 * D_FF), lambda t: (0, 0)),
                  pl.BlockSpec((1, 2 * D_FF), lambda t: (0, 0))],
        out_specs=pl.BlockSpec((bm, FFN_TILE), lambda t: (done(t) % nrb, done(t) // nrb)),
        out_shape=jax.ShapeDtypeStruct((m, D_FF), BF16),
        scratch_shapes=[pltpu.VMEM((FFN_SUB_BLOCKS, n_planes, SUBLANES + bm // FFN_SUB_BLOCKS, LANES), F32),
                        pltpu.VMEM((2, k, FFN_TILE), BF16)],
        compiler_params=_params(("arbitrary",)),
        name="ffn_up",
    )(jnp.zeros((1,), jnp.int32), hg, r, w, w, cw, cb)


def _ffn_down_kernel(a_ref, w_ref, res_ref, o_ref, obf_ref):
    h = res_ref[...] + jnp.dot(a_ref[...], w_ref[...], preferred_element_type=F32)
    o_ref[...] = h
    obf_ref[...] = h.astype(obf_ref.dtype)


def _ffn_down(act, w, res, bm=512, bn=1024):
    m, k = act.shape
    n = w.shape[1]
    return pl.pallas_call(
        _ffn_down_kernel,
        grid=(n // bn, m // bm),
        in_specs=[pl.BlockSpec((bm, k), lambda j, i: (i, 0)),
                  pl.BlockSpec((k, bn), lambda j, i: (0, j), pipeline_mode=pl.Buffered(1)),
                  pl.BlockSpec((bm, bn), lambda j, i: (i, j))],
        out_specs=[pl.BlockSpec((bm, bn), lambda j, i: (i, j)),
                   pl.BlockSpec((bm, bn), lambda j, i: (i, j))],
        out_shape=[jax.ShapeDtypeStruct((m, n), F32), jax.ShapeDtypeStruct((m, n), BF16)],
        compiler_params=_params(("arbitrary", "arbitrary"), vmem=VMEM_LIMIT_HIGH),
        name="ffn_down",
    )(act, w, res)


def _ple_kernel(hb_ref, wg_ref, bg_ref, p_ref, wp_ref, h_ref, gf_ref, o_ref, ss_ref):
    j = pl.program_id(1)
    nj = pl.num_programs(1)
    bn = wg_ref.shape[1]

    @pl.when(j == 0)
    def _():
        ss_ref[...] = jnp.zeros_like(ss_ref)

    gate = jax.nn.sigmoid(jnp.dot(hb_ref[...], wg_ref[...], preferred_element_type=F32) + bg_ref[...])
    emb = jnp.dot(p_ref[...].astype(BF16), wp_ref[...], preferred_element_type=F32)
    h = h_ref[...] + gate * emb
    sq = h * h
    part = sq[:, 0:LANES]
    for c in range(1, bn // LANES):
        part = part + sq[:, c * LANES:(c + 1) * LANES]
    ss_ref[...] += part
    col = pl.multiple_of(j * bn, bn)
    o_ref[:, pl.ds(col, bn)] = h

    @pl.when(j == nj - 1)
    def _():
        ms = jnp.sum(ss_ref[...], axis=-1, keepdims=True) * (1.0 / D_MODEL)
        o_ref[...] = o_ref[...] * lax.rsqrt(ms + EPS) * gf_ref[...]


def _ple(hb, wg, bg, p, wp, h, gf, bm=512, bn=1024):
    m, d = h.shape
    return pl.pallas_call(
        _ple_kernel,
        grid=(m // bm, d // bn),
        in_specs=[pl.BlockSpec((bm, d), lambda i, j: (i, 0)),
                  pl.BlockSpec((d, bn), lambda i, j: (0, j)),
                  pl.BlockSpec((1, bn), lambda i, j: (0, j)),
                  pl.BlockSpec((bm, PLE_DIM), lambda i, j: (i, 0)),
                  pl.BlockSpec((PLE_DIM, bn), lambda i, j: (0, j)),
                  pl.BlockSpec((bm, bn), lambda i, j: (i, j)),
                  pl.BlockSpec((1, d), lambda i, j: (0, 0))],
        out_specs=pl.BlockSpec((bm, d), lambda i, j: (i, 0)),
        out_shape=jax.ShapeDtypeStruct((m, d), F32),
        scratch_shapes=[pltpu.VMEM((bm, LANES), F32)],
        compiler_params=_params(("arbitrary", "arbitrary"), vmem=VMEM_LIMIT_HIGH),
        name="ple",
    )(hb, wg, bg, p, wp, h, gf)


def kernel(x, p, positions, norm_mix_g, w_in, attn_sinks, ssm_conv_w, ssm_conv_b, ssm_dt_bias,
           ssm_a_log, ssm_d, ssm_norm_g, w_o, norm_ffn_g, w_up, ffn_conv_w, ffn_conv_b, w_down,
           ple_gate_w, ple_gate_b, ple_proj, norm_final_g):
    b, s, d = x.shape
    assert (b, s, d) == (1, SEQ, D_MODEL) and w_in.shape[0] == 1
    h = x.reshape(s, d)
    pos = positions.reshape(s, 1)
    half = HEAD_DIM // 2
    inv_freq = ROPE_THETA ** (-jnp.arange(half, dtype=F32) / half)
    freq = jnp.tile(inv_freq, LANES // half).reshape(1, LANES)
    w_dt = lax.slice(w_in, (0, 0, DT_COL0), (1, d, DT_COL0 + N_SSM_HEADS)).reshape(d, N_SSM_HEADS)
    w_dt = jnp.pad(w_dt, ((0, 0), (0, LANES - N_SSM_HEADS)))

    hn = _norm_cast(h, norm_mix_g[0])
    w_qkv, w_z, w_xbc = _split_w_in(jnp.swapaxes(w_in, 1, 2))
    qkv, dt = _proj(hn, w_qkv, QKV_TILE, "in_proj_qkv", w_dt=w_dt.astype(BF16))
    z, = _proj(hn, w_z, SSM_TILE, "in_proj_z")
    xbc, = _proj(hn, w_xbc, SSM_TILE, "in_proj_xbc")
    y_attn = _swa(qkv, pos, freq, attn_sinks[0])
    y_ssm = _ssd(xbc, z, dt, ssm_conv_w[0], ssm_conv_b[0], ssm_dt_bias[0], ssm_a_log[0], ssm_d[0],
                 ssm_norm_g[0])
    h1, hg, r = _out_proj(y_attn, y_ssm, w_o[0].astype(BF16), h, norm_ffn_g[0])
    act = _ffn_up(hg, r, w_up[0], ffn_conv_w[0], ffn_conv_b[0].reshape(1, 2 * D_FF))
    h2, h2b = _ffn_down(act, w_down[0].astype(BF16), h1)
    out = _ple(h2b, ple_gate_w[0].astype(BF16), ple_gate_b[0].reshape(1, d), p[0].reshape(s, PLE_DIM),
               ple_proj[0].astype(BF16), h2, norm_final_g.reshape(1, d))
    return out.reshape(b, s, d)
```

```python
import jax
import jax.numpy as jnp
from jax import lax
from jax.experimental import pallas as pl
from jax.experimental.pallas import tpu as pltpu

F32 = jnp.float32
BF16 = jnp.bfloat16

D_MODEL = 4096
SEQ = 16384
PLE_DIM = 256
D_ATTN = 2048
D_SSM = 2048
HEAD_DIM = 64
N_Q_HEADS = 32
N_KV_HEADS = 4
Q_PER_KV = 8
WINDOW = 128
ROPE_THETA = 10000.0
N_SSM_HEADS = 32
SSM_STATE = 128
SSM_GROUPS = 8
HEADS_PER_GROUP = 4
SSM_CONV = 4
CHUNK = 128
D_FF = 11008
FFN_CONV = 3
EPS = 1e-6
KV_COLS = N_KV_HEADS * HEAD_DIM
QKV_COLS = D_ATTN + 2 * KV_COLS
XBC_COLS = D_SSM + 2 * SSM_GROUPS * SSM_STATE
DT_COL0 = QKV_COLS + D_SSM + XBC_COLS

LANES = 128
SUBLANES = 8
VMEM_LIMIT = 56 * 1024 * 1024
VMEM_LIMIT_HIGH = 62 * 1024 * 1024

QKV_TILE = QKV_COLS // 2
SSM_TILE = 1024
FFN_TILE = 256
N_FFN_TILES = D_FF // FFN_TILE


def _params(sem, vmem=VMEM_LIMIT, flags=None):
    return pltpu.CompilerParams(dimension_semantics=sem, vmem_limit_bytes=vmem, flags=flags)


def _norm_cast_kernel(x_ref, g_ref, o_ref):
    x = x_ref[...]
    ms = jnp.mean(x * x, axis=-1, keepdims=True)
    o_ref[...] = (x * lax.rsqrt(ms + EPS) * g_ref[...]).astype(o_ref.dtype)


def _norm_cast(x, g, bm=512):
    m, d = x.shape
    return pl.pallas_call(
        _norm_cast_kernel,
        grid=(m // bm,),
        in_specs=[pl.BlockSpec((bm, d), lambda i: (i, 0)),
                  pl.BlockSpec((1, d), lambda i: (0, 0))],
        out_specs=pl.BlockSpec((bm, d), lambda i: (i, 0)),
        out_shape=jax.ShapeDtypeStruct((m, d), BF16),
        compiler_params=_params(("arbitrary",)),
        name="norm_cast",
    )(x, g.reshape(1, d))


W_IN_CAST_TILE = 512
W_IN_QKV_TILES = QKV_COLS // W_IN_CAST_TILE
W_IN_Z_TILES = D_SSM // W_IN_CAST_TILE
W_IN_XBC_TILES = XBC_COLS // W_IN_CAST_TILE


def _split_w_in_kernel(wt_ref, qkv_ref, z_ref, xbc_ref):
    j = pl.program_id(0)

    @pl.when(j < W_IN_QKV_TILES)
    def _():
        qkv_ref[...] = wt_ref[...].T.astype(qkv_ref.dtype)

    @pl.when((j >= W_IN_QKV_TILES) & (j < W_IN_QKV_TILES + W_IN_Z_TILES))
    def _():
        z_ref[...] = wt_ref[...].T.astype(z_ref.dtype)

    @pl.when(j >= W_IN_QKV_TILES + W_IN_Z_TILES)
    def _():
        xbc_ref[...] = wt_ref[...].T.astype(xbc_ref.dtype)


def _split_w_in(wt):
    k = wt.shape[2]
    bn = W_IN_CAST_TILE
    q, z = W_IN_QKV_TILES, W_IN_Z_TILES
    return pl.pallas_call(
        _split_w_in_kernel,
        grid=(q + z + W_IN_XBC_TILES,),
        in_specs=[pl.BlockSpec((None, bn, k), lambda j: (0, j, 0))],
        out_specs=[pl.BlockSpec((k, bn), lambda j: (0, jnp.minimum(j, q - 1))),
                   pl.BlockSpec((k, bn), lambda j: (0, jnp.clip(j - q, 0, z - 1))),
                   pl.BlockSpec((k, bn), lambda j: (0, jnp.clip(j - q - z, 0, W_IN_XBC_TILES - 1)))],
        out_shape=[jax.ShapeDtypeStruct((k, QKV_COLS), BF16),
                   jax.ShapeDtypeStruct((k, D_SSM), BF16),
                   jax.ShapeDtypeStruct((k, XBC_COLS), BF16)],
        compiler_params=_params(("arbitrary",)),
        name="split_w_in",
    )(wt)


def _proj_kernel(x_ref, w_ref, o_ref):
    o_ref[...] = jnp.dot(x_ref[...], w_ref[...], preferred_element_type=F32)


def _proj_dt_kernel(x_ref, w_ref, wdt_ref, o_ref, dt_ref):
    o_ref[...] = jnp.dot(x_ref[...], w_ref[...], preferred_element_type=F32)

    @pl.when(pl.program_id(1) == 0)
    def _():
        dt_ref[...] = jnp.dot(x_ref[...], wdt_ref[...], preferred_element_type=F32)


def _proj(hn, w, bn, name, w_dt=None, bm=1024):
    m, k = hn.shape
    n = w.shape[1]
    in_specs = [pl.BlockSpec((bm, k), lambda i, j: (i, 0)),
                pl.BlockSpec((k, bn), lambda i, j: (0, j))]
    out_specs = [pl.BlockSpec((bm, bn), lambda i, j: (i, j))]
    out_shape = [jax.ShapeDtypeStruct((m, n), F32)]
    args = [hn, w]
    if w_dt is not None:
        in_specs.append(pl.BlockSpec((k, LANES), lambda i, j: (0, 0)))
        out_specs.append(pl.BlockSpec((bm, LANES), lambda i, j: (i, 0)))
        out_shape.append(jax.ShapeDtypeStruct((m, LANES), F32))
        args.append(w_dt)
    return pl.pallas_call(
        _proj_kernel if w_dt is None else _proj_dt_kernel,
        grid=(m // bm, n // bn),
        in_specs=in_specs,
        out_specs=out_specs,
        out_shape=out_shape,
        compiler_params=_params(("arbitrary", "arbitrary")),
        name=name,
    )(*args)


def _rope_tile(x, cos, sin_signed, first_half):
    rot = jnp.where(first_half, pltpu.roll(x, LANES - HEAD_DIM // 2, 1),
                    pltpu.roll(x, HEAD_DIM // 2, 1))
    return x * cos + rot * sin_signed


LOG2E = 1.4426950408889634
VT_ROWS = 80


def _swa_kernel(pos_ref, freq_ref, sink_ref, q_ref, k_ref, v_ref, o_ref, kband_ref, vt_ref):
    n = pl.program_id(0)
    w = WINDOW

    lane = lax.broadcasted_iota(jnp.int32, (1, LANES), 1)
    low_half = lane < HEAD_DIM
    first_half = (lane % HEAD_DIM) < (HEAD_DIM // 2)

    @pl.when(n == 0)
    def _():
        kband_ref[...] = jnp.zeros_like(kband_ref)
        vt_ref[...] = jnp.zeros_like(vt_ref)
        ones_row = jnp.where(lax.broadcasted_iota(jnp.int32, (VT_ROWS - HEAD_DIM, 2 * w), 0) == 0, 1.0, 0.0)
        for kh in range(N_KV_HEADS):
            vt_ref[kh, HEAD_DIM:VT_ROWS, :] = ones_row.astype(BF16)

    ang = pos_ref[...].astype(F32) * freq_ref[...]
    cos = jnp.cos(ang)
    sin_signed = jnp.where(first_half, -jnp.sin(ang), jnp.sin(ang))

    for kh in range(N_KV_HEADS):
        kband_ref[kh, 0:w, :] = kband_ref[kh, w:2 * w, :]
        vt_ref[kh, 0:HEAD_DIM, 0:w] = vt_ref[kh, 0:HEAD_DIM, w:2 * w]
    for t in range(KV_COLS // LANES):
        k_rot = _rope_tile(k_ref[:, t * LANES:(t + 1) * LANES], cos, sin_signed, first_half)
        k_sw = pltpu.roll(k_rot, HEAD_DIM, 1)
        kband_ref[2 * t, w:2 * w, :] = jnp.where(low_half, k_rot, k_sw).astype(BF16)
        kband_ref[2 * t + 1, w:2 * w, :] = jnp.where(low_half, k_sw, k_rot).astype(BF16)
        v_t = v_ref[:, t * LANES:(t + 1) * LANES].T
        vt_ref[2 * t, 0:HEAD_DIM, w:2 * w] = v_t[0:HEAD_DIM, :].astype(BF16)
        vt_ref[2 * t + 1, 0:HEAD_DIM, w:2 * w] = v_t[HEAD_DIM:2 * HEAD_DIM, :].astype(BF16)

    ki = lax.broadcasted_iota(jnp.int32, (2 * w, w), 0)
    qi = lax.broadcasted_iota(jnp.int32, (2 * w, w), 1)
    prev_off = jnp.where(n > 0, 0, 2 * w)
    valid = ((ki < w) & (ki > qi + prev_off)) | ((ki >= w) & (ki - w <= qi))
    qscale = HEAD_DIM ** -0.5 * LOG2E
    pairs = Q_PER_KV // 2
    eye = (lax.broadcasted_iota(jnp.int32, (LANES, LANES), 0)
           == lax.broadcasted_iota(jnp.int32, (LANES, LANES), 1)).astype(BF16)

    for kh in range(N_KV_HEADS):
        lhs = []
        for j in range(pairs):
            t = kh * pairs + j
            q_rot = _rope_tile(q_ref[:, t * LANES:(t + 1) * LANES], cos, sin_signed, first_half) * qscale
            lhs.append(jnp.where(low_half, q_rot, 0.0).astype(BF16))
            lhs.append(jnp.where(low_half, 0.0, q_rot).astype(BF16))
        st = lax.dot_general(kband_ref[kh], jnp.concatenate(lhs, axis=0), (((1,), (1,)), ((), ())),
                             preferred_element_type=F32)
        probs = []
        sink_terms = []
        for hq in range(Q_PER_KV):
            s = jnp.where(valid, st[:, hq * w:(hq + 1) * w], -jnp.inf)
            sink = sink_ref[kh * Q_PER_KV + hq] * LOG2E
            m = jnp.maximum(jnp.max(s, axis=0, keepdims=True), sink)
            probs.append(jnp.exp2(s - m).astype(BF16))
            sink_terms.append(jnp.exp2(sink - m))
        ot = jnp.dot(vt_ref[kh], jnp.concatenate(probs, axis=1), preferred_element_type=F32)
        for j in range(pairs):
            halves = []
            for r in range(2):
                hq = 2 * j + r
                cols = slice(hq * w, (hq + 1) * w)
                denom = ot[HEAD_DIM:HEAD_DIM + 1, cols] + sink_terms[hq]
                halves.append(ot[0:HEAD_DIM, cols] / denom)
            pair_t = jnp.concatenate(halves, axis=0).astype(BF16)
            out = lax.dot_general(eye, pair_t, (((1,), (1,)), ((), ())), preferred_element_type=F32)
            t = kh * pairs + j
            o_ref[:, t * LANES:(t + 1) * LANES] = out.astype(o_ref.dtype)


def _swa(qkv, pos, freq, sinks):
    s = qkv.shape[0]
    w = WINDOW
    return pl.pallas_call(
        _swa_kernel,
        grid=(s // w,),
        in_specs=[pl.BlockSpec((w, 1), lambda n: (n, 0)),
                  pl.BlockSpec((1, LANES), lambda n: (0, 0)),
                  pl.BlockSpec(memory_space=pltpu.SMEM),
                  pl.BlockSpec((w, D_ATTN), lambda n: (n, 0)),
                  pl.BlockSpec((w, KV_COLS), lambda n: (n, D_ATTN // KV_COLS)),
                  pl.BlockSpec((w, KV_COLS), lambda n: (n, D_ATTN // KV_COLS + 1))],
        out_specs=pl.BlockSpec((w, D_ATTN), lambda n: (n, 0)),
        out_shape=jax.ShapeDtypeStruct((s, D_ATTN), BF16),
        scratch_shapes=[pltpu.VMEM((N_KV_HEADS, 2 * w, LANES), BF16),
                        pltpu.VMEM((N_KV_HEADS, VT_ROWS, 2 * w), BF16)],
        compiler_params=_params(("arbitrary",)),
        name="swa",
    )(pos, freq, sinks, qkv, qkv, qkv)


def _split3(v):
    hi = v.astype(BF16)
    r = v - hi.astype(F32)
    mid = r.astype(BF16)
    lo = (r - mid.astype(F32)).astype(BF16)
    return hi, mid, lo


def _dot_exact_rhs(a, b_parts):
    acc = jnp.dot(a, b_parts[0], preferred_element_type=F32)
    for part in b_parts[1:]:
        acc = acc + jnp.dot(a, part, preferred_element_type=F32)
    return acc


def _dot_exact_lhs(a_parts, b):
    acc = jnp.dot(a_parts[0], b, preferred_element_type=F32)
    for part in a_parts[1:]:
        acc = acc + jnp.dot(part, b, preferred_element_type=F32)
    return acc


def _ssd_kernel(xbc_ref, z_ref, dt_ref, cw_ref, cb_ref, dtb_ref, alog_ref, dexp_ref,
                ng_ref, tri_ref, expand_ref, o_ref, xpad_ref, xc_ref, y_ref, state_ref):
    c = pl.program_id(0)
    L = CHUNK
    G, R, P, N = SSM_GROUPS, HEADS_PER_GROUP, HEAD_DIM, SSM_STATE
    GP = R * P

    @pl.when(c == 0)
    def _():
        state_ref[...] = jnp.zeros_like(state_ref)
        xpad_ref[:, L:L + SUBLANES, :] = jnp.zeros((XBC_COLS // LANES, SUBLANES, LANES), F32)

    xpad_ref[:, 0:SUBLANES, :] = xpad_ref[:, L:L + SUBLANES, :]

    for p in range(XBC_COLS // LANES):
        cs = slice(p * LANES, (p + 1) * LANES)
        xpad_ref[p, SUBLANES:SUBLANES + L, :] = xbc_ref[:, cs]
        acc = xpad_ref[p, SUBLANES - 3:SUBLANES - 3 + L, :] * cw_ref[0:1, cs]
        for k in range(1, SSM_CONV):
            acc = acc + xpad_ref[p, SUBLANES - 3 + k:SUBLANES - 3 + k + L, :] * cw_ref[k:k + 1, cs]
        acc = acc + cb_ref[:, cs]
        xc_ref[:, cs] = acc * jax.nn.sigmoid(acc)

    dt_raw = dt_ref[...] + dtb_ref[...]
    dt = jnp.maximum(dt_raw, 0.0) + jnp.log1p(jnp.exp(-jnp.abs(dt_raw)))
    a = -jnp.exp(alog_ref[...])
    da = dt * a
    a_cs = _dot_exact_rhs(tri_ref[...], _split3(da))
    a_cs_t = a_cs.T
    expand = expand_ref[...]
    acs_x = _dot_exact_lhs(_split3(a_cs), expand)
    dt_x = _dot_exact_lhs(_split3(dt), expand)

    li = lax.broadcasted_iota(jnp.int32, (L, L), 0)
    si = lax.broadcasted_iota(jnp.int32, (L, L), 1)
    causal = li >= si
    lane = lax.broadcasted_iota(jnp.int32, (1, LANES), 1)
    low_half = lane < P

    for g in range(G):
        xs = slice(g * GP, (g + 1) * GP)
        x_g = xc_ref[:, xs]
        b_g = xc_ref[:, D_SSM + g * N:D_SSM + (g + 1) * N].astype(BF16)
        c_g = xc_ref[:, D_SSM + G * N + g * N:D_SSM + G * N + (g + 1) * N].astype(BF16)
        acs_g = acs_x[:, xs]
        xdt_g = x_g * dt_x[:, xs]
        cb = lax.dot_general(c_g, b_g, (((1,), (1,)), ((), ())), preferred_element_type=F32)

        y_pairs = []
        for j in range(R // 2):
            ws = []
            for r in range(2):
                h = g * R + 2 * j + r
                seg = a_cs[:, h:h + 1] - a_cs_t[h:h + 1, :]
                decay = jnp.exp(jnp.where(causal, seg, -jnp.inf))
                ws.append((cb * decay).astype(BF16))
            wcat = jnp.concatenate(ws, axis=1)
            xp = xdt_g[:, j * LANES:(j + 1) * LANES]
            rhs = jnp.concatenate([jnp.where(low_half, xp, 0.0), jnp.where(low_half, 0.0, xp)],
                                  axis=0).astype(BF16)
            y_pairs.append(jnp.dot(wcat, rhs, preferred_element_type=F32))
        y_diag = jnp.concatenate(y_pairs, axis=1)

        a_last = acs_g[L - 1:L, :]
        xdte = (xdt_g * jnp.exp(a_last - acs_g)).astype(BF16)
        new_state = lax.dot_general(b_g, xdte, (((0,), (0,)), ((), ())),
                                    preferred_element_type=F32)
        prev = state_ref[g]
        y_off = jnp.dot(c_g, prev.astype(BF16), preferred_element_type=F32) * jnp.exp(acs_g)
        state_ref[g] = prev * jnp.exp(a_last) + new_state
        y_ref[:, xs] = y_diag + y_off + x_g * dexp_ref[:, xs]

    z = z_ref[...]
    y = y_ref[...] * (z * jax.nn.sigmoid(z))
    ms = jnp.mean(y * y, axis=-1, keepdims=True)
    o_ref[...] = (y * lax.rsqrt(ms + EPS) * ng_ref[...]).astype(o_ref.dtype)


def _ssd(xbc, z, dt, conv_w, conv_b, dt_bias, a_log, d_skip, norm_g):
    s = xbc.shape[0]
    L = CHUNK
    pad = LANES - N_SSM_HEADS
    dtb = jnp.pad(dt_bias, (0, pad)).reshape(1, LANES)
    alog = jnp.pad(a_log, (0, pad)).reshape(1, LANES)
    dexp = jnp.repeat(d_skip, HEAD_DIM).reshape(1, D_SSM)
    tri = (jnp.arange(L)[:, None] >= jnp.arange(L)[None, :]).astype(BF16)
    expand = (jnp.arange(LANES)[:, None] == (jnp.arange(D_SSM)[None, :] // HEAD_DIM)).astype(BF16)
    const = lambda c: (0, 0)
    return pl.pallas_call(
        _ssd_kernel,
        grid=(s // L,),
        in_specs=[pl.BlockSpec((L, XBC_COLS), lambda c: (c, 0)),
                  pl.BlockSpec((L, D_SSM), lambda c: (c, 0)),
                  pl.BlockSpec((L, LANES), lambda c: (c, 0)),
                  pl.BlockSpec((SSM_CONV, XBC_COLS), const),
                  pl.BlockSpec((1, XBC_COLS), const),
                  pl.BlockSpec((1, LANES), const),
                  pl.BlockSpec((1, LANES), const),
                  pl.BlockSpec((1, D_SSM), const),
                  pl.BlockSpec((1, D_SSM), const),
                  pl.BlockSpec((L, L), const),
                  pl.BlockSpec((LANES, D_SSM), const)],
        out_specs=pl.BlockSpec((L, D_SSM), lambda c: (c, 0)),
        out_shape=jax.ShapeDtypeStruct((s, D_SSM), BF16),
        scratch_shapes=[pltpu.VMEM((XBC_COLS // LANES, SUBLANES + L, LANES), F32),
                        pltpu.VMEM((L, XBC_COLS), F32),
                        pltpu.VMEM((L, D_SSM), F32),
                        pltpu.VMEM((SSM_GROUPS, SSM_STATE, HEADS_PER_GROUP * HEAD_DIM), F32)],
        compiler_params=_params(("arbitrary",)),
        name="ssd",
    )(xbc, z, dt, conv_w, conv_b.reshape(1, XBC_COLS), dtb, alog, dexp,
      norm_g.reshape(1, D_SSM), tri, expand)


def _out_proj_kernel(ya_ref, ys_ref, w_ref, res_ref, g_ref, o_ref, hg_ref, r_ref, ss_ref):
    j = pl.program_id(1)
    bn = w_ref.shape[1]

    @pl.when(j == 0)
    def _():
        ss_ref[...] = jnp.zeros_like(ss_ref)

    acc = jnp.dot(ya_ref[...], w_ref[0:D_ATTN, :], preferred_element_type=F32)
    acc = acc + jnp.dot(ys_ref[...], w_ref[D_ATTN:D_ATTN + D_SSM, :], preferred_element_type=F32)
    h = res_ref[...] + acc
    o_ref[...] = h
    hg_ref[...] = (h * g_ref[...]).astype(hg_ref.dtype)
    sq = h * h
    part = sq[:, 0:LANES]
    for c in range(1, bn // LANES):
        part = part + sq[:, c * LANES:(c + 1) * LANES]
    ss_ref[...] += part

    @pl.when(j == pl.num_programs(1) - 1)
    def _():
        ms = jnp.sum(ss_ref[...], axis=-1, keepdims=True) * (1.0 / D_MODEL)
        r_ref[...] = lax.rsqrt(ms + EPS)


def _out_proj(ya, ys, w, res, g, bm=1024, bn=1024):
    m = ya.shape[0]
    k, n = w.shape
    return pl.pallas_call(
        _out_proj_kernel,
        grid=(m // bm, n // bn),
        in_specs=[pl.BlockSpec((bm, D_ATTN), lambda i, j: (i, 0)),
                  pl.BlockSpec((bm, D_SSM), lambda i, j: (i, 0)),
                  pl.BlockSpec((k, bn), lambda i, j: (0, j)),
                  pl.BlockSpec((bm, bn), lambda i, j: (i, j)),
                  pl.BlockSpec((1, bn), lambda i, j: (0, j))],
        out_specs=[pl.BlockSpec((bm, bn), lambda i, j: (i, j)),
                   pl.BlockSpec((bm, bn), lambda i, j: (i, j)),
                   pl.BlockSpec((bm, 1), lambda i, j: (i, 0))],
        out_shape=[jax.ShapeDtypeStruct((m, n), F32),
                   jax.ShapeDtypeStruct((m, n), BF16),
                   jax.ShapeDtypeStruct((m, 1), F32)],
        scratch_shapes=[pltpu.VMEM((bm, LANES), F32)],
        compiler_params=_params(("arbitrary", "arbitrary"), vmem=VMEM_LIMIT_HIGH),
        name="out_proj",
    )(ya, ys, w, res, g.reshape(1, n))


FFN_SUB_BLOCKS = 4


def _as_zero(v, zero_ref):
    return lax.bitcast_convert_type(jnp.max(v), jnp.int32) & zero_ref[0]


def _ffn_up_kernel(zero_ref, x_ref, r_ref, wg_ref, wu_ref, cw_ref, cb_ref, o_ref, ubuf_ref, wbf_ref):
    t = pl.program_id(0)
    n_steps = pl.num_programs(0)
    n_row_blocks = (n_steps - 1) // N_FFN_TILES
    bm = x_ref.shape[0]
    hb = bm // FFN_SUB_BLOCKS
    ft = FFN_TILE
    planes_per_half = ft // LANES
    row_block_new = jnp.minimum(t, n_steps - 2) % n_row_blocks

    @pl.when(t == 0)
    def _():
        ubuf_ref[...] = jnp.zeros_like(ubuf_ref)

    @pl.when((row_block_new == 0) & (t < n_steps - 1))
    def _():
        wbf_ref[0] = wg_ref[...].astype(BF16)
        wbf_ref[1] = wu_ref[...].astype(BF16)

    f_done = jnp.maximum(t - 1, 0) // n_row_blocks
    for sb in range(1, FFN_SUB_BLOCKS):
        ubuf_ref[sb, :, 0:SUBLANES, :] = ubuf_ref[sb - 1, :, hb:hb + SUBLANES, :]
    last_rows = ubuf_ref[FFN_SUB_BLOCKS - 1, :, hb:hb + SUBLANES, :]

    after_matmul = None
    after_tail = None
    for sb in range(FFN_SUB_BLOCKS):
        top = SUBLANES - 2 if after_matmul is None else SUBLANES - 2 + after_matmul
        folded = None
        for q in range(planes_per_half):
            conv = []
            for half in range(2):
                plane = half * planes_per_half + q
                col = pl.multiple_of(half * D_FF + f_done * ft + q * LANES, LANES)
                acc = ubuf_ref[sb, plane, pl.ds(top, hb), :] * cw_ref[0:1, pl.ds(col, LANES)]
                acc = acc + ubuf_ref[sb, plane, pl.ds(top + 1, hb), :] * cw_ref[1:2, pl.ds(col, LANES)]
                acc = acc + ubuf_ref[sb, plane, pl.ds(top + 2, hb), :] * cw_ref[2:3, pl.ds(col, LANES)]
                conv.append(acc + cb_ref[:, pl.ds(col, LANES)])
            gate, up = conv
            act = gate * jax.nn.sigmoid(gate) * up
            o_ref[sb * hb:(sb + 1) * hb, q * LANES:(q + 1) * LANES] = act.astype(o_ref.dtype)
            part = jnp.max(act.reshape(hb // SUBLANES, SUBLANES, LANES), axis=0)
            folded = part if folded is None else jnp.maximum(folded, part)
        tail_done = _as_zero(folded, zero_ref)
        if sb == 0:
            ubuf_ref[0, :, 0:SUBLANES, :] = jnp.where(row_block_new == 0, jnp.zeros_like(last_rows), last_rows)

        row0 = sb * hb if after_tail is None else pl.multiple_of(sb * hb + after_tail, hb)
        x = x_ref[pl.ds(row0, hb), :]
        r = r_ref[sb * hb:(sb + 1) * hb, :]
        first = None
        for half in range(2):
            u = jnp.dot(x, wbf_ref[half], preferred_element_type=F32) * r
            if first is None:
                first = u[0:SUBLANES, 0:LANES]
            for q in range(planes_per_half):
                ubuf_ref[sb, half * planes_per_half + q, SUBLANES:SUBLANES + hb, :] = u[:, q * LANES:(q + 1) * LANES]
        after_matmul = _as_zero(first, zero_ref)
        after_tail = tail_done


def _ffn_up(hg, r, w, cw, cb, bm=1024):
    m, k = hg.shape
    nf = N_FFN_TILES
    nrb = m // bm
    n_tiles = nrb * nf
    n_planes = 2 * FFN_TILE // LANES
    cur = lambda t: jnp.minimum(t, n_tiles - 1)
    done = lambda t: jnp.maximum(t - 1, 0)
    return pl.pallas_call(
        _ffn_up_kernel,
        grid=(n_tiles + 1,),
        in_specs=[pl.BlockSpec(memory_space=pltpu.SMEM),
                  pl.BlockSpec((bm, k), lambda t: (cur(t) % nrb, 0)),
                  pl.BlockSpec((bm, 1), lambda t: (cur(t) % nrb, 0)),
                  pl.BlockSpec((k, FFN_TILE), lambda t: (0, cur(t) // nrb)),
                  pl.BlockSpec((k, FFN_TILE), lambda t: (0, nf + cur(t) // nrb)),
                  pl.BlockSpec((FFN_CONV, 2 * D_FF), lambda t: (0, 0)),
                  pl.BlockSpec((1, 2 * D_FF), lambda t: (0, 0))],
        out_specs=pl.BlockSpec((bm, FFN_TILE), lambda t: (done(t) % nrb, done(t) // nrb)),
        out_shape=jax.ShapeDtypeStruct((m, D_FF), BF16),
        scratch_shapes=[pltpu.VMEM((FFN_SUB_BLOCKS, n_planes, SUBLANES + bm // FFN_SUB_BLOCKS, LANES), F32),
                        pltpu.VMEM((2, k, FFN_TILE), BF16)],
        compiler_params=_params(("arbitrary",)),
        name="ffn_up",
    )(jnp.zeros((1,), jnp.int32), hg, r, w, w, cw, cb)


def _ffn_down_kernel(a_ref, w_ref, res_ref, o_ref, obf_ref):
    h = res_ref[...] + jnp.dot(a_ref[...], w_ref[...], preferred_element_type=F32)
    o_ref[...] = h
    obf_ref[...] = h.astype(obf_ref.dtype)


def _ffn_down(act, w, res, bm=512, bn=1024):
    m, k = act.shape
    n = w.shape[1]
    return pl.pallas_call(
        _ffn_down_kernel,
        grid=(n // bn, m // bm),
        in_specs=[pl.BlockSpec((bm, k), lambda j, i: (i, 0)),
                  pl.BlockSpec((k, bn), lambda j, i: (0, j), pipeline_mode=pl.Buffered(1)),
                  pl.BlockSpec((bm, bn), lambda j, i: (i, j))],
        out_specs=[pl.BlockSpec((bm, bn), lambda j, i: (i, j)),
                   pl.BlockSpec((bm, bn), lambda j, i: (i, j))],
        out_shape=[jax.ShapeDtypeStruct((m, n), F32), jax.ShapeDtypeStruct((m, n), BF16)],
        compiler_params=_params(("arbitrary", "arbitrary"), vmem=VMEM_LIMIT_HIGH),
        name="ffn_down",
    )(act, w, res)


def _ple_kernel(hb_ref, wg_ref, bg_ref, p_ref, wp_ref, h_ref, gf_ref, o_ref, ss_ref):
    j = pl.program_id(1)
    nj = pl.num_programs(1)
    bn = wg_ref.shape[1]

    @pl.when(j == 0)
    def _():
        ss_ref[...] = jnp.zeros_like(ss_ref)

    gate = jax.nn.sigmoid(jnp.dot(hb_ref[...], wg_ref[...], preferred_element_type=F32) + bg_ref[...])
    emb = jnp.dot(p_ref[...].astype(BF16), wp_ref[...], preferred_element_type=F32)
    h = h_ref[...] + gate * emb
    sq = h * h
    part = sq[:, 0:LANES]
    for c in range(1, bn // LANES):
        part = part + sq[:, c * LANES:(c + 1) * LANES]
    ss_ref[...] += part
    col = pl.multiple_of(j * bn, bn)
    o_ref[:, pl.ds(col, bn)] = h

    @pl.when(j == nj - 1)
    def _():
        ms = jnp.sum(ss_ref[...], axis=-1, keepdims=True) * (1.0 / D_MODEL)
        o_ref[...] = o_ref[...] * lax.rsqrt(ms + EPS) * gf_ref[...]


def _ple(hb, wg, bg, p, wp, h, gf, bm=512, bn=1024):
    m, d = h.shape
    return pl.pallas_call(
        _ple_kernel,
        grid=(m // bm, d // bn),
        in_specs=[pl.BlockSpec((bm, d), lambda i, j: (i, 0)),
                  pl.BlockSpec((d, bn), lambda i, j: (0, j)),
                  pl.BlockSpec((1, bn), lambda i, j: (0, j)),
                  pl.BlockSpec((bm, PLE_DIM), lambda i, j: (i, 0)),
                  pl.BlockSpec((PLE_DIM, bn), lambda i, j: (0, j)),
                  pl.BlockSpec((bm, bn), lambda i, j: (i, j)),
                  pl.BlockSpec((1, d), lambda i, j: (0, 0))],
        out_specs=pl.BlockSpec((bm, d), lambda i, j: (i, 0)),
        out_shape=jax.ShapeDtypeStruct((m, d), F32),
        scratch_shapes=[pltpu.VMEM((bm, LANES), F32)],
        compiler_params=_params(("arbitrary", "arbitrary"), vmem=VMEM_LIMIT_HIGH),
        name="ple",
    )(hb, wg, bg, p, wp, h, gf)


def kernel(x, p, positions, norm_mix_g, w_in, attn_sinks, ssm_conv_w, ssm_conv_b, ssm_dt_bias,
           ssm_a_log, ssm_d, ssm_norm_g, w_o, norm_ffn_g, w_up, ffn_conv_w, ffn_conv_b, w_down,
           ple_gate_w, ple_gate_b, ple_proj, norm_final_g):
    b, s, d = x.shape
    assert (b, s, d) == (1, SEQ, D_MODEL) and w_in.shape[0] == 1
    h = x.reshape(s, d)
    pos = positions.reshape(s, 1)
    half = HEAD_DIM // 2
    inv_freq = ROPE_THETA ** (-jnp.arange(half, dtype=F32) / half)
    freq = jnp.tile(inv_freq, LANES // half).reshape(1, LANES)
    w_dt = lax.slice(w_in, (0, 0, DT_COL0), (1, d, DT_COL0 + N_SSM_HEADS)).reshape(d, N_SSM_HEADS)
    w_dt = jnp.pad(w_dt, ((0, 0), (0, LANES - N_SSM_HEADS)))

    hn = _norm_cast(h, norm_mix_g[0])
    w_qkv, w_z, w_xbc = _split_w_in(jnp.swapaxes(w_in, 1, 2))
    qkv, dt = _proj(hn, w_qkv, QKV_TILE, "in_proj_qkv", w_dt=w_dt.astype(BF16))
    z, = _proj(hn, w_z, SSM_TILE, "in_proj_z")
    xbc, = _proj(hn, w_xbc, SSM_TILE, "in_proj_xbc")
    y_attn = _swa(qkv, pos, freq, attn_sinks[0])
    y_ssm = _ssd(xbc, z, dt, ssm_conv_w[0], ssm_conv_b[0], ssm_dt_bias[0], ssm_a_log[0], ssm_d[0],
                 ssm_norm_g[0])
    h1, hg, r = _out_proj(y_attn, y_ssm, w_o[0].astype(BF16), h, norm_ffn_g[0])
    act = _ffn_up(hg, r, w_up[0], ffn_conv_w[0], ffn_conv_b[0].reshape(1, 2 * D_FF))
    h2, h2b = _ffn_down(act, w_down[0].astype(BF16), h1)
    out = _ple(h2b, ple_gate_w[0].astype(BF16), ple_gate_b[0].reshape(1, d), p[0].reshape(s, PLE_DIM),
               ple_proj[0].astype(BF16), h2, norm_final_g.reshape(1, d))
    return out.reshape(b, s, d)
```

```python
import jax
import jax.numpy as jnp
from jax import lax
from jax.experimental import pallas as pl
from jax.experimental.pallas import tpu as pltpu

F32 = jnp.float32
BF16 = jnp.bfloat16

D_MODEL = 4096
SEQ = 16384
PLE_DIM = 256
D_ATTN = 2048
D_SSM = 2048
HEAD_DIM = 64
N_Q_HEADS = 32
N_KV_HEADS = 4
Q_PER_KV = 8
WINDOW = 128
ROPE_THETA = 10000.0
N_SSM_HEADS = 32
SSM_STATE = 128
SSM_GROUPS = 8
HEADS_PER_GROUP = 4
SSM_CONV = 4
CHUNK = 128
D_FF = 11008
FFN_CONV = 3
EPS = 1e-6
KV_COLS = N_KV_HEADS * HEAD_DIM
QKV_COLS = D_ATTN + 2 * KV_COLS
XBC_COLS = D_SSM + 2 * SSM_GROUPS * SSM_STATE
DT_COL0 = QKV_COLS + D_SSM + XBC_COLS

LANES = 128
SUBLANES = 8
VMEM_LIMIT = 56 * 1024 * 1024
VMEM_LIMIT_HIGH = 62 * 1024 * 1024

QKV_TILE = QKV_COLS // 2
SSM_TILE = 1024
FFN_TILE = 256
N_FFN_TILES = D_FF // FFN_TILE


def _params(sem, vmem=VMEM_LIMIT, flags=None):
    return pltpu.CompilerParams(dimension_semantics=sem, vmem_limit_bytes=vmem, flags=flags)


def _norm_cast_kernel(x_ref, g_ref, o_ref):
    x = x_ref[...]
    ms = jnp.mean(x * x, axis=-1, keepdims=True)
    o_ref[...] = (x * lax.rsqrt(ms + EPS) * g_ref[...]).astype(o_ref.dtype)


def _norm_cast(x, g, bm=512):
    m, d = x.shape
    return pl.pallas_call(
        _norm_cast_kernel,
        grid=(m // bm,),
        in_specs=[pl.BlockSpec((bm, d), lambda i: (i, 0)),
                  pl.BlockSpec((1, d), lambda i: (0, 0))],
        out_specs=pl.BlockSpec((bm, d), lambda i: (i, 0)),
        out_shape=jax.ShapeDtypeStruct((m, d), BF16),
        compiler_params=_params(("arbitrary",)),
        name="norm_cast",
    )(x, g.reshape(1, d))


W_IN_CAST_TILE = 512
W_IN_QKV_TILES = QKV_COLS // W_IN_CAST_TILE
W_IN_Z_TILES = D_SSM // W_IN_CAST_TILE
W_IN_XBC_TILES = XBC_COLS // W_IN_CAST_TILE


def _split_w_in_kernel(wt_ref, qkv_ref, z_ref, xbc_ref):
    j = pl.program_id(0)

    @pl.when(j < W_IN_QKV_TILES)
    def _():
        qkv_ref[...] = wt_ref[...].T.astype(qkv_ref.dtype)

    @pl.when((j >= W_IN_QKV_TILES) & (j < W_IN_QKV_TILES + W_IN_Z_TILES))
    def _():
        z_ref[...] = wt_ref[...].T.astype(z_ref.dtype)

    @pl.when(j >= W_IN_QKV_TILES + W_IN_Z_TILES)
    def _():
        xbc_ref[...] = wt_ref[...].T.astype(xbc_ref.dtype)


def _split_w_in(wt):
    k = wt.shape[2]
    bn = W_IN_CAST_TILE
    q, z = W_IN_QKV_TILES, W_IN_Z_TILES
    return pl.pallas_call(
        _split_w_in_kernel,
        grid=(q + z + W_IN_XBC_TILES,),
        in_specs=[pl.BlockSpec((None, bn, k), lambda j: (0, j, 0))],
        out_specs=[pl.BlockSpec((k, bn), lambda j: (0, jnp.minimum(j, q - 1))),
                   pl.BlockSpec((k, bn), lambda j: (0, jnp.clip(j - q, 0, z - 1))),
                   pl.BlockSpec((k, bn), lambda j: (0, jnp.clip(j - q - z, 0, W_IN_XBC_TILES - 1)))],
        out_shape=[jax.ShapeDtypeStruct((k, QKV_COLS), BF16),
                   jax.ShapeDtypeStruct((k, D_SSM), BF16),
                   jax.ShapeDtypeStruct((k, XBC_COLS), BF16)],
        compiler_params=_params(("arbitrary",)),
        name="split_w_in",
    )(wt)


def _proj_kernel(x_ref, w_ref, o_ref):
    o_ref[...] = jnp.dot(x_ref[...], w_ref[...], preferred_element_type=F32)


def _proj_dt_kernel(x_ref, w_ref, wdt_ref, o_ref, dt_ref):
    o_ref[...] = jnp.dot(x_ref[...], w_ref[...], preferred_element_type=F32)

    @pl.when(pl.program_id(1) == 0)
    def _():
        dt_ref[...] = jnp.dot(x_ref[...], wdt_ref[...], preferred_element_type=F32)


def _proj_side_kernel(x_ref, w_ref, side_ref, o_ref, side_bf_ref):
    o_ref[...] = jnp.dot(x_ref[...], w_ref[...], preferred_element_type=F32)
    side_bf_ref[...] = side_ref[...].astype(side_bf_ref.dtype)


def _proj(hn, w, bn, name, w_dt=None, side=None, bm=1024):
    m, k = hn.shape
    n = w.shape[1]
    nj = n // bn
    in_specs = [pl.BlockSpec((bm, k), lambda i, j: (i, 0)),
                pl.BlockSpec((k, bn), lambda i, j: (0, j))]
    out_specs = [pl.BlockSpec((bm, bn), lambda i, j: (i, j))]
    out_shape = [jax.ShapeDtypeStruct((m, n), F32)]
    args = [hn, w]
    body = _proj_kernel
    if w_dt is not None:
        in_specs.append(pl.BlockSpec((k, LANES), lambda i, j: (0, 0)))
        out_specs.append(pl.BlockSpec((bm, LANES), lambda i, j: (i, 0)))
        out_shape.append(jax.ShapeDtypeStruct((m, LANES), F32))
        args.append(w_dt)
        body = _proj_dt_kernel
    if side is not None:
        rows, cols = side.shape
        slab = rows // ((m // bm) * nj)
        assert slab * (m // bm) * nj == rows and w_dt is None
        in_specs.append(pl.BlockSpec((slab, cols), lambda i, j: (i * nj + j, 0)))
        out_specs.append(pl.BlockSpec((slab, cols), lambda i, j: (i * nj + j, 0)))
        out_shape.append(jax.ShapeDtypeStruct((rows, cols), BF16))
        args.append(side)
        body = _proj_side_kernel
    return pl.pallas_call(
        body,
        grid=(m // bm, nj),
        in_specs=in_specs,
        out_specs=out_specs,
        out_shape=out_shape,
        compiler_params=_params(("arbitrary", "arbitrary")),
        name=name,
    )(*args)


def _rope_tile(x, cos, sin_signed, first_half):
    rot = jnp.where(first_half, pltpu.roll(x, LANES - HEAD_DIM // 2, 1),
                    pltpu.roll(x, HEAD_DIM // 2, 1))
    return x * cos + rot * sin_signed


LOG2E = 1.4426950408889634
VT_ROWS = 80


def _swa_kernel(pos_ref, freq_ref, sink_ref, q_ref, k_ref, v_ref, o_ref, kband_ref, vt_ref):
    n = pl.program_id(0)
    w = WINDOW

    lane = lax.broadcasted_iota(jnp.int32, (1, LANES), 1)
    low_half = lane < HEAD_DIM
    first_half = (lane % HEAD_DIM) < (HEAD_DIM // 2)

    @pl.when(n == 0)
    def _():
        kband_ref[...] = jnp.zeros_like(kband_ref)
        vt_ref[...] = jnp.zeros_like(vt_ref)
        ones_row = jnp.where(lax.broadcasted_iota(jnp.int32, (VT_ROWS - HEAD_DIM, 2 * w), 0) == 0, 1.0, 0.0)
        for kh in range(N_KV_HEADS):
            vt_ref[kh, HEAD_DIM:VT_ROWS, :] = ones_row.astype(BF16)

    ang = pos_ref[...].astype(F32) * freq_ref[...]
    cos = jnp.cos(ang)
    sin_signed = jnp.where(first_half, -jnp.sin(ang), jnp.sin(ang))

    for kh in range(N_KV_HEADS):
        kband_ref[kh, 0:w, :] = kband_ref[kh, w:2 * w, :]
        vt_ref[kh, 0:HEAD_DIM, 0:w] = vt_ref[kh, 0:HEAD_DIM, w:2 * w]
    for t in range(KV_COLS // LANES):
        k_rot = _rope_tile(k_ref[:, t * LANES:(t + 1) * LANES], cos, sin_signed, first_half)
        k_sw = pltpu.roll(k_rot, HEAD_DIM, 1)
        kband_ref[2 * t, w:2 * w, :] = jnp.where(low_half, k_rot, k_sw).astype(BF16)
        kband_ref[2 * t + 1, w:2 * w, :] = jnp.where(low_half, k_sw, k_rot).astype(BF16)
        v_t = v_ref[:, t * LANES:(t + 1) * LANES].T
        vt_ref[2 * t, 0:HEAD_DIM, w:2 * w] = v_t[0:HEAD_DIM, :].astype(BF16)
        vt_ref[2 * t + 1, 0:HEAD_DIM, w:2 * w] = v_t[HEAD_DIM:2 * HEAD_DIM, :].astype(BF16)

    ki = lax.broadcasted_iota(jnp.int32, (2 * w, w), 0)
    qi = lax.broadcasted_iota(jnp.int32, (2 * w, w), 1)
    prev_off = jnp.where(n > 0, 0, 2 * w)
    valid = ((ki < w) & (ki > qi + prev_off)) | ((ki >= w) & (ki - w <= qi))
    qscale = HEAD_DIM ** -0.5 * LOG2E
    pairs = Q_PER_KV // 2
    eye = (lax.broadcasted_iota(jnp.int32, (LANES, LANES), 0)
           == lax.broadcasted_iota(jnp.int32, (LANES, LANES), 1)).astype(BF16)

    for kh in range(N_KV_HEADS):
        lhs = []
        for j in range(pairs):
            t = kh * pairs + j
            q_rot = _rope_tile(q_ref[:, t * LANES:(t + 1) * LANES], cos, sin_signed, first_half) * qscale
            lhs.append(jnp.where(low_half, q_rot, 0.0).astype(BF16))
            lhs.append(jnp.where(low_half, 0.0, q_rot).astype(BF16))
        st = lax.dot_general(kband_ref[kh], jnp.concatenate(lhs, axis=0), (((1,), (1,)), ((), ())),
                             preferred_element_type=F32)
        probs = []
        sink_terms = []
        for hq in range(Q_PER_KV):
            s = jnp.where(valid, st[:, hq * w:(hq + 1) * w], -jnp.inf)
            sink = sink_ref[kh * Q_PER_KV + hq] * LOG2E
            m = jnp.maximum(jnp.max(s, axis=0, keepdims=True), sink)
            probs.append(jnp.exp2(s - m).astype(BF16))
            sink_terms.append(jnp.exp2(sink - m))
        ot = jnp.dot(vt_ref[kh], jnp.concatenate(probs, axis=1), preferred_element_type=F32)
        for j in range(pairs):
            halves = []
            for r in range(2):
                hq = 2 * j + r
                cols = slice(hq * w, (hq + 1) * w)
                denom = ot[HEAD_DIM:HEAD_DIM + 1, cols] + sink_terms[hq]
                halves.append(ot[0:HEAD_DIM, cols] / denom)
            pair_t = jnp.concatenate(halves, axis=0).astype(BF16)
            out = lax.dot_general(eye, pair_t, (((1,), (1,)), ((), ())), preferred_element_type=F32)
            t = kh * pairs + j
            o_ref[:, t * LANES:(t + 1) * LANES] = out.astype(o_ref.dtype)


def _swa(qkv, pos, freq, sinks):
    s = qkv.shape[0]
    w = WINDOW
    return pl.pallas_call(
        _swa_kernel,
        grid=(s // w,),
        in_specs=[pl.BlockSpec((w, 1), lambda n: (n, 0)),
                  pl.BlockSpec((1, LANES), lambda n: (0, 0)),
                  pl.BlockSpec(memory_space=pltpu.SMEM),
                  pl.BlockSpec((w, D_ATTN), lambda n: (n, 0)),
                  pl.BlockSpec((w, KV_COLS), lambda n: (n, D_ATTN // KV_COLS)),
                  pl.BlockSpec((w, KV_COLS), lambda n: (n, D_ATTN // KV_COLS + 1))],
        out_specs=pl.BlockSpec((w, D_ATTN), lambda n: (n, 0)),
        out_shape=jax.ShapeDtypeStruct((s, D_ATTN), BF16),
        scratch_shapes=[pltpu.VMEM((N_KV_HEADS, 2 * w, LANES), BF16),
                        pltpu.VMEM((N_KV_HEADS, VT_ROWS, 2 * w), BF16)],
        compiler_params=_params(("arbitrary",)),
        name="swa",
    )(pos, freq, sinks, qkv, qkv, qkv)


def _split3(v):
    hi = v.astype(BF16)
    r = v - hi.astype(F32)
    mid = r.astype(BF16)
    lo = (r - mid.astype(F32)).astype(BF16)
    return hi, mid, lo


def _dot_exact_rhs(a, b_parts):
    acc = jnp.dot(a, b_parts[0], preferred_element_type=F32)
    for part in b_parts[1:]:
        acc = acc + jnp.dot(a, part, preferred_element_type=F32)
    return acc


def _dot_exact_lhs(a_parts, b):
    acc = jnp.dot(a_parts[0], b, preferred_element_type=F32)
    for part in a_parts[1:]:
        acc = acc + jnp.dot(part, b, preferred_element_type=F32)
    return acc


def _ssd_kernel(xbc_ref, z_ref, dt_ref, cw_ref, cb_ref, dtb_ref, alog_ref, dexp_ref,
                ng_ref, tri_ref, expand_ref, o_ref, xpad_ref, xc_ref, y_ref, state_ref):
    c = pl.program_id(0)
    L = CHUNK
    G, R, P, N = SSM_GROUPS, HEADS_PER_GROUP, HEAD_DIM, SSM_STATE
    GP = R * P

    @pl.when(c == 0)
    def _():
        state_ref[...] = jnp.zeros_like(state_ref)
        xpad_ref[:, L:L + SUBLANES, :] = jnp.zeros((XBC_COLS // LANES, SUBLANES, LANES), F32)

    xpad_ref[:, 0:SUBLANES, :] = xpad_ref[:, L:L + SUBLANES, :]

    for p in range(XBC_COLS // LANES):
        cs = slice(p * LANES, (p + 1) * LANES)
        xpad_ref[p, SUBLANES:SUBLANES + L, :] = xbc_ref[:, cs]
        acc = xpad_ref[p, SUBLANES - 3:SUBLANES - 3 + L, :] * cw_ref[0:1, cs]
        for k in range(1, SSM_CONV):
            acc = acc + xpad_ref[p, SUBLANES - 3 + k:SUBLANES - 3 + k + L, :] * cw_ref[k:k + 1, cs]
        acc = acc + cb_ref[:, cs]
        xc_ref[:, cs] = acc * jax.nn.sigmoid(acc)

    dt_raw = dt_ref[...] + dtb_ref[...]
    dt = jnp.maximum(dt_raw, 0.0) + jnp.log1p(jnp.exp(-jnp.abs(dt_raw)))
    a = -jnp.exp(alog_ref[...])
    da = dt * a
    a_cs = _dot_exact_rhs(tri_ref[...], _split3(da))
    a_cs_t = a_cs.T
    expand = expand_ref[...]
    acs_x = _dot_exact_lhs(_split3(a_cs), expand)
    dt_x = _dot_exact_lhs(_split3(dt), expand)

    li = lax.broadcasted_iota(jnp.int32, (L, L), 0)
    si = lax.broadcasted_iota(jnp.int32, (L, L), 1)
    causal = li >= si
    lane = lax.broadcasted_iota(jnp.int32, (1, LANES), 1)
    low_half = lane < P

    for g in range(G):
        xs = slice(g * GP, (g + 1) * GP)
        x_g = xc_ref[:, xs]
        b_g = xc_ref[:, D_SSM + g * N:D_SSM + (g + 1) * N].astype(BF16)
        c_g = xc_ref[:, D_SSM + G * N + g * N:D_SSM + G * N + (g + 1) * N].astype(BF16)
        acs_g = acs_x[:, xs]
        xdt_g = x_g * dt_x[:, xs]
        cb = lax.dot_general(c_g, b_g, (((1,), (1,)), ((), ())), preferred_element_type=F32)

        y_pairs = []
        for j in range(R // 2):
            ws = []
            for r in range(2):
                h = g * R + 2 * j + r
                seg = a_cs[:, h:h + 1] - a_cs_t[h:h + 1, :]
                decay = jnp.exp(jnp.where(causal, seg, -jnp.inf))
                ws.append((cb * decay).astype(BF16))
            wcat = jnp.concatenate(ws, axis=1)
            xp = xdt_g[:, j * LANES:(j + 1) * LANES]
            rhs = jnp.concatenate([jnp.where(low_half, xp, 0.0), jnp.where(low_half, 0.0, xp)],
                                  axis=0).astype(BF16)
            y_pairs.append(jnp.dot(wcat, rhs, preferred_element_type=F32))
        y_diag = jnp.concatenate(y_pairs, axis=1)

        a_last = acs_g[L - 1:L, :]
        xdte = (xdt_g * jnp.exp(a_last - acs_g)).astype(BF16)
        new_state = lax.dot_general(b_g, xdte, (((0,), (0,)), ((), ())),
                                    preferred_element_type=F32)
        prev = state_ref[g]
        y_off = jnp.dot(c_g, prev.astype(BF16), preferred_element_type=F32) * jnp.exp(acs_g)
        state_ref[g] = prev * jnp.exp(a_last) + new_state
        y_ref[:, xs] = y_diag + y_off + x_g * dexp_ref[:, xs]

    z = z_ref[...]
    y = y_ref[...] * (z * jax.nn.sigmoid(z))
    ms = jnp.mean(y * y, axis=-1, keepdims=True)
    o_ref[...] = (y * lax.rsqrt(ms + EPS) * ng_ref[...]).astype(o_ref.dtype)


def _ssd(xbc, z, dt, conv_w, conv_b, dt_bias, a_log, d_skip, norm_g):
    s = xbc.shape[0]
    L = CHUNK
    pad = LANES - N_SSM_HEADS
    dtb = jnp.pad(dt_bias, (0, pad)).reshape(1, LANES)
    alog = jnp.pad(a_log, (0, pad)).reshape(1, LANES)
    dexp = jnp.repeat(d_skip, HEAD_DIM).reshape(1, D_SSM)
    tri = (jnp.arange(L)[:, None] >= jnp.arange(L)[None, :]).astype(BF16)
    expand = (jnp.arange(LANES)[:, None] == (jnp.arange(D_SSM)[None, :] // HEAD_DIM)).astype(BF16)
    const = lambda c: (0, 0)
    return pl.pallas_call(
        _ssd_kernel,
        grid=(s // L,),
        in_specs=[pl.BlockSpec((L, XBC_COLS), lambda c: (c, 0)),
                  pl.BlockSpec((L, D_SSM), lambda c: (c, 0)),
                  pl.BlockSpec((L, LANES), lambda c: (c, 0)),
                  pl.BlockSpec((SSM_CONV, XBC_COLS), const),
                  pl.BlockSpec((1, XBC_COLS), const),
                  pl.BlockSpec((1, LANES), const),
                  pl.BlockSpec((1, LANES), const),
                  pl.BlockSpec((1, D_SSM), const),
                  pl.BlockSpec((1, D_SSM), const),
                  pl.BlockSpec((L, L), const),
                  pl.BlockSpec((LANES, D_SSM), const)],
        out_specs=pl.BlockSpec((L, D_SSM), lambda c: (c, 0)),
        out_shape=jax.ShapeDtypeStruct((s, D_SSM), BF16),
        scratch_shapes=[pltpu.VMEM((XBC_COLS // LANES, SUBLANES + L, LANES), F32),
                        pltpu.VMEM((L, XBC_COLS), F32),
                        pltpu.VMEM((L, D_SSM), F32),
                        pltpu.VMEM((SSM_GROUPS, SSM_STATE, HEADS_PER_GROUP * HEAD_DIM), F32)],
        compiler_params=_params(("arbitrary",)),
        name="ssd",
    )(xbc, z, dt, conv_w, conv_b.reshape(1, XBC_COLS), dtb, alog, dexp,
      norm_g.reshape(1, D_SSM), tri, expand)


def _out_proj_kernel(ya_ref, ys_ref, w_ref, res_ref, g_ref, o_ref, hg_ref, r_ref, ss_ref):
    j = pl.program_id(1)
    bn = w_ref.shape[1]

    @pl.when(j == 0)
    def _():
        ss_ref[...] = jnp.zeros_like(ss_ref)

    acc = jnp.dot(ya_ref[...], w_ref[0:D_ATTN, :], preferred_element_type=F32)
    acc = acc + jnp.dot(ys_ref[...], w_ref[D_ATTN:D_ATTN + D_SSM, :], preferred_element_type=F32)
    h = res_ref[...] + acc
    o_ref[...] = h
    hg_ref[...] = (h * g_ref[...]).astype(hg_ref.dtype)
    sq = h * h
    part = sq[:, 0:LANES]
    for c in range(1, bn // LANES):
        part = part + sq[:, c * LANES:(c + 1) * LANES]
    ss_ref[...] += part

    @pl.when(j == pl.num_programs(1) - 1)
    def _():
        ms = jnp.sum(ss_ref[...], axis=-1, keepdims=True) * (1.0 / D_MODEL)
        r_ref[...] = lax.rsqrt(ms + EPS)


def _out_proj(ya, ys, w, res, g, bm=1024, bn=1024):
    m = ya.shape[0]
    k, n = w.shape
    return pl.pallas_call(
        _out_proj_kernel,
        grid=(m // bm, n // bn),
        in_specs=[pl.BlockSpec((bm, D_ATTN), lambda i, j: (i, 0)),
                  pl.BlockSpec((bm, D_SSM), lambda i, j: (i, 0)),
                  pl.BlockSpec((k, bn), lambda i, j: (0, j)),
                  pl.BlockSpec((bm, bn), lambda i, j: (i, j)),
                  pl.BlockSpec((1, bn), lambda i, j: (0, j))],
        out_specs=[pl.BlockSpec((bm, bn), lambda i, j: (i, j)),
                   pl.BlockSpec((bm, bn), lambda i, j: (i, j)),
                   pl.BlockSpec((bm, 1), lambda i, j: (i, 0))],
        out_shape=[jax.ShapeDtypeStruct((m, n), F32),
                   jax.ShapeDtypeStruct((m, n), BF16),
                   jax.ShapeDtypeStruct((m, 1), F32)],
        scratch_shapes=[pltpu.VMEM((bm, LANES), F32)],
        compiler_params=_params(("arbitrary", "arbitrary"), vmem=VMEM_LIMIT_HIGH),
        name="out_proj",
    )(ya, ys, w, res, g.reshape(1, n))


def _ffn_up_kernel(x_ref, r_ref, wg_ref, wu_ref, cw_ref, cb_ref, side_ref, o_ref, side_bf_ref, ubuf_ref, wbf_ref):
    t = pl.program_id(0)
    n_steps = pl.num_programs(0)
    n_row_blocks = (n_steps - 1) // N_FFN_TILES
    bm = x_ref.shape[0]
    ft = FFN_TILE
    planes_per_half = ft // LANES
    row_block_new = jnp.minimum(t, n_steps - 2) % n_row_blocks
    side_bf_ref[...] = side_ref[...].astype(side_bf_ref.dtype)

    @pl.when(t == 0)
    def _():
        ubuf_ref[...] = jnp.zeros_like(ubuf_ref)

    @pl.when((row_block_new == 0) & (t < n_steps - 1))
    def _():
        wbf_ref[0] = wg_ref[...].astype(BF16)
        wbf_ref[1] = wu_ref[...].astype(BF16)

    f_done = jnp.maximum(t - 1, 0) // n_row_blocks
    for q in range(planes_per_half):
        conv = []
        for half in range(2):
            plane = half * planes_per_half + q
            col = pl.multiple_of(half * D_FF + f_done * ft + q * LANES, LANES)
            acc = ubuf_ref[plane, SUBLANES - 2:SUBLANES - 2 + bm, :] * cw_ref[0:1, pl.ds(col, LANES)]
            acc = acc + ubuf_ref[plane, SUBLANES - 1:SUBLANES - 1 + bm, :] * cw_ref[1:2, pl.ds(col, LANES)]
            acc = acc + ubuf_ref[plane, SUBLANES:SUBLANES + bm, :] * cw_ref[2:3, pl.ds(col, LANES)]
            conv.append(acc + cb_ref[:, pl.ds(col, LANES)])
        gate, up = conv
        o_ref[:, q * LANES:(q + 1) * LANES] = (gate * jax.nn.sigmoid(gate) * up).astype(o_ref.dtype)
    last_rows = ubuf_ref[:, bm:bm + SUBLANES, :]
    ubuf_ref[:, 0:SUBLANES, :] = jnp.where(row_block_new == 0, jnp.zeros_like(last_rows), last_rows)

    x = x_ref[...]
    r = r_ref[...]
    for half in range(2):
        u = jnp.dot(x, wbf_ref[half], preferred_element_type=F32) * r
        for q in range(planes_per_half):
            ubuf_ref[half * planes_per_half + q, SUBLANES:SUBLANES + bm, :] = u[:, q * LANES:(q + 1) * LANES]


def _ffn_up(hg, r, w, cw, cb, side, bm=1024):
    m, k = hg.shape
    nf = N_FFN_TILES
    nrb = m // bm
    n_tiles = nrb * nf
    n_planes = 2 * FFN_TILE // LANES
    side_rows, side_cols = side.shape
    slab = side_rows // n_tiles
    assert slab * n_tiles == side_rows
    cur = lambda t: jnp.minimum(t, n_tiles - 1)
    done = lambda t: jnp.maximum(t - 1, 0)
    return pl.pallas_call(
        _ffn_up_kernel,
        grid=(n_tiles + 1,),
        in_specs=[pl.BlockSpec((bm, k), lambda t: (cur(t) % nrb, 0)),
                  pl.BlockSpec((bm, 1), lambda t: (cur(t) % nrb, 0)),
                  pl.BlockSpec((k, FFN_TILE), lambda t: (0, cur(t) // nrb)),
                  pl.BlockSpec((k, FFN_TILE), lambda t: (0, nf + cur(t) // nrb)),
                  pl.BlockSpec((FFN_CONV, 2 * D_FF), lambda t: (0, 0)),
                  pl.BlockSpec((1, 2 * D_FF), lambda t: (0, 0)),
                  pl.BlockSpec((slab, side_cols), lambda t: (cur(t), 0))],
        out_specs=[pl.BlockSpec((bm, FFN_TILE), lambda t: (done(t) % nrb, done(t) // nrb)),
                   pl.BlockSpec((slab, side_cols), lambda t: (cur(t), 0))],
        out_shape=[jax.ShapeDtypeStruct((m, D_FF), BF16),
                   jax.ShapeDtypeStruct((side_rows, side_cols), BF16)],
        scratch_shapes=[pltpu.VMEM((n_planes, SUBLANES + bm, LANES), F32),
                        pltpu.VMEM((2, k, FFN_TILE), BF16)],
        compiler_params=_params(("arbitrary",)),
        name="ffn_up",
    )(hg, r, w, w, cw, cb, side)


def _ffn_down_kernel(a_ref, w_ref, res_ref, side_ref, o_ref, obf_ref, side_bf_ref):
    h = res_ref[...] + jnp.dot(a_ref[...], w_ref[...], preferred_element_type=F32)
    o_ref[...] = h
    obf_ref[...] = h.astype(obf_ref.dtype)
    side_bf_ref[...] = side_ref[...].astype(side_bf_ref.dtype)


def _ffn_down(act, w, res, side, bm=512, bn=1024):
    m, k = act.shape
    n = w.shape[1]
    ni = m // bm
    side_rows, side_cols = side.shape
    slab = side_rows // ((n // bn) * ni)
    assert slab * (n // bn) * ni == side_rows
    return pl.pallas_call(
        _ffn_down_kernel,
        grid=(n // bn, m // bm),
        in_specs=[pl.BlockSpec((bm, k), lambda j, i: (i, 0)),
                  pl.BlockSpec((k, bn), lambda j, i: (0, j), pipeline_mode=pl.Buffered(1)),
                  pl.BlockSpec((bm, bn), lambda j, i: (i, j)),
                  pl.BlockSpec((slab, side_cols), lambda j, i: (j * ni + i, 0))],
        out_specs=[pl.BlockSpec((bm, bn), lambda j, i: (i, j)),
                   pl.BlockSpec((bm, bn), lambda j, i: (i, j)),
                   pl.BlockSpec((slab, side_cols), lambda j, i: (j * ni + i, 0))],
        out_shape=[jax.ShapeDtypeStruct((m, n), F32), jax.ShapeDtypeStruct((m, n), BF16),
                   jax.ShapeDtypeStruct((side_rows, side_cols), BF16)],
        compiler_params=_params(("arbitrary", "arbitrary"), vmem=VMEM_LIMIT_HIGH),
        name="ffn_down",
    )(act, w, res, side)


def _ple_kernel(hb_ref, wg_ref, bg_ref, p_ref, wp_ref, h_ref, gf_ref, o_ref, ss_ref):
    j = pl.program_id(1)
    nj = pl.num_programs(1)
    bn = wg_ref.shape[1]

    @pl.when(j == 0)
    def _():
        ss_ref[...] = jnp.zeros_like(ss_ref)

    gate = jax.nn.sigmoid(jnp.dot(hb_ref[...], wg_ref[...], preferred_element_type=F32) + bg_ref[...])
    emb = jnp.dot(p_ref[...].astype(BF16), wp_ref[...], preferred_element_type=F32)
    h = h_ref[...] + gate * emb
    sq = h * h
    part = sq[:, 0:LANES]
    for c in range(1, bn // LANES):
        part = part + sq[:, c * LANES:(c + 1) * LANES]
    ss_ref[...] += part
    col = pl.multiple_of(j * bn, bn)
    o_ref[:, pl.ds(col, bn)] = h

    @pl.when(j == nj - 1)
    def _():
        ms = jnp.sum(ss_ref[...], axis=-1, keepdims=True) * (1.0 / D_MODEL)
        o_ref[...] = o_ref[...] * lax.rsqrt(ms + EPS) * gf_ref[...]


def _ple(hb, wg, bg, p, wp, h, gf, bm=512, bn=1024):
    m, d = h.shape
    return pl.pallas_call(
        _ple_kernel,
        grid=(m // bm, d // bn),
        in_specs=[pl.BlockSpec((bm, d), lambda i, j: (i, 0)),
                  pl.BlockSpec((d, bn), lambda i, j: (0, j)),
                  pl.BlockSpec((1, bn), lambda i, j: (0, j)),
                  pl.BlockSpec((bm, PLE_DIM), lambda i, j: (i, 0)),
                  pl.BlockSpec((PLE_DIM, bn), lambda i, j: (0, j)),
                  pl.BlockSpec((bm, bn), lambda i, j: (i, j)),
                  pl.BlockSpec((1, d), lambda i, j: (0, 0))],
        out_specs=pl.BlockSpec((bm, d), lambda i, j: (i, 0)),
        out_shape=jax.ShapeDtypeStruct((m, d), F32),
        scratch_shapes=[pltpu.VMEM((bm, LANES), F32)],
        compiler_params=_params(("arbitrary", "arbitrary"), vmem=VMEM_LIMIT_HIGH),
        name="ple",
    )(hb, wg, bg, p, wp, h, gf)


def kernel(x, p, positions, norm_mix_g, w_in, attn_sinks, ssm_conv_w, ssm_conv_b, ssm_dt_bias,
           ssm_a_log, ssm_d, ssm_norm_g, w_o, norm_ffn_g, w_up, ffn_conv_w, ffn_conv_b, w_down,
           ple_gate_w, ple_gate_b, ple_proj, norm_final_g):
    b, s, d = x.shape
    assert (b, s, d) == (1, SEQ, D_MODEL) and w_in.shape[0] == 1
    h = x.reshape(s, d)
    pos = positions.reshape(s, 1)
    half = HEAD_DIM // 2
    inv_freq = ROPE_THETA ** (-jnp.arange(half, dtype=F32) / half)
    freq = jnp.tile(inv_freq, LANES // half).reshape(1, LANES)
    w_dt = lax.slice(w_in, (0, 0, DT_COL0), (1, d, DT_COL0 + N_SSM_HEADS)).reshape(d, N_SSM_HEADS)
    w_dt = jnp.pad(w_dt, ((0, 0), (0, LANES - N_SSM_HEADS)))

    hn = _norm_cast(h, norm_mix_g[0])
    w_qkv, w_z, w_xbc = _split_w_in(jnp.swapaxes(w_in, 1, 2))
    qkv, dt = _proj(hn, w_qkv, QKV_TILE, "in_proj_qkv", w_dt=w_dt.astype(BF16))
    z, = _proj(hn, w_z, SSM_TILE, "in_proj_z")
    xbc, w_o_bf = _proj(hn, w_xbc, SSM_TILE, "in_proj_xbc", side=w_o[0])
    y_attn = _swa(qkv, pos, freq, attn_sinks[0])
    y_ssm = _ssd(xbc, z, dt, ssm_conv_w[0], ssm_conv_b[0], ssm_dt_bias[0], ssm_a_log[0], ssm_d[0],
                 ssm_norm_g[0])
    h1, hg, r = _out_proj(y_attn, y_ssm, w_o_bf, h, norm_ffn_g[0])
    act, w_down_bf = _ffn_up(hg, r, w_up[0], ffn_conv_w[0], ffn_conv_b[0].reshape(1, 2 * D_FF), w_down[0])
    h2, h2b, w_gate_bf = _ffn_down(act, w_down_bf, h1, ple_gate_w[0])
    out = _ple(h2b, w_gate_bf, ple_gate_b[0].reshape(1, d), p[0].reshape(s, PLE_DIM),
               ple_proj[0].astype(BF16), h2, norm_final_g.reshape(1, d))
    return out.reshape(b, s, d)
```

```python
import jax
import jax.numpy as jnp
from jax import lax
from jax.experimental import pallas as pl
from jax.experimental.pallas import tpu as pltpu

F32 = jnp.float32
BF16 = jnp.bfloat16

D_MODEL = 4096
SEQ = 16384
PLE_DIM = 256
D_ATTN = 2048
D_SSM = 2048
HEAD_DIM = 64
N_Q_HEADS = 32
N_KV_HEADS = 4
Q_PER_KV = 8
WINDOW = 128
ROPE_THETA = 10000.0
N_SSM_HEADS = 32
SSM_STATE = 128
SSM_GROUPS = 8
HEADS_PER_GROUP = 4
SSM_CONV = 4
CHUNK = 128
D_FF = 11008
FFN_CONV = 3
EPS = 1e-6
KV_COLS = N_KV_HEADS * HEAD_DIM
QKV_COLS = D_ATTN + 2 * KV_COLS
XBC_COLS = D_SSM + 2 * SSM_GROUPS * SSM_STATE
DT_COL0 = QKV_COLS + D_SSM + XBC_COLS

LANES = 128
SUBLANES = 8
VMEM_LIMIT = 56 * 1024 * 1024
VMEM_LIMIT_HIGH = 62 * 1024 * 1024

QKV_TILE = QKV_COLS // 2
SSM_TILE = 1024
FFN_TILE = 256
N_FFN_TILES = D_FF // FFN_TILE


def _params(sem, vmem=VMEM_LIMIT, flags=None):
    return pltpu.CompilerParams(dimension_semantics=sem, vmem_limit_bytes=vmem, flags=flags)


def _norm_cast_kernel(x_ref, g_ref, o_ref):
    x = x_ref[...]
    ms = jnp.mean(x * x, axis=-1, keepdims=True)
    o_ref[...] = (x * lax.rsqrt(ms + EPS) * g_ref[...]).astype(o_ref.dtype)


def _norm_cast(x, g, bm=512):
    m, d = x.shape
    return pl.pallas_call(
        _norm_cast_kernel,
        grid=(m // bm,),
        in_specs=[pl.BlockSpec((bm, d), lambda i: (i, 0)),
                  pl.BlockSpec((1, d), lambda i: (0, 0))],
        out_specs=pl.BlockSpec((bm, d), lambda i: (i, 0)),
        out_shape=jax.ShapeDtypeStruct((m, d), BF16),
        compiler_params=_params(("arbitrary",)),
        name="norm_cast",
    )(x, g.reshape(1, d))


W_IN_CAST_TILE = 512
W_IN_QKV_TILES = QKV_COLS // W_IN_CAST_TILE
W_IN_Z_TILES = D_SSM // W_IN_CAST_TILE
W_IN_XBC_TILES = XBC_COLS // W_IN_CAST_TILE


def _split_w_in_kernel(wt_ref, qkv_ref, z_ref, xbc_ref):
    j = pl.program_id(0)

    @pl.when(j < W_IN_QKV_TILES)
    def _():
        qkv_ref[...] = wt_ref[...].T.astype(qkv_ref.dtype)

    @pl.when((j >= W_IN_QKV_TILES) & (j < W_IN_QKV_TILES + W_IN_Z_TILES))
    def _():
        z_ref[...] = wt_ref[...].T.astype(z_ref.dtype)

    @pl.when(j >= W_IN_QKV_TILES + W_IN_Z_TILES)
    def _():
        xbc_ref[...] = wt_ref[...].T.astype(xbc_ref.dtype)


def _split_w_in(wt):
    k = wt.shape[2]
    bn = W_IN_CAST_TILE
    q, z = W_IN_QKV_TILES, W_IN_Z_TILES
    return pl.pallas_call(
        _split_w_in_kernel,
        grid=(q + z + W_IN_XBC_TILES,),
        in_specs=[pl.BlockSpec((None, bn, k), lambda j: (0, j, 0))],
        out_specs=[pl.BlockSpec((k, bn), lambda j: (0, jnp.minimum(j, q - 1))),
                   pl.BlockSpec((k, bn), lambda j: (0, jnp.clip(j - q, 0, z - 1))),
                   pl.BlockSpec((k, bn), lambda j: (0, jnp.clip(j - q - z, 0, W_IN_XBC_TILES - 1)))],
        out_shape=[jax.ShapeDtypeStruct((k, QKV_COLS), BF16),
                   jax.ShapeDtypeStruct((k, D_SSM), BF16),
                   jax.ShapeDtypeStruct((k, XBC_COLS), BF16)],
        compiler_params=_params(("arbitrary",)),
        name="split_w_in",
    )(wt)


def _proj_kernel(x_ref, w_ref, o_ref):
    o_ref[...] = jnp.dot(x_ref[...], w_ref[...], preferred_element_type=F32)


def _proj_dt_kernel(x_ref, w_ref, wdt_ref, o_ref, dt_ref):
    o_ref[...] = jnp.dot(x_ref[...], w_ref[...], preferred_element_type=F32)

    @pl.when(pl.program_id(1) == 0)
    def _():
        dt_ref[...] = jnp.dot(x_ref[...], wdt_ref[...], preferred_element_type=F32)


def _proj_side_kernel(x_ref, w_ref, side_ref, o_ref, side_bf_ref):
    o_ref[...] = jnp.dot(x_ref[...], w_ref[...], preferred_element_type=F32)
    side_bf_ref[...] = side_ref[...].astype(side_bf_ref.dtype)


def _proj(hn, w, bn, name, w_dt=None, side=None, bm=1024):
    m, k = hn.shape
    n = w.shape[1]
    nj = n // bn
    in_specs = [pl.BlockSpec((bm, k), lambda i, j: (i, 0)),
                pl.BlockSpec((k, bn), lambda i, j: (0, j))]
    out_specs = [pl.BlockSpec((bm, bn), lambda i, j: (i, j))]
    out_shape = [jax.ShapeDtypeStruct((m, n), F32)]
    args = [hn, w]
    body = _proj_kernel
    if w_dt is not None:
        in_specs.append(pl.BlockSpec((k, LANES), lambda i, j: (0, 0)))
        out_specs.append(pl.BlockSpec((bm, LANES), lambda i, j: (i, 0)))
        out_shape.append(jax.ShapeDtypeStruct((m, LANES), F32))
        args.append(w_dt)
        body = _proj_dt_kernel
    if side is not None:
        rows, cols = side.shape
        slab = rows // ((m // bm) * nj)
        assert slab * (m // bm) * nj == rows and w_dt is None
        in_specs.append(pl.BlockSpec((slab, cols), lambda i, j: (i * nj + j, 0)))
        out_specs.append(pl.BlockSpec((slab, cols), lambda i, j: (i * nj + j, 0)))
        out_shape.append(jax.ShapeDtypeStruct((rows, cols), BF16))
        args.append(side)
        body = _proj_side_kernel
    return pl.pallas_call(
        body,
        grid=(m // bm, nj),
        in_specs=in_specs,
        out_specs=out_specs,
        out_shape=out_shape,
        compiler_params=_params(("arbitrary", "arbitrary")),
        name=name,
    )(*args)


def _rope_tile(x, cos, sin_signed, first_half):
    rot = jnp.where(first_half, pltpu.roll(x, LANES - HEAD_DIM // 2, 1),
                    pltpu.roll(x, HEAD_DIM // 2, 1))
    return x * cos + rot * sin_signed


LOG2E = 1.4426950408889634
VT_ROWS = 80


def _swa_kernel(pos_ref, freq_ref, sink_ref, q_ref, k_ref, v_ref, o_ref, kband_ref, vt_ref):
    n = pl.program_id(0)
    w = WINDOW

    lane = lax.broadcasted_iota(jnp.int32, (1, LANES), 1)
    low_half = lane < HEAD_DIM
    first_half = (lane % HEAD_DIM) < (HEAD_DIM // 2)

    @pl.when(n == 0)
    def _():
        kband_ref[...] = jnp.zeros_like(kband_ref)
        vt_ref[...] = jnp.zeros_like(vt_ref)
        ones_row = jnp.where(lax.broadcasted_iota(jnp.int32, (VT_ROWS - HEAD_DIM, 2 * w), 0) == 0, 1.0, 0.0)
        for kh in range(N_KV_HEADS):
            vt_ref[kh, HEAD_DIM:VT_ROWS, :] = ones_row.astype(BF16)

    ang = pos_ref[...].astype(F32) * freq_ref[...]
    cos = jnp.cos(ang)
    sin_signed = jnp.where(first_half, -jnp.sin(ang), jnp.sin(ang))

    for kh in range(N_KV_HEADS):
        kband_ref[kh, 0:w, :] = kband_ref[kh, w:2 * w, :]
        vt_ref[kh, 0:HEAD_DIM, 0:w] = vt_ref[kh, 0:HEAD_DIM, w:2 * w]
    for t in range(KV_COLS // LANES):
        k_rot = _rope_tile(k_ref[:, t * LANES:(t + 1) * LANES], cos, sin_signed, first_half)
        k_sw = pltpu.roll(k_rot, HEAD_DIM, 1)
        kband_ref[2 * t, w:2 * w, :] = jnp.where(low_half, k_rot, k_sw).astype(BF16)
        kband_ref[2 * t + 1, w:2 * w, :] = jnp.where(low_half, k_sw, k_rot).astype(BF16)
        v_t = v_ref[:, t * LANES:(t + 1) * LANES].T
        vt_ref[2 * t, 0:HEAD_DIM, w:2 * w] = v_t[0:HEAD_DIM, :].astype(BF16)
        vt_ref[2 * t + 1, 0:HEAD_DIM, w:2 * w] = v_t[HEAD_DIM:2 * HEAD_DIM, :].astype(BF16)

    ki = lax.broadcasted_iota(jnp.int32, (2 * w, w), 0)
    qi = lax.broadcasted_iota(jnp.int32, (2 * w, w), 1)
    prev_off = jnp.where(n > 0, 0, 2 * w)
    valid = ((ki < w) & (ki > qi + prev_off)) | ((ki >= w) & (ki - w <= qi))
    qscale = HEAD_DIM ** -0.5 * LOG2E
    pairs = Q_PER_KV // 2
    eye = (lax.broadcasted_iota(jnp.int32, (LANES, LANES), 0)
           == lax.broadcasted_iota(jnp.int32, (LANES, LANES), 1)).astype(BF16)

    scores = []
    for kh in range(N_KV_HEADS):
        lhs = []
        for j in range(pairs):
            t = kh * pairs + j
            q_rot = _rope_tile(q_ref[:, t * LANES:(t + 1) * LANES], cos, sin_signed, first_half) * qscale
            lhs.append(jnp.where(low_half, q_rot, 0.0).astype(BF16))
            lhs.append(jnp.where(low_half, 0.0, q_rot).astype(BF16))
        scores.append(lax.dot_general(kband_ref[kh], jnp.concatenate(lhs, axis=0), (((1,), (1,)), ((), ())),
                                      preferred_element_type=F32))
    outs_t = []
    sink_terms_all = []
    for kh in range(N_KV_HEADS):
        st = scores[kh]
        probs = []
        sink_terms = []
        for hq in range(Q_PER_KV):
            s = jnp.where(valid, st[:, hq * w:(hq + 1) * w], -jnp.inf)
            sink = sink_ref[kh * Q_PER_KV + hq] * LOG2E
            m = jnp.maximum(jnp.max(s, axis=0, keepdims=True), sink)
            probs.append(jnp.exp2(s - m).astype(BF16))
            sink_terms.append(jnp.exp2(sink - m))
        outs_t.append(jnp.dot(vt_ref[kh], jnp.concatenate(probs, axis=1), preferred_element_type=F32))
        sink_terms_all.append(sink_terms)
    for kh in range(N_KV_HEADS):
        ot = outs_t[kh]
        sink_terms = sink_terms_all[kh]
        for j in range(pairs):
            halves = []
            for r in range(2):
                hq = 2 * j + r
                cols = slice(hq * w, (hq + 1) * w)
                denom = ot[HEAD_DIM:HEAD_DIM + 1, cols] + sink_terms[hq]
                halves.append(ot[0:HEAD_DIM, cols] / denom)
            pair_t = jnp.concatenate(halves, axis=0).astype(BF16)
            out = lax.dot_general(eye, pair_t, (((1,), (1,)), ((), ())), preferred_element_type=F32)
            t = kh * pairs + j
            o_ref[:, t * LANES:(t + 1) * LANES] = out.astype(o_ref.dtype)


def _swa(qkv, pos, freq, sinks):
    s = qkv.shape[0]
    w = WINDOW
    return pl.pallas_call(
        _swa_kernel,
        grid=(s // w,),
        in_specs=[pl.BlockSpec((w, 1), lambda n: (n, 0)),
                  pl.BlockSpec((1, LANES), lambda n: (0, 0)),
                  pl.BlockSpec(memory_space=pltpu.SMEM),
                  pl.BlockSpec((w, D_ATTN), lambda n: (n, 0)),
                  pl.BlockSpec((w, KV_COLS), lambda n: (n, D_ATTN // KV_COLS)),
                  pl.BlockSpec((w, KV_COLS), lambda n: (n, D_ATTN // KV_COLS + 1))],
        out_specs=pl.BlockSpec((w, D_ATTN), lambda n: (n, 0)),
        out_shape=jax.ShapeDtypeStruct((s, D_ATTN), BF16),
        scratch_shapes=[pltpu.VMEM((N_KV_HEADS, 2 * w, LANES), BF16),
                        pltpu.VMEM((N_KV_HEADS, VT_ROWS, 2 * w), BF16)],
        compiler_params=_params(("arbitrary",)),
        name="swa",
    )(pos, freq, sinks, qkv, qkv, qkv)


def _split3(v):
    hi = v.astype(BF16)
    r = v - hi.astype(F32)
    mid = r.astype(BF16)
    lo = (r - mid.astype(F32)).astype(BF16)
    return hi, mid, lo


def _dot_exact_rhs(a, b_parts):
    acc = jnp.dot(a, b_parts[0], preferred_element_type=F32)
    for part in b_parts[1:]:
        acc = acc + jnp.dot(a, part, preferred_element_type=F32)
    return acc


def _dot_exact_lhs(a_parts, b):
    acc = jnp.dot(a_parts[0], b, preferred_element_type=F32)
    for part in a_parts[1:]:
        acc = acc + jnp.dot(part, b, preferred_element_type=F32)
    return acc


def _ssd_kernel(xbc_ref, z_ref, dt_ref, cw_ref, cb_ref, dtb_ref, alog_ref, dexp_ref,
                ng_ref, tri_ref, expand_ref, o_ref, xpad_ref, xc_ref, y_ref, state_ref):
    c = pl.program_id(0)
    L = CHUNK
    G, R, P, N = SSM_GROUPS, HEADS_PER_GROUP, HEAD_DIM, SSM_STATE
    GP = R * P

    @pl.when(c == 0)
    def _():
        state_ref[...] = jnp.zeros_like(state_ref)
        xpad_ref[:, L:L + SUBLANES, :] = jnp.zeros((XBC_COLS // LANES, SUBLANES, LANES), F32)

    xpad_ref[:, 0:SUBLANES, :] = xpad_ref[:, L:L + SUBLANES, :]

    for p in range(XBC_COLS // LANES):
        cs = slice(p * LANES, (p + 1) * LANES)
        xpad_ref[p, SUBLANES:SUBLANES + L, :] = xbc_ref[:, cs]
        acc = xpad_ref[p, SUBLANES - 3:SUBLANES - 3 + L, :] * cw_ref[0:1, cs]
        for k in range(1, SSM_CONV):
            acc = acc + xpad_ref[p, SUBLANES - 3 + k:SUBLANES - 3 + k + L, :] * cw_ref[k:k + 1, cs]
        acc = acc + cb_ref[:, cs]
        xc_ref[:, cs] = acc * jax.nn.sigmoid(acc)

    dt_raw = dt_ref[...] + dtb_ref[...]
    dt = jnp.maximum(dt_raw, 0.0) + jnp.log1p(jnp.exp(-jnp.abs(dt_raw)))
    a = -jnp.exp(alog_ref[...]) * LOG2E
    da = dt * a
    a_cs = _dot_exact_rhs(tri_ref[...], _split3(da))
    a_cs_t = a_cs.T
    expand = expand_ref[...]
    acs_x = _dot_exact_lhs(_split3(a_cs), expand)
    dt_x = _dot_exact_lhs(_split3(dt), expand)

    li = lax.broadcasted_iota(jnp.int32, (L, L), 0)
    si = lax.broadcasted_iota(jnp.int32, (L, L), 1)
    causal = li >= si
    lane = lax.broadcasted_iota(jnp.int32, (1, LANES), 1)
    low_half = lane < P

    groups = []
    for g in range(G):
        xs = slice(g * GP, (g + 1) * GP)
        x_g = xc_ref[:, xs]
        b_g = xc_ref[:, D_SSM + g * N:D_SSM + (g + 1) * N].astype(BF16)
        c_g = xc_ref[:, D_SSM + G * N + g * N:D_SSM + G * N + (g + 1) * N].astype(BF16)
        acs_g = acs_x[:, xs]
        xdt_g = x_g * dt_x[:, xs]
        cb = lax.dot_general(c_g, b_g, (((1,), (1,)), ((), ())), preferred_element_type=F32)
        a_last = acs_g[L - 1:L, :]
        xdte = (xdt_g * jnp.exp2(a_last - acs_g)).astype(BF16)
        new_state = lax.dot_general(b_g, xdte, (((0,), (0,)), ((), ())),
                                    preferred_element_type=F32)
        prev = state_ref[g]
        y_off = jnp.dot(c_g, prev.astype(BF16), preferred_element_type=F32) * jnp.exp2(acs_g)
        state_ref[g] = prev * jnp.exp2(a_last) + new_state
        groups.append((xs, x_g, xdt_g, cb, y_off))

    for g in range(G):
        xs, x_g, xdt_g, cb, y_off = groups[g]
        y_pairs = []
        for j in range(R // 2):
            ws = []
            for r in range(2):
                h = g * R + 2 * j + r
                seg = a_cs[:, h:h + 1] - a_cs_t[h:h + 1, :]
                decay = jnp.exp2(jnp.where(causal, seg, -jnp.inf))
                ws.append((cb * decay).astype(BF16))
            wcat = jnp.concatenate(ws, axis=1)
            xp = xdt_g[:, j * LANES:(j + 1) * LANES]
            rhs = jnp.concatenate([jnp.where(low_half, xp, 0.0), jnp.where(low_half, 0.0, xp)],
                                  axis=0).astype(BF16)
            y_pairs.append(jnp.dot(wcat, rhs, preferred_element_type=F32))
        y_diag = jnp.concatenate(y_pairs, axis=1)
        y_ref[:, xs] = y_diag + y_off + x_g * dexp_ref[:, xs]

    z = z_ref[...]
    y = y_ref[...] * (z * jax.nn.sigmoid(z))
    ms = jnp.mean(y * y, axis=-1, keepdims=True)
    o_ref[...] = (y * lax.rsqrt(ms + EPS) * ng_ref[...]).astype(o_ref.dtype)


def _ssd(xbc, z, dt, conv_w, conv_b, dt_bias, a_log, d_skip, norm_g):
    s = xbc.shape[0]
    L = CHUNK
    pad = LANES - N_SSM_HEADS
    dtb = jnp.pad(dt_bias, (0, pad)).reshape(1, LANES)
    alog = jnp.pad(a_log, (0, pad)).reshape(1, LANES)
    dexp = jnp.repeat(d_skip, HEAD_DIM).reshape(1, D_SSM)
    tri = (jnp.arange(L)[:, None] >= jnp.arange(L)[None, :]).astype(BF16)
    expand = (jnp.arange(LANES)[:, None] == (jnp.arange(D_SSM)[None, :] // HEAD_DIM)).astype(BF16)
    const = lambda c: (0, 0)
    return pl.pallas_call(
        _ssd_kernel,
        grid=(s // L,),
        in_specs=[pl.BlockSpec((L, XBC_COLS), lambda c: (c, 0)),
                  pl.BlockSpec((L, D_SSM), lambda c: (c, 0)),
                  pl.BlockSpec((L, LANES), lambda c: (c, 0)),
                  pl.BlockSpec((SSM_CONV, XBC_COLS), const),
                  pl.BlockSpec((1, XBC_COLS), const),
                  pl.BlockSpec((1, LANES), const),
                  pl.BlockSpec((1, LANES), const),
                  pl.BlockSpec((1, D_SSM), const),
                  pl.BlockSpec((1, D_SSM), const),
                  pl.BlockSpec((L, L), const),
                  pl.BlockSpec((LANES, D_SSM), const)],
        out_specs=pl.BlockSpec((L, D_SSM), lambda c: (c, 0)),
        out_shape=jax.ShapeDtypeStruct((s, D_SSM), BF16),
        scratch_shapes=[pltpu.VMEM((XBC_COLS // LANES, SUBLANES + L, LANES), F32),
                        pltpu.VMEM((L, XBC_COLS), F32),
                        pltpu.VMEM((L, D_SSM), F32),
                        pltpu.VMEM((SSM_GROUPS, SSM_STATE, HEADS_PER_GROUP * HEAD_DIM), F32)],
        compiler_params=_params(("arbitrary",)),
        name="ssd",
    )(xbc, z, dt, conv_w, conv_b.reshape(1, XBC_COLS), dtb, alog, dexp,
      norm_g.reshape(1, D_SSM), tri, expand)


def _out_proj_kernel(ya_ref, ys_ref, w_ref, res_ref, g_ref, o_ref, hg_ref, r_ref, ss_ref):
    j = pl.program_id(1)
    bn = w_ref.shape[1]

    @pl.when(j == 0)
    def _():
        ss_ref[...] = jnp.zeros_like(ss_ref)

    acc = jnp.dot(ya_ref[...], w_ref[0:D_ATTN, :], preferred_element_type=F32)
    acc = acc + jnp.dot(ys_ref[...], w_ref[D_ATTN:D_ATTN + D_SSM, :], preferred_element_type=F32)
    h = res_ref[...] + acc
    o_ref[...] = h
    hg_ref[...] = (h * g_ref[...]).astype(hg_ref.dtype)
    sq = h * h
    part = sq[:, 0:LANES]
    for c in range(1, bn // LANES):
        part = part + sq[:, c * LANES:(c + 1) * LANES]
    ss_ref[...] += part

    @pl.when(j == pl.num_programs(1) - 1)
    def _():
        ms = jnp.sum(ss_ref[...], axis=-1, keepdims=True) * (1.0 / D_MODEL)
        r_ref[...] = lax.rsqrt(ms + EPS)


def _out_proj(ya, ys, w, res, g, bm=1024, bn=1024):
    m = ya.shape[0]
    k, n = w.shape
    return pl.pallas_call(
        _out_proj_kernel,
        grid=(m // bm, n // bn),
        in_specs=[pl.BlockSpec((bm, D_ATTN), lambda i, j: (i, 0)),
                  pl.BlockSpec((bm, D_SSM), lambda i, j: (i, 0)),
                  pl.BlockSpec((k, bn), lambda i, j: (0, j)),
                  pl.BlockSpec((bm, bn), lambda i, j: (i, j)),
                  pl.BlockSpec((1, bn), lambda i, j: (0, j))],
        out_specs=[pl.BlockSpec((bm, bn), lambda i, j: (i, j)),
                   pl.BlockSpec((bm, bn), lambda i, j: (i, j)),
                   pl.BlockSpec((bm, 1), lambda i, j: (i, 0))],
        out_shape=[jax.ShapeDtypeStruct((m, n), F32),
                   jax.ShapeDtypeStruct((m, n), BF16),
                   jax.ShapeDtypeStruct((m, 1), F32)],
        scratch_shapes=[pltpu.VMEM((bm, LANES), F32)],
        compiler_params=_params(("arbitrary", "arbitrary"), vmem=VMEM_LIMIT_HIGH),
        name="out_proj",
    )(ya, ys, w, res, g.reshape(1, n))


def _ffn_up_kernel(x_ref, r_ref, wg_ref, wu_ref, cw_ref, cb_ref, side_ref, o_ref, side_bf_ref, ubuf_ref, wbf_ref):
    t = pl.program_id(0)
    n_steps = pl.num_programs(0)
    n_row_blocks = (n_steps - 1) // N_FFN_TILES
    bm = x_ref.shape[0]
    ft = FFN_TILE
    planes_per_half = ft // LANES
    row_block_new = jnp.minimum(t, n_steps - 2) % n_row_blocks
    side_bf_ref[...] = side_ref[...].astype(side_bf_ref.dtype)

    @pl.when(t == 0)
    def _():
        ubuf_ref[...] = jnp.zeros_like(ubuf_ref)

    @pl.when((row_block_new == 0) & (t < n_steps - 1))
    def _():
        wbf_ref[0] = wg_ref[...].astype(BF16)
        wbf_ref[1] = wu_ref[...].astype(BF16)

    f_done = jnp.maximum(t - 1, 0) // n_row_blocks
    for q in range(planes_per_half):
        conv = []
        for half in range(2):
            plane = half * planes_per_half + q
            col = pl.multiple_of(half * D_FF + f_done * ft + q * LANES, LANES)
            acc = ubuf_ref[plane, SUBLANES - 2:SUBLANES - 2 + bm, :] * cw_ref[0:1, pl.ds(col, LANES)]
            acc = acc + ubuf_ref[plane, SUBLANES - 1:SUBLANES - 1 + bm, :] * cw_ref[1:2, pl.ds(col, LANES)]
            acc = acc + ubuf_ref[plane, SUBLANES:SUBLANES + bm, :] * cw_ref[2:3, pl.ds(col, LANES)]
            conv.append(acc + cb_ref[:, pl.ds(col, LANES)])
        gate, up = conv
        o_ref[:, q * LANES:(q + 1) * LANES] = (gate * jax.nn.sigmoid(gate) * up).astype(o_ref.dtype)
    last_rows = ubuf_ref[:, bm:bm + SUBLANES, :]
    ubuf_ref[:, 0:SUBLANES, :] = jnp.where(row_block_new == 0, jnp.zeros_like(last_rows), last_rows)

    x = x_ref[...]
    r = r_ref[...]
    for half in range(2):
        u = jnp.dot(x, wbf_ref[half], preferred_element_type=F32) * r
        for q in range(planes_per_half):
            ubuf_ref[half * planes_per_half + q, SUBLANES:SUBLANES + bm, :] = u[:, q * LANES:(q + 1) * LANES]


def _ffn_up(hg, r, w, cw, cb, side, bm=1024):
    m, k = hg.shape
    nf = N_FFN_TILES
    nrb = m // bm
    n_tiles = nrb * nf
    n_planes = 2 * FFN_TILE // LANES
    side_rows, side_cols = side.shape
    slab = side_rows // n_tiles
    assert slab * n_tiles == side_rows
    cur = lambda t: jnp.minimum(t, n_tiles - 1)
    done = lambda t: jnp.maximum(t - 1, 0)
    return pl.pallas_call(
        _ffn_up_kernel,
        grid=(n_tiles + 1,),
        in_specs=[pl.BlockSpec((bm, k), lambda t: (cur(t) % nrb, 0)),
                  pl.BlockSpec((bm, 1), lambda t: (cur(t) % nrb, 0)),
                  pl.BlockSpec((k, FFN_TILE), lambda t: (0, cur(t) // nrb)),
                  pl.BlockSpec((k, FFN_TILE), lambda t: (0, nf + cur(t) // nrb)),
                  pl.BlockSpec((FFN_CONV, 2 * D_FF), lambda t: (0, 0)),
                  pl.BlockSpec((1, 2 * D_FF), lambda t: (0, 0)),
                  pl.BlockSpec((slab, side_cols), lambda t: (cur(t), 0))],
        out_specs=[pl.BlockSpec((bm, FFN_TILE), lambda t: (done(t) % nrb, done(t) // nrb)),
                   pl.BlockSpec((slab, side_cols), lambda t: (cur(t), 0))],
        out_shape=[jax.ShapeDtypeStruct((m, D_FF), BF16),
                   jax.ShapeDtypeStruct((side_rows, side_cols), BF16)],
        scratch_shapes=[pltpu.VMEM((n_planes, SUBLANES + bm, LANES), F32),
                        pltpu.VMEM((2, k, FFN_TILE), BF16)],
        compiler_params=_params(("arbitrary",)),
        name="ffn_up",
    )(hg, r, w, w, cw, cb, side)


def _ffn_down_kernel(a_ref, w_ref, res_ref, side_ref, o_ref, obf_ref, side_bf_ref):
    h = res_ref[...] + jnp.dot(a_ref[...], w_ref[...], preferred_element_type=F32)
    o_ref[...] = h
    obf_ref[...] = h.astype(obf_ref.dtype)
    side_bf_ref[...] = side_ref[...].astype(side_bf_ref.dtype)


def _ffn_down(act, w, res, side, bm=512, bn=1024):
    m, k = act.shape
    n = w.shape[1]
    ni = m // bm
    side_rows, side_cols = side.shape
    slab = side_rows // ((n // bn) * ni)
    assert slab * (n // bn) * ni == side_rows
    return pl.pallas_call(
        _ffn_down_kernel,
        grid=(n // bn, m // bm),
        in_specs=[pl.BlockSpec((bm, k), lambda j, i: (i, 0)),
                  pl.BlockSpec((k, bn), lambda j, i: (0, j), pipeline_mode=pl.Buffered(1)),
                  pl.BlockSpec((bm, bn), lambda j, i: (i, j)),
                  pl.BlockSpec((slab, side_cols), lambda j, i: (j * ni + i, 0))],
        out_specs=[pl.BlockSpec((bm, bn), lambda j, i: (i, j)),
                   pl.BlockSpec((bm, bn), lambda j, i: (i, j)),
                   pl.BlockSpec((slab, side_cols), lambda j, i: (j * ni + i, 0))],
        out_shape=[jax.ShapeDtypeStruct((m, n), F32), jax.ShapeDtypeStruct((m, n), BF16),
                   jax.ShapeDtypeStruct((side_rows, side_cols), BF16)],
        compiler_params=_params(("arbitrary", "arbitrary"), vmem=VMEM_LIMIT_HIGH),
        name="ffn_down",
    )(act, w, res, side)


def _ple_kernel(hb_ref, wg_ref, bg_ref, p_ref, wp_ref, h_ref, gf_ref, o_ref, ss_ref):
    j = pl.program_id(1)
    nj = pl.num_programs(1)
    bn = wg_ref.shape[1]

    @pl.when(j == 0)
    def _():
        ss_ref[...] = jnp.zeros_like(ss_ref)

    gate = jax.nn.sigmoid(jnp.dot(hb_ref[...], wg_ref[...], preferred_element_type=F32) + bg_ref[...])
    emb = jnp.dot(p_ref[...].astype(BF16), wp_ref[...], preferred_element_type=F32)
    h = h_ref[...] + gate * emb
    sq = h * h
    part = sq[:, 0:LANES]
    for c in range(1, bn // LANES):
        part = part + sq[:, c * LANES:(c + 1) * LANES]
    ss_ref[...] += part
    col = pl.multiple_of(j * bn, bn)
    o_ref[:, pl.ds(col, bn)] = h

    @pl.when(j == nj - 1)
    def _():
        ms = jnp.sum(ss_ref[...], axis=-1, keepdims=True) * (1.0 / D_MODEL)
        o_ref[...] = o_ref[...] * lax.rsqrt(ms + EPS) * gf_ref[...]


def _ple(hb, wg, bg, p, wp, h, gf, bm=512, bn=1024):
    m, d = h.shape
    return pl.pallas_call(
        _ple_kernel,
        grid=(m // bm, d // bn),
        in_specs=[pl.BlockSpec((bm, d), lambda i, j: (i, 0)),
                  pl.BlockSpec((d, bn), lambda i, j: (0, j)),
                  pl.BlockSpec((1, bn), lambda i, j: (0, j)),
                  pl.BlockSpec((bm, PLE_DIM), lambda i, j: (i, 0)),
                  pl.BlockSpec((PLE_DIM, bn), lambda i, j: (0, j)),
                  pl.BlockSpec((bm, bn), lambda i, j: (i, j)),
                  pl.BlockSpec((1, d), lambda i, j: (0, 0))],
        out_specs=pl.BlockSpec((bm, d), lambda i, j: (i, 0)),
        out_shape=jax.ShapeDtypeStruct((m, d), F32),
        scratch_shapes=[pltpu.VMEM((bm, LANES), F32)],
        compiler_params=_params(("arbitrary", "arbitrary"), vmem=VMEM_LIMIT_HIGH),
        name="ple",
    )(hb, wg, bg, p, wp, h, gf)


def kernel(x, p, positions, norm_mix_g, w_in, attn_sinks, ssm_conv_w, ssm_conv_b, ssm_dt_bias,
           ssm_a_log, ssm_d, ssm_norm_g, w_o, norm_ffn_g, w_up, ffn_conv_w, ffn_conv_b, w_down,
           ple_gate_w, ple_gate_b, ple_proj, norm_final_g):
    b, s, d = x.shape
    assert (b, s, d) == (1, SEQ, D_MODEL) and w_in.shape[0] == 1
    h = x.reshape(s, d)
    pos = positions.reshape(s, 1)
    half = HEAD_DIM // 2
    inv_freq = ROPE_THETA ** (-jnp.arange(half, dtype=F32) / half)
    freq = jnp.tile(inv_freq, LANES // half).reshape(1, LANES)
    w_dt = lax.slice(w_in, (0, 0, DT_COL0), (1, d, DT_COL0 + N_SSM_HEADS)).reshape(d, N_SSM_HEADS)
    w_dt = jnp.pad(w_dt, ((0, 0), (0, LANES - N_SSM_HEADS)))

    hn = _norm_cast(h, norm_mix_g[0])
    w_qkv, w_z, w_xbc = _split_w_in(jnp.swapaxes(w_in, 1, 2))
    qkv, dt = _proj(hn, w_qkv, QKV_TILE, "in_proj_qkv", w_dt=w_dt.astype(BF16))
    z, = _proj(hn, w_z, SSM_TILE, "in_proj_z")
    xbc, w_o_bf = _proj(hn, w_xbc, SSM_TILE, "in_proj_xbc", side=w_o[0])
    y_attn = _swa(qkv, pos, freq, attn_sinks[0])
    y_ssm = _ssd(xbc, z, dt, ssm_conv_w[0], ssm_conv_b[0], ssm_dt_bias[0], ssm_a_log[0], ssm_d[0],
                 ssm_norm_g[0])
    h1, hg, r = _out_proj(y_attn, y_ssm, w_o_bf, h, norm_ffn_g[0])
    act, w_down_bf = _ffn_up(hg, r, w_up[0], ffn_conv_w[0], ffn_conv_b[0].reshape(1, 2 * D_FF), w_down[0])
    h2, h2b, w_gate_bf = _ffn_down(act, w_down_bf, h1, ple_gate_w[0])
    out = _ple(h2b, w_gate_bf, ple_gate_b[0].reshape(1, d), p[0].reshape(s, PLE_DIM),
               ple_proj[0].astype(BF16), h2, norm_final_g.reshape(1, d))
    return out.reshape(b, s, d)
```

```python
import jax
import jax.numpy as jnp
from jax import lax
from jax.experimental import pallas as pl
from jax.experimental.pallas import tpu as pltpu

F32 = jnp.float32
BF16 = jnp.bfloat16

D_MODEL = 4096
SEQ = 16384
PLE_DIM = 256
D_ATTN = 2048
D_SSM = 2048
HEAD_DIM = 64
N_Q_HEADS = 32
N_KV_HEADS = 4
Q_PER_KV = 8
WINDOW = 128
ROPE_THETA = 10000.0
N_SSM_HEADS = 32
SSM_STATE = 128
SSM_GROUPS = 8
HEADS_PER_GROUP = 4
SSM_CONV = 4
CHUNK = 128
D_FF = 11008
FFN_CONV = 3
EPS = 1e-6
KV_COLS = N_KV_HEADS * HEAD_DIM
QKV_COLS = D_ATTN + 2 * KV_COLS
XBC_COLS = D_SSM + 2 * SSM_GROUPS * SSM_STATE
DT_COL0 = QKV_COLS + D_SSM + XBC_COLS

LANES = 128
SUBLANES = 8
VMEM_LIMIT = 56 * 1024 * 1024
VMEM_LIMIT_HIGH = 62 * 1024 * 1024

QKV_TILE = QKV_COLS // 2
SSM_TILE = 1024
FFN_TILE = 256
N_FFN_TILES = D_FF // FFN_TILE


def _params(sem, vmem=VMEM_LIMIT, flags=None):
    return pltpu.CompilerParams(dimension_semantics=sem, vmem_limit_bytes=vmem, flags=flags)


def _norm_cast_kernel(x_ref, g_ref, o_ref):
    x = x_ref[...]
    ms = jnp.mean(x * x, axis=-1, keepdims=True)
    o_ref[...] = (x * lax.rsqrt(ms + EPS) * g_ref[...]).astype(o_ref.dtype)


def _norm_cast(x, g, bm=512):
    m, d = x.shape
    return pl.pallas_call(
        _norm_cast_kernel,
        grid=(m // bm,),
        in_specs=[pl.BlockSpec((bm, d), lambda i: (i, 0)),
                  pl.BlockSpec((1, d), lambda i: (0, 0))],
        out_specs=pl.BlockSpec((bm, d), lambda i: (i, 0)),
        out_shape=jax.ShapeDtypeStruct((m, d), BF16),
        compiler_params=_params(("arbitrary",)),
        name="norm_cast",
    )(x, g.reshape(1, d))


W_IN_CAST_TILE = 512
W_IN_QKV_TILES = QKV_COLS // W_IN_CAST_TILE
W_IN_Z_TILES = D_SSM // W_IN_CAST_TILE
W_IN_XBC_TILES = XBC_COLS // W_IN_CAST_TILE


def _split_w_in_kernel(wt_ref, qkv_ref, z_ref, xbc_ref):
    j = pl.program_id(0)

    @pl.when(j < W_IN_QKV_TILES)
    def _():
        qkv_ref[...] = wt_ref[...].T.astype(qkv_ref.dtype)

    @pl.when((j >= W_IN_QKV_TILES) & (j < W_IN_QKV_TILES + W_IN_Z_TILES))
    def _():
        z_ref[...] = wt_ref[...].T.astype(z_ref.dtype)

    @pl.when(j >= W_IN_QKV_TILES + W_IN_Z_TILES)
    def _():
        xbc_ref[...] = wt_ref[...].T.astype(xbc_ref.dtype)


def _split_w_in(wt):
    k = wt.shape[2]
    bn = W_IN_CAST_TILE
    q, z = W_IN_QKV_TILES, W_IN_Z_TILES
    return pl.pallas_call(
        _split_w_in_kernel,
        grid=(q + z + W_IN_XBC_TILES,),
        in_specs=[pl.BlockSpec((None, bn, k), lambda j: (0, j, 0))],
        out_specs=[pl.BlockSpec((k, bn), lambda j: (0, jnp.minimum(j, q - 1))),
                   pl.BlockSpec((k, bn), lambda j: (0, jnp.clip(j - q, 0, z - 1))),
                   pl.BlockSpec((k, bn), lambda j: (0, jnp.clip(j - q - z, 0, W_IN_XBC_TILES - 1)))],
        out_shape=[jax.ShapeDtypeStruct((k, QKV_COLS), BF16),
                   jax.ShapeDtypeStruct((k, D_SSM), BF16),
                   jax.ShapeDtypeStruct((k, XBC_COLS), BF16)],
        compiler_params=_params(("arbitrary",)),
        name="split_w_in",
    )(wt)


def _proj_kernel(x_ref, w_ref, o_ref):
    o_ref[...] = jnp.dot(x_ref[...], w_ref[...], preferred_element_type=F32)


def _proj_dt_kernel(x_ref, w_ref, wdt_ref, o_ref, dt_ref):
    o_ref[...] = jnp.dot(x_ref[...], w_ref[...], preferred_element_type=F32)

    @pl.when(pl.program_id(1) == 0)
    def _():
        dt_ref[...] = jnp.dot(x_ref[...], wdt_ref[...], preferred_element_type=F32)


def _proj_side_kernel(x_ref, w_ref, side_ref, o_ref, side_bf_ref):
    o_ref[...] = jnp.dot(x_ref[...], w_ref[...], preferred_element_type=F32)
    side_bf_ref[...] = side_ref[...].astype(side_bf_ref.dtype)


def _proj(hn, w, bn, name, w_dt=None, side=None, bm=1024):
    m, k = hn.shape
    n = w.shape[1]
    nj = n // bn
    in_specs = [pl.BlockSpec((bm, k), lambda i, j: (i, 0)),
                pl.BlockSpec((k, bn), lambda i, j: (0, j))]
    out_specs = [pl.BlockSpec((bm, bn), lambda i, j: (i, j))]
    out_shape = [jax.ShapeDtypeStruct((m, n), F32)]
    args = [hn, w]
    body = _proj_kernel
    if w_dt is not None:
        in_specs.append(pl.BlockSpec((k, LANES), lambda i, j: (0, 0)))
        out_specs.append(pl.BlockSpec((bm, LANES), lambda i, j: (i, 0)))
        out_shape.append(jax.ShapeDtypeStruct((m, LANES), F32))
        args.append(w_dt)
        body = _proj_dt_kernel
    if side is not None:
        rows, cols = side.shape
        slab = rows // ((m // bm) * nj)
        assert slab * (m // bm) * nj == rows and w_dt is None
        in_specs.append(pl.BlockSpec((slab, cols), lambda i, j: (i * nj + j, 0)))
        out_specs.append(pl.BlockSpec((slab, cols), lambda i, j: (i * nj + j, 0)))
        out_shape.append(jax.ShapeDtypeStruct((rows, cols), BF16))
        args.append(side)
        body = _proj_side_kernel
    return pl.pallas_call(
        body,
        grid=(m // bm, nj),
        in_specs=in_specs,
        out_specs=out_specs,
        out_shape=out_shape,
        compiler_params=_params(("arbitrary", "arbitrary")),
        name=name,
    )(*args)


def _rope_tile(x, cos, sin_signed, first_half):
    rot = jnp.where(first_half, pltpu.roll(x, LANES - HEAD_DIM // 2, 1),
                    pltpu.roll(x, HEAD_DIM // 2, 1))
    return x * cos + rot * sin_signed


LOG2E = 1.4426950408889634
VT_ROWS = 80


def _swa_kernel(pos_ref, freq_ref, sink_ref, q_ref, k_ref, v_ref, o_ref, kband_ref, vt_ref):
    n = pl.program_id(0)
    w = WINDOW

    lane = lax.broadcasted_iota(jnp.int32, (1, LANES), 1)
    low_half = lane < HEAD_DIM
    first_half = (lane % HEAD_DIM) < (HEAD_DIM // 2)

    @pl.when(n == 0)
    def _():
        kband_ref[...] = jnp.zeros_like(kband_ref)
        vt_ref[...] = jnp.zeros_like(vt_ref)
        ones_row = jnp.where(lax.broadcasted_iota(jnp.int32, (VT_ROWS - HEAD_DIM, 2 * w), 0) == 0, 1.0, 0.0)
        for kh in range(N_KV_HEADS):
            vt_ref[kh, HEAD_DIM:VT_ROWS, :] = ones_row.astype(BF16)

    ang = pos_ref[...].astype(F32) * freq_ref[...]
    cos = jnp.cos(ang)
    sin_signed = jnp.where(first_half, -jnp.sin(ang), jnp.sin(ang))

    for kh in range(N_KV_HEADS):
        kband_ref[kh, 0:w, :] = kband_ref[kh, w:2 * w, :]
        vt_ref[kh, 0:HEAD_DIM, 0:w] = vt_ref[kh, 0:HEAD_DIM, w:2 * w]
    for t in range(KV_COLS // LANES):
        k_rot = _rope_tile(k_ref[:, t * LANES:(t + 1) * LANES], cos, sin_signed, first_half)
        k_sw = pltpu.roll(k_rot, HEAD_DIM, 1)
        kband_ref[2 * t, w:2 * w, :] = jnp.where(low_half, k_rot, k_sw).astype(BF16)
        kband_ref[2 * t + 1, w:2 * w, :] = jnp.where(low_half, k_sw, k_rot).astype(BF16)
        v_t = v_ref[:, t * LANES:(t + 1) * LANES].T
        vt_ref[2 * t, 0:HEAD_DIM, w:2 * w] = v_t[0:HEAD_DIM, :].astype(BF16)
        vt_ref[2 * t + 1, 0:HEAD_DIM, w:2 * w] = v_t[HEAD_DIM:2 * HEAD_DIM, :].astype(BF16)

    ki = lax.broadcasted_iota(jnp.int32, (2 * w, w), 0)
    qi = lax.broadcasted_iota(jnp.int32, (2 * w, w), 1)
    prev_off = jnp.where(n > 0, 0, 2 * w)
    valid = ((ki < w) & (ki > qi + prev_off)) | ((ki >= w) & (ki - w <= qi))
    qscale = HEAD_DIM ** -0.5 * LOG2E
    pairs = Q_PER_KV // 2
    eye = (lax.broadcasted_iota(jnp.int32, (LANES, LANES), 0)
           == lax.broadcasted_iota(jnp.int32, (LANES, LANES), 1)).astype(BF16)

    scores = []
    for kh in range(N_KV_HEADS):
        lhs = []
        for j in range(pairs):
            t = kh * pairs + j
            q_rot = _rope_tile(q_ref[:, t * LANES:(t + 1) * LANES], cos, sin_signed, first_half) * qscale
            lhs.append(jnp.where(low_half, q_rot, 0.0).astype(BF16))
            lhs.append(jnp.where(low_half, 0.0, q_rot).astype(BF16))
        scores.append(lax.dot_general(kband_ref[kh], jnp.concatenate(lhs, axis=0), (((1,), (1,)), ((), ())),
                                      preferred_element_type=F32))
    outs_t = []
    sink_terms_all = []
    for kh in range(N_KV_HEADS):
        st = scores[kh]
        probs = []
        sink_terms = []
        for hq in range(Q_PER_KV):
            s = jnp.where(valid, st[:, hq * w:(hq + 1) * w], -jnp.inf)
            sink = sink_ref[kh * Q_PER_KV + hq] * LOG2E
            m = jnp.maximum(jnp.max(s, axis=0, keepdims=True), sink)
            probs.append(jnp.exp2(s - m).astype(BF16))
            sink_terms.append(jnp.exp2(sink - m))
        outs_t.append(jnp.dot(vt_ref[kh], jnp.concatenate(probs, axis=1), preferred_element_type=F32))
        sink_terms_all.append(sink_terms)
    for kh in range(N_KV_HEADS):
        ot = outs_t[kh]
        sink_terms = sink_terms_all[kh]
        for j in range(pairs):
            halves = []
            for r in range(2):
                hq = 2 * j + r
                cols = slice(hq * w, (hq + 1) * w)
                denom = ot[HEAD_DIM:HEAD_DIM + 1, cols] + sink_terms[hq]
                halves.append(ot[0:HEAD_DIM, cols] / denom)
            pair_t = jnp.concatenate(halves, axis=0).astype(BF16)
            out = lax.dot_general(eye, pair_t, (((1,), (1,)), ((), ())), preferred_element_type=F32)
            t = kh * pairs + j
            o_ref[:, t * LANES:(t + 1) * LANES] = out.astype(o_ref.dtype)


def _swa(qkv, pos, freq, sinks):
    s = qkv.shape[0]
    w = WINDOW
    return pl.pallas_call(
        _swa_kernel,
        grid=(s // w,),
        in_specs=[pl.BlockSpec((w, 1), lambda n: (n, 0)),
                  pl.BlockSpec((1, LANES), lambda n: (0, 0)),
                  pl.BlockSpec(memory_space=pltpu.SMEM),
                  pl.BlockSpec((w, D_ATTN), lambda n: (n, 0)),
                  pl.BlockSpec((w, KV_COLS), lambda n: (n, D_ATTN // KV_COLS)),
                  pl.BlockSpec((w, KV_COLS), lambda n: (n, D_ATTN // KV_COLS + 1))],
        out_specs=pl.BlockSpec((w, D_ATTN), lambda n: (n, 0)),
        out_shape=jax.ShapeDtypeStruct((s, D_ATTN), BF16),
        scratch_shapes=[pltpu.VMEM((N_KV_HEADS, 2 * w, LANES), BF16),
                        pltpu.VMEM((N_KV_HEADS, VT_ROWS, 2 * w), BF16)],
        compiler_params=_params(("arbitrary",)),
        name="swa",
    )(pos, freq, sinks, qkv, qkv, qkv)


def _split3(v):
    hi = v.astype(BF16)
    r = v - hi.astype(F32)
    mid = r.astype(BF16)
    lo = (r - mid.astype(F32)).astype(BF16)
    return hi, mid, lo


def _dot_exact_rhs(a, b_parts):
    acc = jnp.dot(a, b_parts[0], preferred_element_type=F32)
    for part in b_parts[1:]:
        acc = acc + jnp.dot(a, part, preferred_element_type=F32)
    return acc


def _dot_exact_lhs(a_parts, b):
    acc = jnp.dot(a_parts[0], b, preferred_element_type=F32)
    for part in a_parts[1:]:
        acc = acc + jnp.dot(part, b, preferred_element_type=F32)
    return acc


def _ssd_kernel(xbc_ref, z_ref, dt_ref, cw_ref, cb_ref, dtb_ref, alog_ref, dexp_ref,
                ng_ref, tri_ref, expand_ref, o_ref, xpad_ref, xc_ref, y_ref, state_ref):
    c = pl.program_id(0)
    L = CHUNK
    G, R, P, N = SSM_GROUPS, HEADS_PER_GROUP, HEAD_DIM, SSM_STATE
    GP = R * P

    @pl.when(c == 0)
    def _():
        state_ref[...] = jnp.zeros_like(state_ref)
        xpad_ref[:, L:L + SUBLANES, :] = jnp.zeros((XBC_COLS // LANES, SUBLANES, LANES), F32)

    xpad_ref[:, 0:SUBLANES, :] = xpad_ref[:, L:L + SUBLANES, :]

    for p in range(XBC_COLS // LANES):
        cs = slice(p * LANES, (p + 1) * LANES)
        xpad_ref[p, SUBLANES:SUBLANES + L, :] = xbc_ref[:, cs]
        acc = xpad_ref[p, SUBLANES - 3:SUBLANES - 3 + L, :] * cw_ref[0:1, cs]
        for k in range(1, SSM_CONV):
            acc = acc + xpad_ref[p, SUBLANES - 3 + k:SUBLANES - 3 + k + L, :] * cw_ref[k:k + 1, cs]
        acc = acc + cb_ref[:, cs]
        xc_ref[:, cs] = acc * jax.nn.sigmoid(acc)

    dt_raw = dt_ref[...] + dtb_ref[...]
    dt = jnp.maximum(dt_raw, 0.0) + jnp.log1p(jnp.exp(-jnp.abs(dt_raw)))
    a = -jnp.exp(alog_ref[...]) * LOG2E
    da = dt * a
    a_cs = _dot_exact_rhs(tri_ref[...], _split3(da))
    a_cs_t = a_cs.T
    expand = expand_ref[...]
    acs_x = _dot_exact_lhs(_split3(a_cs), expand)
    dt_x = _dot_exact_lhs(_split3(dt), expand)

    li = lax.broadcasted_iota(jnp.int32, (L, L), 0)
    si = lax.broadcasted_iota(jnp.int32, (L, L), 1)
    causal = li >= si
    lane = lax.broadcasted_iota(jnp.int32, (1, LANES), 1)
    low_half = lane < P

    groups = []
    for g in range(G):
        xs = slice(g * GP, (g + 1) * GP)
        x_g = xc_ref[:, xs]
        b_g = xc_ref[:, D_SSM + g * N:D_SSM + (g + 1) * N].astype(BF16)
        c_g = xc_ref[:, D_SSM + G * N + g * N:D_SSM + G * N + (g + 1) * N].astype(BF16)
        acs_g = acs_x[:, xs]
        xdt_g = x_g * dt_x[:, xs]
        cb = lax.dot_general(c_g, b_g, (((1,), (1,)), ((), ())), preferred_element_type=F32)
        a_last = acs_g[L - 1:L, :]
        xdte = (xdt_g * jnp.exp2(a_last - acs_g)).astype(BF16)
        new_state = lax.dot_general(b_g, xdte, (((0,), (0,)), ((), ())),
                                    preferred_element_type=F32)
        prev = state_ref[g]
        y_off = jnp.dot(c_g, prev.astype(BF16), preferred_element_type=F32) * jnp.exp2(acs_g)
        state_ref[g] = prev * jnp.exp2(a_last) + new_state
        groups.append((xs, x_g, xdt_g, cb, y_off))

    for g in range(G):
        xs, x_g, xdt_g, cb, y_off = groups[g]
        y_pairs = []
        for j in range(R // 2):
            ws = []
            for r in range(2):
                h = g * R + 2 * j + r
                seg = a_cs[:, h:h + 1] - a_cs_t[h:h + 1, :]
                decay = jnp.exp2(jnp.where(causal, seg, -jnp.inf))
                ws.append((cb * decay).astype(BF16))
            wcat = jnp.concatenate(ws, axis=1)
            xp = xdt_g[:, j * LANES:(j + 1) * LANES]
            rhs = jnp.concatenate([jnp.where(low_half, xp, 0.0), jnp.where(low_half, 0.0, xp)],
                                  axis=0).astype(BF16)
            y_pairs.append(jnp.dot(wcat, rhs, preferred_element_type=F32))
        y_diag = jnp.concatenate(y_pairs, axis=1)
        y_ref[:, xs] = y_diag + y_off + x_g * dexp_ref[:, xs]

    z = z_ref[...]
    y = y_ref[...] * (z * jax.nn.sigmoid(z))
    ms = jnp.mean(y * y, axis=-1, keepdims=True)
    o_ref[...] = (y * lax.rsqrt(ms + EPS) * ng_ref[...]).astype(o_ref.dtype)


def _ssd(xbc, z, dt, conv_w, conv_b, dt_bias, a_log, d_skip, norm_g):
    s = xbc.shape[0]
    L = CHUNK
    pad = LANES - N_SSM_HEADS
    dtb = jnp.pad(dt_bias, (0, pad)).reshape(1, LANES)
    alog = jnp.pad(a_log, (0, pad)).reshape(1, LANES)
    dexp = jnp.repeat(d_skip, HEAD_DIM).reshape(1, D_SSM)
    tri = (jnp.arange(L)[:, None] >= jnp.arange(L)[None, :]).astype(BF16)
    expand = (jnp.arange(LANES)[:, None] == (jnp.arange(D_SSM)[None, :] // HEAD_DIM)).astype(BF16)
    const = lambda c: (0, 0)
    return pl.pallas_call(
        _ssd_kernel,
        grid=(s // L,),
        in_specs=[pl.BlockSpec((L, XBC_COLS), lambda c: (c, 0)),
                  pl.BlockSpec((L, D_SSM), lambda c: (c, 0)),
                  pl.BlockSpec((L, LANES), lambda c: (c, 0)),
                  pl.BlockSpec((SSM_CONV, XBC_COLS), const),
                  pl.BlockSpec((1, XBC_COLS), const),
                  pl.BlockSpec((1, LANES), const),
                  pl.BlockSpec((1, LANES), const),
                  pl.BlockSpec((1, D_SSM), const),
                  pl.BlockSpec((1, D_SSM), const),
                  pl.BlockSpec((L, L), const),
                  pl.BlockSpec((LANES, D_SSM), const)],
        out_specs=pl.BlockSpec((L, D_SSM), lambda c: (c, 0)),
        out_shape=jax.ShapeDtypeStruct((s, D_SSM), BF16),
        scratch_shapes=[pltpu.VMEM((XBC_COLS // LANES, SUBLANES + L, LANES), F32),
                        pltpu.VMEM((L, XBC_COLS), F32),
                        pltpu.VMEM((L, D_SSM), F32),
                        pltpu.VMEM((SSM_GROUPS, SSM_STATE, HEADS_PER_GROUP * HEAD_DIM), F32)],
        compiler_params=_params(("arbitrary",)),
        name="ssd",
    )(xbc, z, dt, conv_w, conv_b.reshape(1, XBC_COLS), dtb, alog, dexp,
      norm_g.reshape(1, D_SSM), tri, expand)


def _out_proj_kernel(ya_ref, ys_ref, w_ref, res_ref, g_ref, o_ref, hg_ref, r_ref, ss_ref):
    j = pl.program_id(1)
    bn = w_ref.shape[1]

    @pl.when(j == 0)
    def _():
        ss_ref[...] = jnp.zeros_like(ss_ref)

    acc = jnp.dot(ya_ref[...], w_ref[0:D_ATTN, :], preferred_element_type=F32)
    acc = acc + jnp.dot(ys_ref[...], w_ref[D_ATTN:D_ATTN + D_SSM, :], preferred_element_type=F32)
    h = res_ref[...] + acc
    o_ref[...] = h
    hg_ref[...] = (h * g_ref[...]).astype(hg_ref.dtype)
    sq = h * h
    part = sq[:, 0:LANES]
    for c in range(1, bn // LANES):
        part = part + sq[:, c * LANES:(c + 1) * LANES]
    ss_ref[...] += part

    @pl.when(j == pl.num_programs(1) - 1)
    def _():
        ms = jnp.sum(ss_ref[...], axis=-1, keepdims=True) * (1.0 / D_MODEL)
        r_ref[...] = lax.rsqrt(ms + EPS)


def _out_proj(ya, ys, w, res, g, bm=1024, bn=1024):
    m = ya.shape[0]
    k, n = w.shape
    return pl.pallas_call(
        _out_proj_kernel,
        grid=(m // bm, n // bn),
        in_specs=[pl.BlockSpec((bm, D_ATTN), lambda i, j: (i, 0)),
                  pl.BlockSpec((bm, D_SSM), lambda i, j: (i, 0)),
                  pl.BlockSpec((k, bn), lambda i, j: (0, j)),
                  pl.BlockSpec((bm, bn), lambda i, j: (i, j)),
                  pl.BlockSpec((1, bn), lambda i, j: (0, j))],
        out_specs=[pl.BlockSpec((bm, bn), lambda i, j: (i, j)),
                   pl.BlockSpec((bm, bn), lambda i, j: (i, j)),
                   pl.BlockSpec((bm, 1), lambda i, j: (i, 0))],
        out_shape=[jax.ShapeDtypeStruct((m, n), F32),
                   jax.ShapeDtypeStruct((m, n), BF16),
                   jax.ShapeDtypeStruct((m, 1), F32)],
        scratch_shapes=[pltpu.VMEM((bm, LANES), F32)],
        compiler_params=_params(("arbitrary", "arbitrary"), vmem=VMEM_LIMIT_HIGH),
        name="out_proj",
    )(ya, ys, w, res, g.reshape(1, n))


FFN_ROW_SPLIT = 8


def _ffn_up_kernel(x_ref, r_ref, wg_ref, wu_ref, cw_ref, cb_ref, side_ref, o_ref, side_bf_ref, ubuf_ref, wbf_ref):
    t = pl.program_id(0)
    n_steps = pl.num_programs(0)
    n_row_blocks = (n_steps - 1) // N_FFN_TILES
    bm = x_ref.shape[0]
    ft = FFN_TILE
    planes_per_half = ft // LANES
    row_block_new = jnp.minimum(t, n_steps - 2) % n_row_blocks
    side_bf_ref[...] = side_ref[...].astype(side_bf_ref.dtype)

    @pl.when(t == 0)
    def _():
        ubuf_ref[...] = jnp.zeros_like(ubuf_ref)

    @pl.when((row_block_new == 0) & (t < n_steps - 1))
    def _():
        wbf_ref[0] = wg_ref[...].astype(BF16)
        wbf_ref[1] = wu_ref[...].astype(BF16)

    f_done = jnp.maximum(t - 1, 0) // n_row_blocks
    for q in range(planes_per_half):
        conv = []
        for half in range(2):
            plane = half * planes_per_half + q
            col = pl.multiple_of(half * D_FF + f_done * ft + q * LANES, LANES)
            acc = ubuf_ref[plane, SUBLANES - 2:SUBLANES - 2 + bm, :] * cw_ref[0:1, pl.ds(col, LANES)]
            acc = acc + ubuf_ref[plane, SUBLANES - 1:SUBLANES - 1 + bm, :] * cw_ref[1:2, pl.ds(col, LANES)]
            acc = acc + ubuf_ref[plane, SUBLANES:SUBLANES + bm, :] * cw_ref[2:3, pl.ds(col, LANES)]
            conv.append(acc + cb_ref[:, pl.ds(col, LANES)])
        gate, up = conv
        o_ref[:, q * LANES:(q + 1) * LANES] = (gate * jax.nn.sigmoid(gate) * up).astype(o_ref.dtype)
    last_rows = ubuf_ref[:, bm:bm + SUBLANES, :]
    ubuf_ref[:, 0:SUBLANES, :] = jnp.where(row_block_new == 0, jnp.zeros_like(last_rows), last_rows)

    hb = bm // FFN_ROW_SPLIT
    for sb in range(FFN_ROW_SPLIT):
        rows = slice(sb * hb, (sb + 1) * hb)
        x = x_ref[rows, :]
        r = r_ref[rows, :]
        for half in range(2):
            u = jnp.dot(x, wbf_ref[half], preferred_element_type=F32) * r
            for q in range(planes_per_half):
                ubuf_ref[half * planes_per_half + q, SUBLANES + sb * hb:SUBLANES + (sb + 1) * hb, :] = \
                    u[:, q * LANES:(q + 1) * LANES]


def _ffn_up(hg, r, w, cw, cb, side, bm=1024):
    m, k = hg.shape
    nf = N_FFN_TILES
    nrb = m // bm
    n_tiles = nrb * nf
    n_planes = 2 * FFN_TILE // LANES
    side_rows, side_cols = side.shape
    slab = side_rows // n_tiles
    assert slab * n_tiles == side_rows
    cur = lambda t: jnp.minimum(t, n_tiles - 1)
    done = lambda t: jnp.maximum(t - 1, 0)
    return pl.pallas_call(
        _ffn_up_kernel,
        grid=(n_tiles + 1,),
        in_specs=[pl.BlockSpec((bm, k), lambda t: (cur(t) % nrb, 0)),
                  pl.BlockSpec((bm, 1), lambda t: (cur(t) % nrb, 0)),
                  pl.BlockSpec((k, FFN_TILE), lambda t: (0, cur(t) // nrb)),
                  pl.BlockSpec((k, FFN_TILE), lambda t: (0, nf + cur(t) // nrb)),
                  pl.BlockSpec((FFN_CONV, 2 * D_FF), lambda t: (0, 0)),
                  pl.BlockSpec((1, 2 * D_FF), lambda t: (0, 0)),
                  pl.BlockSpec((slab, side_cols), lambda t: (cur(t), 0))],
        out_specs=[pl.BlockSpec((bm, FFN_TILE), lambda t: (done(t) % nrb, done(t) // nrb)),
                   pl.BlockSpec((slab, side_cols), lambda t: (cur(t), 0))],
        out_shape=[jax.ShapeDtypeStruct((m, D_FF), BF16),
                   jax.ShapeDtypeStruct((side_rows, side_cols), BF16)],
        scratch_shapes=[pltpu.VMEM((n_planes, SUBLANES + bm, LANES), F32),
                        pltpu.VMEM((2, k, FFN_TILE), BF16)],
        compiler_params=_params(("arbitrary",)),
        name="ffn_up",
    )(hg, r, w, w, cw, cb, side)


def _ffn_down_kernel(a_ref, w_ref, res_ref, side_ref, o_ref, obf_ref, side_bf_ref):
    h = res_ref[...] + jnp.dot(a_ref[...], w_ref[...], preferred_element_type=F32)
    o_ref[...] = h
    obf_ref[...] = h.astype(obf_ref.dtype)
    side_bf_ref[...] = side_ref[...].astype(side_bf_ref.dtype)


def _ffn_down(act, w, res, side, bm=512, bn=1024):
    m, k = act.shape
    n = w.shape[1]
    ni = m // bm
    side_rows, side_cols = side.shape
    slab = side_rows // ((n // bn) * ni)
    assert slab * (n // bn) * ni == side_rows
    return pl.pallas_call(
        _ffn_down_kernel,
        grid=(n // bn, m // bm),
        in_specs=[pl.BlockSpec((bm, k), lambda j, i: (i, 0)),
                  pl.BlockSpec((k, bn), lambda j, i: (0, j), pipeline_mode=pl.Buffered(1)),
                  pl.BlockSpec((bm, bn), lambda j, i: (i, j)),
                  pl.BlockSpec((slab, side_cols), lambda j, i: (j * ni + i, 0))],
        out_specs=[pl.BlockSpec((bm, bn), lambda j, i: (i, j)),
                   pl.BlockSpec((bm, bn), lambda j, i: (i, j)),
                   pl.BlockSpec((slab, side_cols), lambda j, i: (j * ni + i, 0))],
        out_shape=[jax.ShapeDtypeStruct((m, n), F32), jax.ShapeDtypeStruct((m, n), BF16),
                   jax.ShapeDtypeStruct((side_rows, side_cols), BF16)],
        compiler_params=_params(("arbitrary", "arbitrary"), vmem=VMEM_LIMIT_HIGH),
        name="ffn_down",
    )(act, w, res, side)


def _ple_kernel(hb_ref, wg_ref, bg_ref, p_ref, wp_ref, h_ref, gf_ref, o_ref, ss_ref):
    j = pl.program_id(1)
    nj = pl.num_programs(1)
    bn = wg_ref.shape[1]

    @pl.when(j == 0)
    def _():
        ss_ref[...] = jnp.zeros_like(ss_ref)

    gate = jax.nn.sigmoid(jnp.dot(hb_ref[...], wg_ref[...], preferred_element_type=F32) + bg_ref[...])
    emb = jnp.dot(p_ref[...].astype(BF16), wp_ref[...], preferred_element_type=F32)
    h = h_ref[...] + gate * emb
    sq = h * h
    part = sq[:, 0:LANES]
    for c in range(1, bn // LANES):
        part = part + sq[:, c * LANES:(c + 1) * LANES]
    ss_ref[...] += part
    col = pl.multiple_of(j * bn, bn)
    o_ref[:, pl.ds(col, bn)] = h

    @pl.when(j == nj - 1)
    def _():
        ms = jnp.sum(ss_ref[...], axis=-1, keepdims=True) * (1.0 / D_MODEL)
        o_ref[...] = o_ref[...] * lax.rsqrt(ms + EPS) * gf_ref[...]


def _ple(hb, wg, bg, p, wp, h, gf, bm=512, bn=1024):
    m, d = h.shape
    return pl.pallas_call(
        _ple_kernel,
        grid=(m // bm, d // bn),
        in_specs=[pl.BlockSpec((bm, d), lambda i, j: (i, 0)),
                  pl.BlockSpec((d, bn), lambda i, j: (0, j)),
                  pl.BlockSpec((1, bn), lambda i, j: (0, j)),
                  pl.BlockSpec((bm, PLE_DIM), lambda i, j: (i, 0)),
                  pl.BlockSpec((PLE_DIM, bn), lambda i, j: (0, j)),
                  pl.BlockSpec((bm, bn), lambda i, j: (i, j)),
                  pl.BlockSpec((1, d), lambda i, j: (0, 0))],
        out_specs=pl.BlockSpec((bm, d), lambda i, j: (i, 0)),
        out_shape=jax.ShapeDtypeStruct((m, d), F32),
        scratch_shapes=[pltpu.VMEM((bm, LANES), F32)],
        compiler_params=_params(("arbitrary", "arbitrary"), vmem=VMEM_LIMIT_HIGH),
        name="ple",
    )(hb, wg, bg, p, wp, h, gf)


def kernel(x, p, positions, norm_mix_g, w_in, attn_sinks, ssm_conv_w, ssm_conv_b, ssm_dt_bias,
           ssm_a_log, ssm_d, ssm_norm_g, w_o, norm_ffn_g, w_up, ffn_conv_w, ffn_conv_b, w_down,
           ple_gate_w, ple_gate_b, ple_proj, norm_final_g):
    b, s, d = x.shape
    assert (b, s, d) == (1, SEQ, D_MODEL) and w_in.shape[0] == 1
    h = x.reshape(s, d)
    pos = positions.reshape(s, 1)
    half = HEAD_DIM // 2
    inv_freq = ROPE_THETA ** (-jnp.arange(half, dtype=F32) / half)
    freq = jnp.tile(inv_freq, LANES // half).reshape(1, LANES)
    w_dt = lax.slice(w_in, (0, 0, DT_COL0), (1, d, DT_COL0 + N_SSM_HEADS)).reshape(d, N_SSM_HEADS)
    w_dt = jnp.pad(w_dt, ((0, 0), (0, LANES - N_SSM_HEADS)))

    hn = _norm_cast(h, norm_mix_g[0])
    w_qkv, w_z, w_xbc = _split_w_in(jnp.swapaxes(w_in, 1, 2))
    qkv, dt = _proj(hn, w_qkv, QKV_TILE, "in_proj_qkv", w_dt=w_dt.astype(BF16))
    z, = _proj(hn, w_z, SSM_TILE, "in_proj_z")
    xbc, w_o_bf = _proj(hn, w_xbc, SSM_TILE, "in_proj_xbc", side=w_o[0])
    y_attn = _swa(qkv, pos, freq, attn_sinks[0])
    y_ssm = _ssd(xbc, z, dt, ssm_conv_w[0], ssm_conv_b[0], ssm_dt_bias[0], ssm_a_log[0], ssm_d[0],
                 ssm_norm_g[0])
    h1, hg, r = _out_proj(y_attn, y_ssm, w_o_bf, h, norm_ffn_g[0])
    act, w_down_bf = _ffn_up(hg, r, w_up[0], ffn_conv_w[0], ffn_conv_b[0].reshape(1, 2 * D_FF), w_down[0])
    h2, h2b, w_gate_bf = _ffn_down(act, w_down_bf, h1, ple_gate_w[0])
    out = _ple(h2b, w_gate_bf, ple_gate_b[0].reshape(1, d), p[0].reshape(s, PLE_DIM),
               ple_proj[0].astype(BF16), h2, norm_final_g.reshape(1, d))
    return out.reshape(b, s, d)
```

```python
import jax
import jax.numpy as jnp
from jax import lax
from jax.experimental import pallas as pl
from jax.experimental.pallas import tpu as pltpu

F32 = jnp.float32
BF16 = jnp.bfloat16

D_MODEL = 4096
SEQ = 16384
PLE_DIM = 256
D_ATTN = 2048
D_SSM = 2048
HEAD_DIM = 64
N_Q_HEADS = 32
N_KV_HEADS = 4
Q_PER_KV = 8
WINDOW = 128
ROPE_THETA = 10000.0
N_SSM_HEADS = 32
SSM_STATE = 128
SSM_GROUPS = 8
HEADS_PER_GROUP = 4
SSM_CONV = 4
CHUNK = 128
D_FF = 11008
FFN_CONV = 3
EPS = 1e-6
KV_COLS = N_KV_HEADS * HEAD_DIM
QKV_COLS = D_ATTN + 2 * KV_COLS
XBC_COLS = D_SSM + 2 * SSM_GROUPS * SSM_STATE
DT_COL0 = QKV_COLS + D_SSM + XBC_COLS

LANES = 128
SUBLANES = 8
VMEM_LIMIT = 56 * 1024 * 1024
VMEM_LIMIT_HIGH = 62 * 1024 * 1024

QKV_TILE = QKV_COLS // 2
SSM_TILE = 1024
FFN_TILE = 256
N_FFN_TILES = D_FF // FFN_TILE


def _params(sem, vmem=VMEM_LIMIT, flags=None):
    return pltpu.CompilerParams(dimension_semantics=sem, vmem_limit_bytes=vmem, flags=flags)


def _norm_cast_kernel(x_ref, g_ref, o_ref):
    x = x_ref[...]
    ms = jnp.mean(x * x, axis=-1, keepdims=True)
    o_ref[...] = (x * lax.rsqrt(ms + EPS) * g_ref[...]).astype(o_ref.dtype)


def _norm_cast(x, g, bm=512):
    m, d = x.shape
    return pl.pallas_call(
        _norm_cast_kernel,
        grid=(m // bm,),
        in_specs=[pl.BlockSpec((bm, d), lambda i: (i, 0)),
                  pl.BlockSpec((1, d), lambda i: (0, 0))],
        out_specs=pl.BlockSpec((bm, d), lambda i: (i, 0)),
        out_shape=jax.ShapeDtypeStruct((m, d), BF16),
        compiler_params=_params(("arbitrary",)),
        name="norm_cast",
    )(x, g.reshape(1, d))


W_IN_CAST_TILE = 512
W_IN_QKV_TILES = QKV_COLS // W_IN_CAST_TILE
W_IN_Z_TILES = D_SSM // W_IN_CAST_TILE
W_IN_XBC_TILES = XBC_COLS // W_IN_CAST_TILE


def _split_w_in_kernel(wt_ref, qkv_ref, z_ref, xbc_ref):
    j = pl.program_id(0)

    @pl.when(j < W_IN_QKV_TILES)
    def _():
        qkv_ref[...] = wt_ref[...].T.astype(qkv_ref.dtype)

    @pl.when((j >= W_IN_QKV_TILES) & (j < W_IN_QKV_TILES + W_IN_Z_TILES))
    def _():
        z_ref[...] = wt_ref[...].T.astype(z_ref.dtype)

    @pl.when(j >= W_IN_QKV_TILES + W_IN_Z_TILES)
    def _():
        xbc_ref[...] = wt_ref[...].T.astype(xbc_ref.dtype)


def _split_w_in(wt):
    k = wt.shape[2]
    bn = W_IN_CAST_TILE
    q, z = W_IN_QKV_TILES, W_IN_Z_TILES
    return pl.pallas_call(
        _split_w_in_kernel,
        grid=(q + z + W_IN_XBC_TILES,),
        in_specs=[pl.BlockSpec((None, bn, k), lambda j: (0, j, 0))],
        out_specs=[pl.BlockSpec((k, bn), lambda j: (0, jnp.minimum(j, q - 1))),
                   pl.BlockSpec((k, bn), lambda j: (0, jnp.clip(j - q, 0, z - 1))),
                   pl.BlockSpec((k, bn), lambda j: (0, jnp.clip(j - q - z, 0, W_IN_XBC_TILES - 1)))],
        out_shape=[jax.ShapeDtypeStruct((k, QKV_COLS), BF16),
                   jax.ShapeDtypeStruct((k, D_SSM), BF16),
                   jax.ShapeDtypeStruct((k, XBC_COLS), BF16)],
        compiler_params=_params(("arbitrary",)),
        name="split_w_in",
    )(wt)


def _proj_kernel(x_ref, w_ref, o_ref):
    o_ref[...] = jnp.dot(x_ref[...], w_ref[...], preferred_element_type=F32)


def _proj_dt_kernel(x_ref, w_ref, wdt_ref, o_ref, dt_ref):
    o_ref[...] = jnp.dot(x_ref[...], w_ref[...], preferred_element_type=F32)

    @pl.when(pl.program_id(1) == 0)
    def _():
        dt_ref[...] = jnp.dot(x_ref[...], wdt_ref[...], preferred_element_type=F32)


def _proj_side_kernel(x_ref, w_ref, side_ref, o_ref, side_bf_ref):
    o_ref[...] = jnp.dot(x_ref[...], w_ref[...], preferred_element_type=F32)
    side_bf_ref[...] = side_ref[...].astype(side_bf_ref.dtype)


def _proj(hn, w, bn, name, w_dt=None, side=None, bm=1024):
    m, k = hn.shape
    n = w.shape[1]
    nj = n // bn
    in_specs = [pl.BlockSpec((bm, k), lambda i, j: (i, 0)),
                pl.BlockSpec((k, bn), lambda i, j: (0, j))]
    out_specs = [pl.BlockSpec((bm, bn), lambda i, j: (i, j))]
    out_shape = [jax.ShapeDtypeStruct((m, n), F32)]
    args = [hn, w]
    body = _proj_kernel
    if w_dt is not None:
        in_specs.append(pl.BlockSpec((k, LANES), lambda i, j: (0, 0)))
        out_specs.append(pl.BlockSpec((bm, LANES), lambda i, j: (i, 0)))
        out_shape.append(jax.ShapeDtypeStruct((m, LANES), F32))
        args.append(w_dt)
        body = _proj_dt_kernel
    if side is not None:
        rows, cols = side.shape
        slab = rows // ((m // bm) * nj)
        assert slab * (m // bm) * nj == rows and w_dt is None
        in_specs.append(pl.BlockSpec((slab, cols), lambda i, j: (i * nj + j, 0)))
        out_specs.append(pl.BlockSpec((slab, cols), lambda i, j: (i * nj + j, 0)))
        out_shape.append(jax.ShapeDtypeStruct((rows, cols), BF16))
        args.append(side)
        body = _proj_side_kernel
    return pl.pallas_call(
        body,
        grid=(m // bm, nj),
        in_specs=in_specs,
        out_specs=out_specs,
        out_shape=out_shape,
        compiler_params=_params(("arbitrary", "arbitrary")),
        name=name,
    )(*args)


def _rope_tile(x, cos, sin_signed, first_half):
    rot = jnp.where(first_half, pltpu.roll(x, LANES - HEAD_DIM // 2, 1),
                    pltpu.roll(x, HEAD_DIM // 2, 1))
    return x * cos + rot * sin_signed


LOG2E = 1.4426950408889634
VT_ROWS = 80


def _swa_kernel(pos_ref, freq_ref, sink_ref, q_ref, k_ref, v_ref, o_ref, kband_ref, vt_ref):
    n = pl.program_id(0)
    w = WINDOW

    lane = lax.broadcasted_iota(jnp.int32, (1, LANES), 1)
    low_half = lane < HEAD_DIM
    first_half = (lane % HEAD_DIM) < (HEAD_DIM // 2)

    @pl.when(n == 0)
    def _():
        kband_ref[...] = jnp.zeros_like(kband_ref)
        vt_ref[...] = jnp.zeros_like(vt_ref)
        ones_row = jnp.where(lax.broadcasted_iota(jnp.int32, (VT_ROWS - HEAD_DIM, 2 * w), 0) == 0, 1.0, 0.0)
        for kh in range(N_KV_HEADS):
            vt_ref[kh, HEAD_DIM:VT_ROWS, :] = ones_row.astype(BF16)

    ang = pos_ref[...].astype(F32) * freq_ref[...]
    cos = jnp.cos(ang)
    sin_signed = jnp.where(first_half, -jnp.sin(ang), jnp.sin(ang))

    for kh in range(N_KV_HEADS):
        kband_ref[kh, 0:w, :] = kband_ref[kh, w:2 * w, :]
        vt_ref[kh, 0:HEAD_DIM, 0:w] = vt_ref[kh, 0:HEAD_DIM, w:2 * w]
    for t in range(KV_COLS // LANES):
        k_rot = _rope_tile(k_ref[:, t * LANES:(t + 1) * LANES], cos, sin_signed, first_half)
        k_sw = pltpu.roll(k_rot, HEAD_DIM, 1)
        kband_ref[2 * t, w:2 * w, :] = jnp.where(low_half, k_rot, k_sw).astype(BF16)
        kband_ref[2 * t + 1, w:2 * w, :] = jnp.where(low_half, k_sw, k_rot).astype(BF16)
        v_t = v_ref[:, t * LANES:(t + 1) * LANES].T
        vt_ref[2 * t, 0:HEAD_DIM, w:2 * w] = v_t[0:HEAD_DIM, :].astype(BF16)
        vt_ref[2 * t + 1, 0:HEAD_DIM, w:2 * w] = v_t[HEAD_DIM:2 * HEAD_DIM, :].astype(BF16)

    ki = lax.broadcasted_iota(jnp.int32, (2 * w, w), 0)
    qi = lax.broadcasted_iota(jnp.int32, (2 * w, w), 1)
    prev_off = jnp.where(n > 0, 0, 2 * w)
    valid = ((ki < w) & (ki > qi + prev_off)) | ((ki >= w) & (ki - w <= qi))
    qscale = HEAD_DIM ** -0.5 * LOG2E
    pairs = Q_PER_KV // 2
    eye = (lax.broadcasted_iota(jnp.int32, (LANES, LANES), 0)
           == lax.broadcasted_iota(jnp.int32, (LANES, LANES), 1)).astype(BF16)

    scores = []
    for kh in range(N_KV_HEADS):
        lhs = []
        for j in range(pairs):
            t = kh * pairs + j
            q_rot = _rope_tile(q_ref[:, t * LANES:(t + 1) * LANES], cos, sin_signed, first_half) * qscale
            lhs.append(jnp.where(low_half, q_rot, 0.0).astype(BF16))
            lhs.append(jnp.where(low_half, 0.0, q_rot).astype(BF16))
        scores.append(lax.dot_general(kband_ref[kh], jnp.concatenate(lhs, axis=0), (((1,), (1,)), ((), ())),
                                      preferred_element_type=F32))
    outs_t = []
    sink_terms_all = []
    for kh in range(N_KV_HEADS):
        st = scores[kh]
        probs = []
        sink_terms = []
        for hq in range(Q_PER_KV):
            s = jnp.where(valid, st[:, hq * w:(hq + 1) * w], -jnp.inf)
            sink = sink_ref[kh * Q_PER_KV + hq] * LOG2E
            m = jnp.maximum(jnp.max(s, axis=0, keepdims=True), sink)
            probs.append(jnp.exp2(s - m).astype(BF16))
            sink_terms.append(jnp.exp2(sink - m))
        outs_t.append(jnp.dot(vt_ref[kh], jnp.concatenate(probs, axis=1), preferred_element_type=F32))
        sink_terms_all.append(sink_terms)
    for kh in range(N_KV_HEADS):
        ot = outs_t[kh]
        sink_terms = sink_terms_all[kh]
        for j in range(pairs):
            halves = []
            for r in range(2):
                hq = 2 * j + r
                cols = slice(hq * w, (hq + 1) * w)
                denom = ot[HEAD_DIM:HEAD_DIM + 1, cols] + sink_terms[hq]
                halves.append(ot[0:HEAD_DIM, cols] / denom)
            pair_t = jnp.concatenate(halves, axis=0).astype(BF16)
            out = lax.dot_general(eye, pair_t, (((1,), (1,)), ((), ())), preferred_element_type=F32)
            t = kh * pairs + j
            o_ref[:, t * LANES:(t + 1) * LANES] = out.astype(o_ref.dtype)


def _split3(v):
    hi = v.astype(BF16)
    r = v - hi.astype(F32)
    mid = r.astype(BF16)
    lo = (r - mid.astype(F32)).astype(BF16)
    return hi, mid, lo


def _dot_exact_rhs(a, b_parts):
    acc = jnp.dot(a, b_parts[0], preferred_element_type=F32)
    for part in b_parts[1:]:
        acc = acc + jnp.dot(a, part, preferred_element_type=F32)
    return acc


def _dot_exact_lhs(a_parts, b):
    acc = jnp.dot(a_parts[0], b, preferred_element_type=F32)
    for part in a_parts[1:]:
        acc = acc + jnp.dot(part, b, preferred_element_type=F32)
    return acc


def _ssd_kernel(xbc_ref, z_ref, dt_ref, cw_ref, cb_ref, dtb_ref, alog_ref, dexp_ref,
                ng_ref, tri_ref, expand_ref, o_ref, xpad_ref, xc_ref, y_ref, state_ref):
    c = pl.program_id(0)
    L = CHUNK
    G, R, P, N = SSM_GROUPS, HEADS_PER_GROUP, HEAD_DIM, SSM_STATE
    GP = R * P

    @pl.when(c == 0)
    def _():
        state_ref[...] = jnp.zeros_like(state_ref)
        xpad_ref[:, L:L + SUBLANES, :] = jnp.zeros((XBC_COLS // LANES, SUBLANES, LANES), F32)

    xpad_ref[:, 0:SUBLANES, :] = xpad_ref[:, L:L + SUBLANES, :]

    for p in range(XBC_COLS // LANES):
        cs = slice(p * LANES, (p + 1) * LANES)
        xpad_ref[p, SUBLANES:SUBLANES + L, :] = xbc_ref[:, cs]
        acc = xpad_ref[p, SUBLANES - 3:SUBLANES - 3 + L, :] * cw_ref[0:1, cs]
        for k in range(1, SSM_CONV):
            acc = acc + xpad_ref[p, SUBLANES - 3 + k:SUBLANES - 3 + k + L, :] * cw_ref[k:k + 1, cs]
        acc = acc + cb_ref[:, cs]
        xc_ref[:, cs] = acc * jax.nn.sigmoid(acc)

    dt_raw = dt_ref[...] + dtb_ref[...]
    dt = jnp.maximum(dt_raw, 0.0) + jnp.log1p(jnp.exp(-jnp.abs(dt_raw)))
    a = -jnp.exp(alog_ref[...]) * LOG2E
    da = dt * a
    a_cs = _dot_exact_rhs(tri_ref[...], _split3(da))
    a_cs_t = a_cs.T
    expand = expand_ref[...]
    acs_x = _dot_exact_lhs(_split3(a_cs), expand)
    dt_x = _dot_exact_lhs(_split3(dt), expand)

    li = lax.broadcasted_iota(jnp.int32, (L, L), 0)
    si = lax.broadcasted_iota(jnp.int32, (L, L), 1)
    causal = li >= si
    lane = lax.broadcasted_iota(jnp.int32, (1, LANES), 1)
    low_half = lane < P

    groups = []
    for g in range(G):
        xs = slice(g * GP, (g + 1) * GP)
        x_g = xc_ref[:, xs]
        b_g = xc_ref[:, D_SSM + g * N:D_SSM + (g + 1) * N].astype(BF16)
        c_g = xc_ref[:, D_SSM + G * N + g * N:D_SSM + G * N + (g + 1) * N].astype(BF16)
        acs_g = acs_x[:, xs]
        xdt_g = x_g * dt_x[:, xs]
        cb = lax.dot_general(c_g, b_g, (((1,), (1,)), ((), ())), preferred_element_type=F32)
        a_last = acs_g[L - 1:L, :]
        xdte = (xdt_g * jnp.exp2(a_last - acs_g)).astype(BF16)
        new_state = lax.dot_general(b_g, xdte, (((0,), (0,)), ((), ())),
                                    preferred_element_type=F32)
        prev = state_ref[g]
        y_off = jnp.dot(c_g, prev.astype(BF16), preferred_element_type=F32) * jnp.exp2(acs_g)
        state_ref[g] = prev * jnp.exp2(a_last) + new_state
        groups.append((xs, x_g, xdt_g, cb, y_off))

    for g in range(G):
        xs, x_g, xdt_g, cb, y_off = groups[g]
        y_pairs = []
        for j in range(R // 2):
            ws = []
            for r in range(2):
                h = g * R + 2 * j + r
                seg = a_cs[:, h:h + 1] - a_cs_t[h:h + 1, :]
                decay = jnp.exp2(jnp.where(causal, seg, -jnp.inf))
                ws.append((cb * decay).astype(BF16))
            wcat = jnp.concatenate(ws, axis=1)
            xp = xdt_g[:, j * LANES:(j + 1) * LANES]
            rhs = jnp.concatenate([jnp.where(low_half, xp, 0.0), jnp.where(low_half, 0.0, xp)],
                                  axis=0).astype(BF16)
            y_pairs.append(jnp.dot(wcat, rhs, preferred_element_type=F32))
        y_diag = jnp.concatenate(y_pairs, axis=1)
        y_ref[:, xs] = y_diag + y_off + x_g * dexp_ref[:, xs]

    z = z_ref[...]
    y = y_ref[...] * (z * jax.nn.sigmoid(z))
    ms = jnp.mean(y * y, axis=-1, keepdims=True)
    o_ref[...] = (y * lax.rsqrt(ms + EPS) * ng_ref[...]).astype(o_ref.dtype)


N_SWA_INPUTS = 6
N_SSD_INPUTS = 11


def _mixers_kernel(*refs):
    swa_in = refs[0:N_SWA_INPUTS]
    ssd_in = refs[N_SWA_INPUTS:N_SWA_INPUTS + N_SSD_INPUTS]
    o_attn, o_ssm, kband, vt, xpad, xc, ybuf, state = refs[N_SWA_INPUTS + N_SSD_INPUTS:]
    _swa_kernel(*swa_in, o_attn, kband, vt)
    _ssd_kernel(*ssd_in, o_ssm, xpad, xc, ybuf, state)


def _mixers(qkv, pos, freq, sinks, xbc, z, dt, conv_w, conv_b, dt_bias, a_log, d_skip, norm_g):
    s = xbc.shape[0]
    L = CHUNK
    assert WINDOW == CHUNK
    pad = LANES - N_SSM_HEADS
    dtb = jnp.pad(dt_bias, (0, pad)).reshape(1, LANES)
    alog = jnp.pad(a_log, (0, pad)).reshape(1, LANES)
    dexp = jnp.repeat(d_skip, HEAD_DIM).reshape(1, D_SSM)
    tri = (jnp.arange(L)[:, None] >= jnp.arange(L)[None, :]).astype(BF16)
    expand = (jnp.arange(LANES)[:, None] == (jnp.arange(D_SSM)[None, :] // HEAD_DIM)).astype(BF16)
    const = lambda c: (0, 0)
    return pl.pallas_call(
        _mixers_kernel,
        grid=(s // L,),
        in_specs=[pl.BlockSpec((L, 1), lambda c: (c, 0)),
                  pl.BlockSpec((1, LANES), const),
                  pl.BlockSpec(memory_space=pltpu.SMEM),
                  pl.BlockSpec((L, D_ATTN), lambda c: (c, 0)),
                  pl.BlockSpec((L, KV_COLS), lambda c: (c, D_ATTN // KV_COLS)),
                  pl.BlockSpec((L, KV_COLS), lambda c: (c, D_ATTN // KV_COLS + 1)),
                  pl.BlockSpec((L, XBC_COLS), lambda c: (c, 0)),
                  pl.BlockSpec((L, D_SSM), lambda c: (c, 0)),
                  pl.BlockSpec((L, LANES), lambda c: (c, 0)),
                  pl.BlockSpec((SSM_CONV, XBC_COLS), const),
                  pl.BlockSpec((1, XBC_COLS), const),
                  pl.BlockSpec((1, LANES), const),
                  pl.BlockSpec((1, LANES), const),
                  pl.BlockSpec((1, D_SSM), const),
                  pl.BlockSpec((1, D_SSM), const),
                  pl.BlockSpec((L, L), const),
                  pl.BlockSpec((LANES, D_SSM), const)],
        out_specs=[pl.BlockSpec((L, D_ATTN), lambda c: (c, 0)),
                   pl.BlockSpec((L, D_SSM), lambda c: (c, 0))],
        out_shape=[jax.ShapeDtypeStruct((s, D_ATTN), BF16), jax.ShapeDtypeStruct((s, D_SSM), BF16)],
        scratch_shapes=[pltpu.VMEM((N_KV_HEADS, 2 * L, LANES), BF16),
                        pltpu.VMEM((N_KV_HEADS, VT_ROWS, 2 * L), BF16),
                        pltpu.VMEM((XBC_COLS // LANES, SUBLANES + L, LANES), F32),
                        pltpu.VMEM((L, XBC_COLS), F32),
                        pltpu.VMEM((L, D_SSM), F32),
                        pltpu.VMEM((SSM_GROUPS, SSM_STATE, HEADS_PER_GROUP * HEAD_DIM), F32)],
        compiler_params=_params(("arbitrary",)),
        name="mixers",
    )(pos, freq, sinks, qkv, qkv, qkv, xbc, z, dt, conv_w, conv_b.reshape(1, XBC_COLS), dtb, alog, dexp,
      norm_g.reshape(1, D_SSM), tri, expand)


def _out_proj_kernel(ya_ref, ys_ref, w_ref, res_ref, g_ref, o_ref, hg_ref, r_ref, ss_ref):
    j = pl.program_id(1)
    bn = w_ref.shape[1]

    @pl.when(j == 0)
    def _():
        ss_ref[...] = jnp.zeros_like(ss_ref)

    acc = jnp.dot(ya_ref[...], w_ref[0:D_ATTN, :], preferred_element_type=F32)
    acc = acc + jnp.dot(ys_ref[...], w_ref[D_ATTN:D_ATTN + D_SSM, :], preferred_element_type=F32)
    h = res_ref[...] + acc
    o_ref[...] = h
    hg_ref[...] = (h * g_ref[...]).astype(hg_ref.dtype)
    sq = h * h
    part = sq[:, 0:LANES]
    for c in range(1, bn // LANES):
        part = part + sq[:, c * LANES:(c + 1) * LANES]
    ss_ref[...] += part

    @pl.when(j == pl.num_programs(1) - 1)
    def _():
        ms = jnp.sum(ss_ref[...], axis=-1, keepdims=True) * (1.0 / D_MODEL)
        r_ref[...] = lax.rsqrt(ms + EPS)


def _out_proj(ya, ys, w, res, g, bm=1024, bn=1024):
    m = ya.shape[0]
    k, n = w.shape
    return pl.pallas_call(
        _out_proj_kernel,
        grid=(m // bm, n // bn),
        in_specs=[pl.BlockSpec((bm, D_ATTN), lambda i, j: (i, 0)),
                  pl.BlockSpec((bm, D_SSM), lambda i, j: (i, 0)),
                  pl.BlockSpec((k, bn), lambda i, j: (0, j)),
                  pl.BlockSpec((bm, bn), lambda i, j: (i, j)),
                  pl.BlockSpec((1, bn), lambda i, j: (0, j))],
        out_specs=[pl.BlockSpec((bm, bn), lambda i, j: (i, j)),
                   pl.BlockSpec((bm, bn), lambda i, j: (i, j)),
                   pl.BlockSpec((bm, 1), lambda i, j: (i, 0))],
        out_shape=[jax.ShapeDtypeStruct((m, n), F32),
                   jax.ShapeDtypeStruct((m, n), BF16),
                   jax.ShapeDtypeStruct((m, 1), F32)],
        scratch_shapes=[pltpu.VMEM((bm, LANES), F32)],
        compiler_params=_params(("arbitrary", "arbitrary"), vmem=VMEM_LIMIT_HIGH),
        name="out_proj",
    )(ya, ys, w, res, g.reshape(1, n))


FFN_ROW_SPLIT = 8


def _ffn_up_kernel(x_ref, r_ref, wg_ref, wu_ref, cw_ref, cb_ref, side_ref, o_ref, side_bf_ref, ubuf_ref, wbf_ref):
    t = pl.program_id(0)
    n_steps = pl.num_programs(0)
    n_row_blocks = (n_steps - 1) // N_FFN_TILES
    bm = x_ref.shape[0]
    ft = FFN_TILE
    planes_per_half = ft // LANES
    row_block_new = jnp.minimum(t, n_steps - 2) % n_row_blocks
    side_bf_ref[...] = side_ref[...].astype(side_bf_ref.dtype)

    @pl.when(t == 0)
    def _():
        ubuf_ref[...] = jnp.zeros_like(ubuf_ref)

    @pl.when((row_block_new == 0) & (t < n_steps - 1))
    def _():
        wbf_ref[0] = wg_ref[...].astype(BF16)
        wbf_ref[1] = wu_ref[...].astype(BF16)

    f_done = jnp.maximum(t - 1, 0) // n_row_blocks
    for q in range(planes_per_half):
        conv = []
        for half in range(2):
            plane = half * planes_per_half + q
            col = pl.multiple_of(half * D_FF + f_done * ft + q * LANES, LANES)
            acc = ubuf_ref[plane, SUBLANES - 2:SUBLANES - 2 + bm, :] * cw_ref[0:1, pl.ds(col, LANES)]
            acc = acc + ubuf_ref[plane, SUBLANES - 1:SUBLANES - 1 + bm, :] * cw_ref[1:2, pl.ds(col, LANES)]
            acc = acc + ubuf_ref[plane, SUBLANES:SUBLANES + bm, :] * cw_ref[2:3, pl.ds(col, LANES)]
            conv.append(acc + cb_ref[:, pl.ds(col, LANES)])
        gate, up = conv
        o_ref[:, q * LANES:(q + 1) * LANES] = (gate * jax.nn.sigmoid(gate) * up).astype(o_ref.dtype)
    last_rows = ubuf_ref[:, bm:bm + SUBLANES, :]
    ubuf_ref[:, 0:SUBLANES, :] = jnp.where(row_block_new == 0, jnp.zeros_like(last_rows), last_rows)

    hb = bm // FFN_ROW_SPLIT
    for sb in range(FFN_ROW_SPLIT):
        rows = slice(sb * hb, (sb + 1) * hb)
        x = x_ref[rows, :]
        r = r_ref[rows, :]
        for half in range(2):
            u = jnp.dot(x, wbf_ref[half], preferred_element_type=F32) * r
            for q in range(planes_per_half):
                ubuf_ref[half * planes_per_half + q, SUBLANES + sb * hb:SUBLANES + (sb + 1) * hb, :] = \
                    u[:, q * LANES:(q + 1) * LANES]


def _ffn_up(hg, r, w, cw, cb, side, bm=1024):
    m, k = hg.shape
    nf = N_FFN_TILES
    nrb = m // bm
    n_tiles = nrb * nf
    n_planes = 2 * FFN_TILE // LANES
    side_rows, side_cols = side.shape
    slab = side_rows // n_tiles
    assert slab * n_tiles == side_rows
    cur = lambda t: jnp.minimum(t, n_tiles - 1)
    done = lambda t: jnp.maximum(t - 1, 0)
    return pl.pallas_call(
        _ffn_up_kernel,
        grid=(n_tiles + 1,),
        in_specs=[pl.BlockSpec((bm, k), lambda t: (cur(t) % nrb, 0)),
                  pl.BlockSpec((bm, 1), lambda t: (cur(t) % nrb, 0)),
                  pl.BlockSpec((k, FFN_TILE), lambda t: (0, cur(t) // nrb)),
                  pl.BlockSpec((k, FFN_TILE), lambda t: (0, nf + cur(t) // nrb)),
                  pl.BlockSpec((FFN_CONV, 2 * D_FF), lambda t: (0, 0)),
                  pl.BlockSpec((1, 2 * D_FF), lambda t: (0, 0)),
                  pl.BlockSpec((slab, side_cols), lambda t: (cur(t), 0))],
        out_specs=[pl.BlockSpec((bm, FFN_TILE), lambda t: (done(t) % nrb, done(t) // nrb)),
                   pl.BlockSpec((slab, side_cols), lambda t: (cur(t), 0))],
        out_shape=[jax.ShapeDtypeStruct((m, D_FF), BF16),
                   jax.ShapeDtypeStruct((side_rows, side_cols), BF16)],
        scratch_shapes=[pltpu.VMEM((n_planes, SUBLANES + bm, LANES), F32),
                        pltpu.VMEM((2, k, FFN_TILE), BF16)],
        compiler_params=_params(("arbitrary",)),
        name="ffn_up",
    )(hg, r, w, w, cw, cb, side)


def _ffn_down_kernel(a_ref, w_ref, res_ref, side_ref, o_ref, obf_ref, side_bf_ref):
    h = res_ref[...] + jnp.dot(a_ref[...], w_ref[...], preferred_element_type=F32)
    o_ref[...] = h
    obf_ref[...] = h.astype(obf_ref.dtype)
    side_bf_ref[...] = side_ref[...].astype(side_bf_ref.dtype)


def _ffn_down(act, w, res, side, bm=512, bn=1024):
    m, k = act.shape
    n = w.shape[1]
    ni = m // bm
    side_rows, side_cols = side.shape
    slab = side_rows // ((n // bn) * ni)
    assert slab * (n // bn) * ni == side_rows
    return pl.pallas_call(
        _ffn_down_kernel,
        grid=(n // bn, m // bm),
        in_specs=[pl.BlockSpec((bm, k), lambda j, i: (i, 0)),
                  pl.BlockSpec((k, bn), lambda j, i: (0, j), pipeline_mode=pl.Buffered(1)),
                  pl.BlockSpec((bm, bn), lambda j, i: (i, j)),
                  pl.BlockSpec((slab, side_cols), lambda j, i: (j * ni + i, 0))],
        out_specs=[pl.BlockSpec((bm, bn), lambda j, i: (i, j)),
                   pl.BlockSpec((bm, bn), lambda j, i: (i, j)),
                   pl.BlockSpec((slab, side_cols), lambda j, i: (j * ni + i, 0))],
        out_shape=[jax.ShapeDtypeStruct((m, n), F32), jax.ShapeDtypeStruct((m, n), BF16),
                   jax.ShapeDtypeStruct((side_rows, side_cols), BF16)],
        compiler_params=_params(("arbitrary", "arbitrary"), vmem=VMEM_LIMIT_HIGH),
        name="ffn_down",
    )(act, w, res, side)


def _ple_kernel(hb_ref, wg_ref, bg_ref, p_ref, wp_ref, h_ref, gf_ref, o_ref, ss_ref):
    j = pl.program_id(1)
    nj = pl.num_programs(1)
    bn = wg_ref.shape[1]

    @pl.when(j == 0)
    def _():
        ss_ref[...] = jnp.zeros_like(ss_ref)

    gate = jax.nn.sigmoid(jnp.dot(hb_ref[...], wg_ref[...], preferred_element_type=F32) + bg_ref[...])
    emb = jnp.dot(p_ref[...].astype(BF16), wp_ref[...], preferred_element_type=F32)
    h = h_ref[...] + gate * emb
    sq = h * h
    part = sq[:, 0:LANES]
    for c in range(1, bn // LANES):
        part = part + sq[:, c * LANES:(c + 1) * LANES]
    ss_ref[...] += part
    col = pl.multiple_of(j * bn, bn)
    o_ref[:, pl.ds(col, bn)] = h

    @pl.when(j == nj - 1)
    def _():
        ms = jnp.sum(ss_ref[...], axis=-1, keepdims=True) * (1.0 / D_MODEL)
        o_ref[...] = o_ref[...] * lax.rsqrt(ms + EPS) * gf_ref[...]


def _ple(hb, wg, bg, p, wp, h, gf, bm=512, bn=1024):
    m, d = h.shape
    return pl.pallas_call(
        _ple_kernel,
        grid=(m // bm, d // bn),
        in_specs=[pl.BlockSpec((bm, d), lambda i, j: (i, 0)),
                  pl.BlockSpec((d, bn), lambda i, j: (0, j)),
                  pl.BlockSpec((1, bn), lambda i, j: (0, j)),
                  pl.BlockSpec((bm, PLE_DIM), lambda i, j: (i, 0)),
                  pl.BlockSpec((PLE_DIM, bn), lambda i, j: (0, j)),
                  pl.BlockSpec((bm, bn), lambda i, j: (i, j)),
                  pl.BlockSpec((1, d), lambda i, j: (0, 0))],
        out_specs=pl.BlockSpec((bm, d), lambda i, j: (i, 0)),
        out_shape=jax.ShapeDtypeStruct((m, d), F32),
        scratch_shapes=[pltpu.VMEM((bm, LANES), F32)],
        compiler_params=_params(("arbitrary", "arbitrary"), vmem=VMEM_LIMIT_HIGH),
        name="ple",
    )(hb, wg, bg, p, wp, h, gf)


def kernel(x, p, positions, norm_mix_g, w_in, attn_sinks, ssm_conv_w, ssm_conv_b, ssm_dt_bias,
           ssm_a_log, ssm_d, ssm_norm_g, w_o, norm_ffn_g, w_up, ffn_conv_w, ffn_conv_b, w_down,
           ple_gate_w, ple_gate_b, ple_proj, norm_final_g):
    b, s, d = x.shape
    assert (b, s, d) == (1, SEQ, D_MODEL) and w_in.shape[0] == 1
    h = x.reshape(s, d)
    pos = positions.reshape(s, 1)
    half = HEAD_DIM // 2
    inv_freq = ROPE_THETA ** (-jnp.arange(half, dtype=F32) / half)
    freq = jnp.tile(inv_freq, LANES // half).reshape(1, LANES)
    w_dt = lax.slice(w_in, (0, 0, DT_COL0), (1, d, DT_COL0 + N_SSM_HEADS)).reshape(d, N_SSM_HEADS)
    w_dt = jnp.pad(w_dt, ((0, 0), (0, LANES - N_SSM_HEADS)))

    hn = _norm_cast(h, norm_mix_g[0])
    w_qkv, w_z, w_xbc = _split_w_in(jnp.swapaxes(w_in, 1, 2))
    qkv, dt = _proj(hn, w_qkv, QKV_TILE, "in_proj_qkv", w_dt=w_dt.astype(BF16))
    z, = _proj(hn, w_z, SSM_TILE, "in_proj_z")
    xbc, w_o_bf = _proj(hn, w_xbc, SSM_TILE, "in_proj_xbc", side=w_o[0])
    y_attn, y_ssm = _mixers(qkv, pos, freq, attn_sinks[0], xbc, z, dt, ssm_conv_w[0], ssm_conv_b[0],
                            ssm_dt_bias[0], ssm_a_log[0], ssm_d[0], ssm_norm_g[0])
    h1, hg, r = _out_proj(y_attn, y_ssm, w_o_bf, h, norm_ffn_g[0])
    act, w_down_bf = _ffn_up(hg, r, w_up[0], ffn_conv_w[0], ffn_conv_b[0].reshape(1, 2 * D_FF), w_down[0])
    h2, h2b, w_gate_bf = _ffn_down(act, w_down_bf, h1, ple_gate_w[0])
    out = _ple(h2b, w_gate_bf, ple_gate_b[0].reshape(1, d), p[0].reshape(s, PLE_DIM),
               ple_proj[0].astype(BF16), h2, norm_final_g.reshape(1, d))
    return out.reshape(b, s, d)
```

```python
import jax
import jax.numpy as jnp
from jax import lax
from jax.experimental import pallas as pl
from jax.experimental.pallas import tpu as pltpu

F32 = jnp.float32
BF16 = jnp.bfloat16

D_MODEL = 4096
SEQ = 16384
PLE_DIM = 256
D_ATTN = 2048
D_SSM = 2048
HEAD_DIM = 64
N_Q_HEADS = 32
N_KV_HEADS = 4
Q_PER_KV = 8
WINDOW = 128
ROPE_THETA = 10000.0
N_SSM_HEADS = 32
SSM_STATE = 128
SSM_GROUPS = 8
HEADS_PER_GROUP = 4
SSM_CONV = 4
CHUNK = 128
D_FF = 11008
FFN_CONV = 3
EPS = 1e-6
KV_COLS = N_KV_HEADS * HEAD_DIM
QKV_COLS = D_ATTN + 2 * KV_COLS
XBC_COLS = D_SSM + 2 * SSM_GROUPS * SSM_STATE
DT_COL0 = QKV_COLS + D_SSM + XBC_COLS

LANES = 128
SUBLANES = 8
VMEM_LIMIT = 56 * 1024 * 1024
VMEM_LIMIT_HIGH = 62 * 1024 * 1024

QKV_TILE = QKV_COLS // 2
SSM_TILE = 1024
FFN_TILE = 256
N_FFN_TILES = D_FF // FFN_TILE


def _params(sem, vmem=VMEM_LIMIT, flags=None):
    return pltpu.CompilerParams(dimension_semantics=sem, vmem_limit_bytes=vmem, flags=flags)


def _norm_cast_kernel(x_ref, g_ref, o_ref):
    x = x_ref[...]
    ms = jnp.mean(x * x, axis=-1, keepdims=True)
    o_ref[...] = (x * lax.rsqrt(ms + EPS) * g_ref[...]).astype(o_ref.dtype)


def _norm_cast(x, g, bm=512):
    m, d = x.shape
    return pl.pallas_call(
        _norm_cast_kernel,
        grid=(m // bm,),
        in_specs=[pl.BlockSpec((bm, d), lambda i: (i, 0)),
                  pl.BlockSpec((1, d), lambda i: (0, 0))],
        out_specs=pl.BlockSpec((bm, d), lambda i: (i, 0)),
        out_shape=jax.ShapeDtypeStruct((m, d), BF16),
        compiler_params=_params(("arbitrary",)),
        name="norm_cast",
    )(x, g.reshape(1, d))


W_IN_CAST_TILE = 512
W_IN_QKV_TILES = QKV_COLS // W_IN_CAST_TILE
W_IN_Z_TILES = D_SSM // W_IN_CAST_TILE
W_IN_XBC_TILES = XBC_COLS // W_IN_CAST_TILE


def _split_w_in_kernel(wt_ref, qkv_ref, z_ref, xbc_ref):
    j = pl.program_id(0)

    @pl.when(j < W_IN_QKV_TILES)
    def _():
        qkv_ref[...] = wt_ref[...].T.astype(qkv_ref.dtype)

    @pl.when((j >= W_IN_QKV_TILES) & (j < W_IN_QKV_TILES + W_IN_Z_TILES))
    def _():
        z_ref[...] = wt_ref[...].T.astype(z_ref.dtype)

    @pl.when(j >= W_IN_QKV_TILES + W_IN_Z_TILES)
    def _():
        xbc_ref[...] = wt_ref[...].T.astype(xbc_ref.dtype)


def _split_w_in(wt):
    k = wt.shape[2]
    bn = W_IN_CAST_TILE
    q, z = W_IN_QKV_TILES, W_IN_Z_TILES
    return pl.pallas_call(
        _split_w_in_kernel,
        grid=(q + z + W_IN_XBC_TILES,),
        in_specs=[pl.BlockSpec((None, bn, k), lambda j: (0, j, 0))],
        out_specs=[pl.BlockSpec((k, bn), lambda j: (0, jnp.minimum(j, q - 1))),
                   pl.BlockSpec((k, bn), lambda j: (0, jnp.clip(j - q, 0, z - 1))),
                   pl.BlockSpec((k, bn), lambda j: (0, jnp.clip(j - q - z, 0, W_IN_XBC_TILES - 1)))],
        out_shape=[jax.ShapeDtypeStruct((k, QKV_COLS), BF16),
                   jax.ShapeDtypeStruct((k, D_SSM), BF16),
                   jax.ShapeDtypeStruct((k, XBC_COLS), BF16)],
        compiler_params=_params(("arbitrary",)),
        name="split_w_in",
    )(wt)


def _proj_kernel(x_ref, w_ref, o_ref):
    o_ref[...] = jnp.dot(x_ref[...], w_ref[...], preferred_element_type=F32)


def _proj_dt_kernel(x_ref, w_ref, wdt_ref, o_ref, dt_ref):
    o_ref[...] = jnp.dot(x_ref[...], w_ref[...], preferred_element_type=F32)

    @pl.when(pl.program_id(1) == 0)
    def _():
        dt_ref[...] = jnp.dot(x_ref[...], wdt_ref[...], preferred_element_type=F32)


def _proj_side_kernel(x_ref, w_ref, side_ref, o_ref, side_bf_ref):
    o_ref[...] = jnp.dot(x_ref[...], w_ref[...], preferred_element_type=F32)
    side_bf_ref[...] = side_ref[...].astype(side_bf_ref.dtype)


def _proj(hn, w, bn, name, w_dt=None, side=None, bm=1024):
    m, k = hn.shape
    n = w.shape[1]
    nj = n // bn
    in_specs = [pl.BlockSpec((bm, k), lambda i, j: (i, 0)),
                pl.BlockSpec((k, bn), lambda i, j: (0, j))]
    out_specs = [pl.BlockSpec((bm, bn), lambda i, j: (i, j))]
    out_shape = [jax.ShapeDtypeStruct((m, n), F32)]
    args = [hn, w]
    body = _proj_kernel
    if w_dt is not None:
        in_specs.append(pl.BlockSpec((k, LANES), lambda i, j: (0, 0)))
        out_specs.append(pl.BlockSpec((bm, LANES), lambda i, j: (i, 0)))
        out_shape.append(jax.ShapeDtypeStruct((m, LANES), F32))
        args.append(w_dt)
        body = _proj_dt_kernel
    if side is not None:
        rows, cols = side.shape
        slab = rows // ((m // bm) * nj)
        assert slab * (m // bm) * nj == rows and w_dt is None
        in_specs.append(pl.BlockSpec((slab, cols), lambda i, j: (i * nj + j, 0)))
        out_specs.append(pl.BlockSpec((slab, cols), lambda i, j: (i * nj + j, 0)))
        out_shape.append(jax.ShapeDtypeStruct((rows, cols), BF16))
        args.append(side)
        body = _proj_side_kernel
    return pl.pallas_call(
        body,
        grid=(m // bm, nj),
        in_specs=in_specs,
        out_specs=out_specs,
        out_shape=out_shape,
        compiler_params=_params(("arbitrary", "arbitrary")),
        name=name,
    )(*args)


def _rope_tile(x, cos, sin_signed, first_half):
    rot = jnp.where(first_half, pltpu.roll(x, LANES - HEAD_DIM // 2, 1),
                    pltpu.roll(x, HEAD_DIM // 2, 1))
    return x * cos + rot * sin_signed


LOG2E = 1.4426950408889634
VT_ROWS = 80


SWA_BLOCKS_PER_STEP = 4


def _swa_kernel(pos_ref, freq_ref, sink_ref, q_ref, k_ref, v_ref, o_ref, kband_ref, vt_ref):
    n = pl.program_id(0)
    w = WINDOW
    nb = SWA_BLOCKS_PER_STEP

    lane = lax.broadcasted_iota(jnp.int32, (1, LANES), 1)
    low_half = lane < HEAD_DIM
    first_half = (lane % HEAD_DIM) < (HEAD_DIM // 2)

    @pl.when(n == 0)
    def _():
        kband_ref[...] = jnp.zeros_like(kband_ref)
        vt_ref[...] = jnp.zeros_like(vt_ref)
        ones_row = jnp.where(lax.broadcasted_iota(jnp.int32, (VT_ROWS - HEAD_DIM, (1 + nb) * w), 0) == 0, 1.0, 0.0)
        for kh in range(N_KV_HEADS):
            vt_ref[kh, HEAD_DIM:VT_ROWS, :] = ones_row.astype(BF16)

    for kh in range(N_KV_HEADS):
        kband_ref[kh, 0:w, :] = kband_ref[kh, nb * w:(nb + 1) * w, :]
        vt_ref[kh, 0:HEAD_DIM, 0:w] = vt_ref[kh, 0:HEAD_DIM, nb * w:(nb + 1) * w]

    trig = []
    for blk in range(nb):
        rows = slice(blk * w, (blk + 1) * w)
        ang = pos_ref[rows, :].astype(F32) * freq_ref[...]
        cos = jnp.cos(ang)
        sin_signed = jnp.where(first_half, -jnp.sin(ang), jnp.sin(ang))
        trig.append((cos, sin_signed))
        dst = slice((1 + blk) * w, (2 + blk) * w)
        for t in range(KV_COLS // LANES):
            k_rot = _rope_tile(k_ref[rows, t * LANES:(t + 1) * LANES], cos, sin_signed, first_half)
            k_sw = pltpu.roll(k_rot, HEAD_DIM, 1)
            kband_ref[2 * t, dst, :] = jnp.where(low_half, k_rot, k_sw).astype(BF16)
            kband_ref[2 * t + 1, dst, :] = jnp.where(low_half, k_sw, k_rot).astype(BF16)
            v_t = v_ref[rows, t * LANES:(t + 1) * LANES].T
            vt_ref[2 * t, 0:HEAD_DIM, dst] = v_t[0:HEAD_DIM, :].astype(BF16)
            vt_ref[2 * t + 1, 0:HEAD_DIM, dst] = v_t[HEAD_DIM:2 * HEAD_DIM, :].astype(BF16)

    ki = lax.broadcasted_iota(jnp.int32, (2 * w, w), 0)
    qi = lax.broadcasted_iota(jnp.int32, (2 * w, w), 1)
    band = ((ki < w) & (ki > qi)) | ((ki >= w) & (ki - w <= qi))
    first_block_band = band & ((ki >= w) | (n > 0))
    qscale = HEAD_DIM ** -0.5 * LOG2E
    pairs = Q_PER_KV // 2
    eye = (lax.broadcasted_iota(jnp.int32, (LANES, LANES), 0)
           == lax.broadcasted_iota(jnp.int32, (LANES, LANES), 1)).astype(BF16)

    scores = []
    for blk in range(nb):
        rows = slice(blk * w, (blk + 1) * w)
        cos, sin_signed = trig[blk]
        for kh in range(N_KV_HEADS):
            lhs = []
            for j in range(pairs):
                t = kh * pairs + j
                q_rot = _rope_tile(q_ref[rows, t * LANES:(t + 1) * LANES], cos, sin_signed, first_half) * qscale
                lhs.append(jnp.where(low_half, q_rot, 0.0).astype(BF16))
                lhs.append(jnp.where(low_half, 0.0, q_rot).astype(BF16))
            scores.append(lax.dot_general(kband_ref[kh, blk * w:(blk + 2) * w, :], jnp.concatenate(lhs, axis=0),
                                          (((1,), (1,)), ((), ())), preferred_element_type=F32))
    outs_t = []
    sink_terms_all = []
    for blk in range(nb):
        valid = first_block_band if blk == 0 else band
        for kh in range(N_KV_HEADS):
            st = scores[blk * N_KV_HEADS + kh]
            probs = []
            sink_terms = []
            for hq in range(Q_PER_KV):
                s = jnp.where(valid, st[:, hq * w:(hq + 1) * w], -jnp.inf)
                sink = sink_ref[kh * Q_PER_KV + hq] * LOG2E
                m = jnp.maximum(jnp.max(s, axis=0, keepdims=True), sink)
                probs.append(jnp.exp2(s - m).astype(BF16))
                sink_terms.append(jnp.exp2(sink - m))
            outs_t.append(jnp.dot(vt_ref[kh, :, blk * w:(blk + 2) * w], jnp.concatenate(probs, axis=1),
                                  preferred_element_type=F32))
            sink_terms_all.append(sink_terms)
    for blk in range(nb):
        rows = slice(blk * w, (blk + 1) * w)
        for kh in range(N_KV_HEADS):
            ot = outs_t[blk * N_KV_HEADS + kh]
            sink_terms = sink_terms_all[blk * N_KV_HEADS + kh]
            for j in range(pairs):
                halves = []
                for r in range(2):
                    hq = 2 * j + r
                    cols = slice(hq * w, (hq + 1) * w)
                    denom = ot[HEAD_DIM:HEAD_DIM + 1, cols] + sink_terms[hq]
                    halves.append(ot[0:HEAD_DIM, cols] / denom)
                pair_t = jnp.concatenate(halves, axis=0).astype(BF16)
                out = lax.dot_general(eye, pair_t, (((1,), (1,)), ((), ())), preferred_element_type=F32)
                t = kh * pairs + j
                o_ref[rows, t * LANES:(t + 1) * LANES] = out.astype(o_ref.dtype)


def _swa(qkv, pos, freq, sinks):
    s = qkv.shape[0]
    w = WINDOW
    nb = SWA_BLOCKS_PER_STEP
    rows = nb * w
    return pl.pallas_call(
        _swa_kernel,
        grid=(s // rows,),
        in_specs=[pl.BlockSpec((rows, 1), lambda n: (n, 0)),
                  pl.BlockSpec((1, LANES), lambda n: (0, 0)),
                  pl.BlockSpec(memory_space=pltpu.SMEM),
                  pl.BlockSpec((rows, D_ATTN), lambda n: (n, 0)),
                  pl.BlockSpec((rows, KV_COLS), lambda n: (n, D_ATTN // KV_COLS)),
                  pl.BlockSpec((rows, KV_COLS), lambda n: (n, D_ATTN // KV_COLS + 1))],
        out_specs=pl.BlockSpec((rows, D_ATTN), lambda n: (n, 0)),
        out_shape=jax.ShapeDtypeStruct((s, D_ATTN), BF16),
        scratch_shapes=[pltpu.VMEM((N_KV_HEADS, (1 + nb) * w, LANES), BF16),
                        pltpu.VMEM((N_KV_HEADS, VT_ROWS, (1 + nb) * w), BF16)],
        compiler_params=_params(("arbitrary",)),
        name="swa",
    )(pos, freq, sinks, qkv, qkv, qkv)


def _split3(v):
    hi = v.astype(BF16)
    r = v - hi.astype(F32)
    mid = r.astype(BF16)
    lo = (r - mid.astype(F32)).astype(BF16)
    return hi, mid, lo


def _dot_exact_rhs(a, b_parts):
    acc = jnp.dot(a, b_parts[0], preferred_element_type=F32)
    for part in b_parts[1:]:
        acc = acc + jnp.dot(a, part, preferred_element_type=F32)
    return acc


def _dot_exact_lhs(a_parts, b):
    acc = jnp.dot(a_parts[0], b, preferred_element_type=F32)
    for part in a_parts[1:]:
        acc = acc + jnp.dot(part, b, preferred_element_type=F32)
    return acc


def _ssd_kernel(xbc_ref, z_ref, dt_ref, cw_ref, cb_ref, dtb_ref, alog_ref, dexp_ref,
                ng_ref, tri_ref, expand_ref, o_ref, xpad_ref, xc_ref, y_ref, state_ref):
    c = pl.program_id(0)
    L = CHUNK
    G, R, P, N = SSM_GROUPS, HEADS_PER_GROUP, HEAD_DIM, SSM_STATE
    GP = R * P

    @pl.when(c == 0)
    def _():
        state_ref[...] = jnp.zeros_like(state_ref)
        xpad_ref[:, L:L + SUBLANES, :] = jnp.zeros((XBC_COLS // LANES, SUBLANES, LANES), F32)

    xpad_ref[:, 0:SUBLANES, :] = xpad_ref[:, L:L + SUBLANES, :]

    for p in range(XBC_COLS // LANES):
        cs = slice(p * LANES, (p + 1) * LANES)
        xpad_ref[p, SUBLANES:SUBLANES + L, :] = xbc_ref[:, cs]
        acc = xpad_ref[p, SUBLANES - 3:SUBLANES - 3 + L, :] * cw_ref[0:1, cs]
        for k in range(1, SSM_CONV):
            acc = acc + xpad_ref[p, SUBLANES - 3 + k:SUBLANES - 3 + k + L, :] * cw_ref[k:k + 1, cs]
        acc = acc + cb_ref[:, cs]
        xc_ref[:, cs] = acc * jax.nn.sigmoid(acc)

    dt_raw = dt_ref[...] + dtb_ref[...]
    dt = jnp.maximum(dt_raw, 0.0) + jnp.log1p(jnp.exp(-jnp.abs(dt_raw)))
    a = -jnp.exp(alog_ref[...]) * LOG2E
    da = dt * a
    a_cs = _dot_exact_rhs(tri_ref[...], _split3(da))
    a_cs_t = a_cs.T
    expand = expand_ref[...]
    acs_x = _dot_exact_lhs(_split3(a_cs), expand)
    dt_x = _dot_exact_lhs(_split3(dt), expand)

    li = lax.broadcasted_iota(jnp.int32, (L, L), 0)
    si = lax.broadcasted_iota(jnp.int32, (L, L), 1)
    causal = li >= si
    lane = lax.broadcasted_iota(jnp.int32, (1, LANES), 1)
    low_half = lane < P

    groups = []
    for g in range(G):
        xs = slice(g * GP, (g + 1) * GP)
        x_g = xc_ref[:, xs]
        b_g = xc_ref[:, D_SSM + g * N:D_SSM + (g + 1) * N].astype(BF16)
        c_g = xc_ref[:, D_SSM + G * N + g * N:D_SSM + G * N + (g + 1) * N].astype(BF16)
        acs_g = acs_x[:, xs]
        xdt_g = x_g * dt_x[:, xs]
        cb = lax.dot_general(c_g, b_g, (((1,), (1,)), ((), ())), preferred_element_type=F32)
        a_last = acs_g[L - 1:L, :]
        xdte = (xdt_g * jnp.exp2(a_last - acs_g)).astype(BF16)
        new_state = lax.dot_general(b_g, xdte, (((0,), (0,)), ((), ())),
                                    preferred_element_type=F32)
        prev = state_ref[g]
        y_off = jnp.dot(c_g, prev.astype(BF16), preferred_element_type=F32) * jnp.exp2(acs_g)
        state_ref[g] = prev * jnp.exp2(a_last) + new_state
        groups.append((xs, x_g, xdt_g, cb, y_off))

    for g in range(G):
        xs, x_g, xdt_g, cb, y_off = groups[g]
        y_pairs = []
        for j in range(R // 2):
            ws = []
            for r in range(2):
                h = g * R + 2 * j + r
                seg = a_cs[:, h:h + 1] - a_cs_t[h:h + 1, :]
                decay = jnp.exp2(jnp.where(causal, seg, -jnp.inf))
                ws.append((cb * decay).astype(BF16))
            wcat = jnp.concatenate(ws, axis=1)
            xp = xdt_g[:, j * LANES:(j + 1) * LANES]
            rhs = jnp.concatenate([jnp.where(low_half, xp, 0.0), jnp.where(low_half, 0.0, xp)],
                                  axis=0).astype(BF16)
            y_pairs.append(jnp.dot(wcat, rhs, preferred_element_type=F32))
        y_diag = jnp.concatenate(y_pairs, axis=1)
        y_ref[:, xs] = y_diag + y_off + x_g * dexp_ref[:, xs]

    z = z_ref[...]
    y = y_ref[...] * (z * jax.nn.sigmoid(z))
    ms = jnp.mean(y * y, axis=-1, keepdims=True)
    o_ref[...] = (y * lax.rsqrt(ms + EPS) * ng_ref[...]).astype(o_ref.dtype)


def _ssd(xbc, z, dt, conv_w, conv_b, dt_bias, a_log, d_skip, norm_g):
    s = xbc.shape[0]
    L = CHUNK
    pad = LANES - N_SSM_HEADS
    dtb = jnp.pad(dt_bias, (0, pad)).reshape(1, LANES)
    alog = jnp.pad(a_log, (0, pad)).reshape(1, LANES)
    dexp = jnp.repeat(d_skip, HEAD_DIM).reshape(1, D_SSM)
    tri = (jnp.arange(L)[:, None] >= jnp.arange(L)[None, :]).astype(BF16)
    expand = (jnp.arange(LANES)[:, None] == (jnp.arange(D_SSM)[None, :] // HEAD_DIM)).astype(BF16)
    const = lambda c: (0, 0)
    return pl.pallas_call(
        _ssd_kernel,
        grid=(s // L,),
        in_specs=[pl.BlockSpec((L, XBC_COLS), lambda c: (c, 0)),
                  pl.BlockSpec((L, D_SSM), lambda c: (c, 0)),
                  pl.BlockSpec((L, LANES), lambda c: (c, 0)),
                  pl.BlockSpec((SSM_CONV, XBC_COLS), const),
                  pl.BlockSpec((1, XBC_COLS), const),
                  pl.BlockSpec((1, LANES), const),
                  pl.BlockSpec((1, LANES), const),
                  pl.BlockSpec((1, D_SSM), const),
                  pl.BlockSpec((1, D_SSM), const),
                  pl.BlockSpec((L, L), const),
                  pl.BlockSpec((LANES, D_SSM), const)],
        out_specs=pl.BlockSpec((L, D_SSM), lambda c: (c, 0)),
        out_shape=jax.ShapeDtypeStruct((s, D_SSM), BF16),
        scratch_shapes=[pltpu.VMEM((XBC_COLS // LANES, SUBLANES + L, LANES), F32),
                        pltpu.VMEM((L, XBC_COLS), F32),
                        pltpu.VMEM((L, D_SSM), F32),
                        pltpu.VMEM((SSM_GROUPS, SSM_STATE, HEADS_PER_GROUP * HEAD_DIM), F32)],
        compiler_params=_params(("arbitrary",)),
        name="ssd",
    )(xbc, z, dt, conv_w, conv_b.reshape(1, XBC_COLS), dtb, alog, dexp,
      norm_g.reshape(1, D_SSM), tri, expand)


def _out_proj_kernel(ya_ref, ys_ref, w_ref, res_ref, g_ref, o_ref, hg_ref, r_ref, ss_ref):
    j = pl.program_id(1)
    bn = w_ref.shape[1]

    @pl.when(j == 0)
    def _():
        ss_ref[...] = jnp.zeros_like(ss_ref)

    acc = jnp.dot(ya_ref[...], w_ref[0:D_ATTN, :], preferred_element_type=F32)
    acc = acc + jnp.dot(ys_ref[...], w_ref[D_ATTN:D_ATTN + D_SSM, :], preferred_element_type=F32)
    h = res_ref[...] + acc
    o_ref[...] = h
    hg_ref[...] = (h * g_ref[...]).astype(hg_ref.dtype)
    sq = h * h
    part = sq[:, 0:LANES]
    for c in range(1, bn // LANES):
        part = part + sq[:, c * LANES:(c + 1) * LANES]
    ss_ref[...] += part

    @pl.when(j == pl.num_programs(1) - 1)
    def _():
        ms = jnp.sum(ss_ref[...], axis=-1, keepdims=True) * (1.0 / D_MODEL)
        r_ref[...] = lax.rsqrt(ms + EPS)


def _out_proj(ya, ys, w, res, g, bm=1024, bn=1024):
    m = ya.shape[0]
    k, n = w.shape
    return pl.pallas_call(
        _out_proj_kernel,
        grid=(m // bm, n // bn),
        in_specs=[pl.BlockSpec((bm, D_ATTN), lambda i, j: (i, 0)),
                  pl.BlockSpec((bm, D_SSM), lambda i, j: (i, 0)),
                  pl.BlockSpec((k, bn), lambda i, j: (0, j)),
                  pl.BlockSpec((bm, bn), lambda i, j: (i, j)),
                  pl.BlockSpec((1, bn), lambda i, j: (0, j))],
        out_specs=[pl.BlockSpec((bm, bn), lambda i, j: (i, j)),
                   pl.BlockSpec((bm, bn), lambda i, j: (i, j)),
                   pl.BlockSpec((bm, 1), lambda i, j: (i, 0))],
        out_shape=[jax.ShapeDtypeStruct((m, n), F32),
                   jax.ShapeDtypeStruct((m, n), BF16),
                   jax.ShapeDtypeStruct((m, 1), F32)],
        scratch_shapes=[pltpu.VMEM((bm, LANES), F32)],
        compiler_params=_params(("arbitrary", "arbitrary"), vmem=VMEM_LIMIT_HIGH),
        name="out_proj",
    )(ya, ys, w, res, g.reshape(1, n))


FFN_ROW_SPLIT = 8


def _ffn_up_kernel(x_ref, r_ref, wg_ref, wu_ref, cw_ref, cb_ref, side_ref, o_ref, side_bf_ref, ubuf_ref, wbf_ref):
    t = pl.program_id(0)
    n_steps = pl.num_programs(0)
    n_row_blocks = (n_steps - 1) // N_FFN_TILES
    bm = x_ref.shape[0]
    ft = FFN_TILE
    planes_per_half = ft // LANES
    row_block_new = jnp.minimum(t, n_steps - 2) % n_row_blocks
    side_bf_ref[...] = side_ref[...].astype(side_bf_ref.dtype)

    @pl.when(t == 0)
    def _():
        ubuf_ref[...] = jnp.zeros_like(ubuf_ref)

    @pl.when((row_block_new == 0) & (t < n_steps - 1))
    def _():
        wbf_ref[0] = wg_ref[...].astype(BF16)
        wbf_ref[1] = wu_ref[...].astype(BF16)

    f_done = jnp.maximum(t - 1, 0) // n_row_blocks
    for q in range(planes_per_half):
        conv = []
        for half in range(2):
            plane = half * planes_per_half + q
            col = pl.multiple_of(half * D_FF + f_done * ft + q * LANES, LANES)
            acc = ubuf_ref[plane, SUBLANES - 2:SUBLANES - 2 + bm, :] * cw_ref[0:1, pl.ds(col, LANES)]
            acc = acc + ubuf_ref[plane, SUBLANES - 1:SUBLANES - 1 + bm, :] * cw_ref[1:2, pl.ds(col, LANES)]
            acc = acc + ubuf_ref[plane, SUBLANES:SUBLANES + bm, :] * cw_ref[2:3, pl.ds(col, LANES)]
            conv.append(acc + cb_ref[:, pl.ds(col, LANES)])
        gate, up = conv
        o_ref[:, q * LANES:(q + 1) * LANES] = (gate * jax.nn.sigmoid(gate) * up).astype(o_ref.dtype)
    last_rows = ubuf_ref[:, bm:bm + SUBLANES, :]
    ubuf_ref[:, 0:SUBLANES, :] = jnp.where(row_block_new == 0, jnp.zeros_like(last_rows), last_rows)

    hb = bm // FFN_ROW_SPLIT
    for sb in range(FFN_ROW_SPLIT):
        rows = slice(sb * hb, (sb + 1) * hb)
        x = x_ref[rows, :]
        r = r_ref[rows, :]
        for half in range(2):
            u = jnp.dot(x, wbf_ref[half], preferred_element_type=F32) * r
            for q in range(planes_per_half):
                ubuf_ref[half * planes_per_half + q, SUBLANES + sb * hb:SUBLANES + (sb + 1) * hb, :] = \
                    u[:, q * LANES:(q + 1) * LANES]


def _ffn_up(hg, r, w, cw, cb, side, bm=1024):
    m, k = hg.shape
    nf = N_FFN_TILES
    nrb = m // bm
    n_tiles = nrb * nf
    n_planes = 2 * FFN_TILE // LANES
    side_rows, side_cols = side.shape
    slab = side_rows // n_tiles
    assert slab * n_tiles == side_rows
    cur = lambda t: jnp.minimum(t, n_tiles - 1)
    done = lambda t: jnp.maximum(t - 1, 0)
    return pl.pallas_call(
        _ffn_up_kernel,
        grid=(n_tiles + 1,),
        in_specs=[pl.BlockSpec((bm, k), lambda t: (cur(t) % nrb, 0)),
                  pl.BlockSpec((bm, 1), lambda t: (cur(t) % nrb, 0)),
                  pl.BlockSpec((k, FFN_TILE), lambda t: (0, cur(t) // nrb)),
                  pl.BlockSpec((k, FFN_TILE), lambda t: (0, nf + cur(t) // nrb)),
                  pl.BlockSpec((FFN_CONV, 2 * D_FF), lambda t: (0, 0)),
                  pl.BlockSpec((1, 2 * D_FF), lambda t: (0, 0)),
                  pl.BlockSpec((slab, side_cols), lambda t: (cur(t), 0))],
        out_specs=[pl.BlockSpec((bm, FFN_TILE), lambda t: (done(t) % nrb, done(t) // nrb)),
                   pl.BlockSpec((slab, side_cols), lambda t: (cur(t), 0))],
        out_shape=[jax.ShapeDtypeStruct((m, D_FF), BF16),
                   jax.ShapeDtypeStruct((side_rows, side_cols), BF16)],
        scratch_shapes=[pltpu.VMEM((n_planes, SUBLANES + bm, LANES), F32),
                        pltpu.VMEM((2, k, FFN_TILE), BF16)],
        compiler_params=_params(("arbitrary",)),
        name="ffn_up",
    )(hg, r, w, w, cw, cb, side)


def _ffn_down_kernel(a_ref, w_ref, res_ref, side_ref, o_ref, obf_ref, side_bf_ref):
    h = res_ref[...] + jnp.dot(a_ref[...], w_ref[...], preferred_element_type=F32)
    o_ref[...] = h
    obf_ref[...] = h.astype(obf_ref.dtype)
    side_bf_ref[...] = side_ref[...].astype(side_bf_ref.dtype)


def _ffn_down(act, w, res, side, bm=512, bn=1024):
    m, k = act.shape
    n = w.shape[1]
    ni = m // bm
    side_rows, side_cols = side.shape
    slab = side_rows // ((n // bn) * ni)
    assert slab * (n // bn) * ni == side_rows
    return pl.pallas_call(
        _ffn_down_kernel,
        grid=(n // bn, m // bm),
        in_specs=[pl.BlockSpec((bm, k), lambda j, i: (i, 0)),
                  pl.BlockSpec((k, bn), lambda j, i: (0, j), pipeline_mode=pl.Buffered(1)),
                  pl.BlockSpec((bm, bn), lambda j, i: (i, j)),
                  pl.BlockSpec((slab, side_cols), lambda j, i: (j * ni + i, 0))],
        out_specs=[pl.BlockSpec((bm, bn), lambda j, i: (i, j)),
                   pl.BlockSpec((bm, bn), lambda j, i: (i, j)),
                   pl.BlockSpec((slab, side_cols), lambda j, i: (j * ni + i, 0))],
        out_shape=[jax.ShapeDtypeStruct((m, n), F32), jax.ShapeDtypeStruct((m, n), BF16),
                   jax.ShapeDtypeStruct((side_rows, side_cols), BF16)],
        compiler_params=_params(("arbitrary", "arbitrary"), vmem=VMEM_LIMIT_HIGH),
        name="ffn_down",
    )(act, w, res, side)


def _ple_kernel(hb_ref, wg_ref, bg_ref, p_ref, wp_ref, h_ref, gf_ref, o_ref, ss_ref):
    j = pl.program_id(1)
    nj = pl.num_programs(1)
    bn = wg_ref.shape[1]

    @pl.when(j == 0)
    def _():
        ss_ref[...] = jnp.zeros_like(ss_ref)

    gate = jax.nn.sigmoid(jnp.dot(hb_ref[...], wg_ref[...], preferred_element_type=F32) + bg_ref[...])
    emb = jnp.dot(p_ref[...].astype(BF16), wp_ref[...], preferred_element_type=F32)
    h = h_ref[...] + gate * emb
    sq = h * h
    part = sq[:, 0:LANES]
    for c in range(1, bn // LANES):
        part = part + sq[:, c * LANES:(c + 1) * LANES]
    ss_ref[...] += part
    col = pl.multiple_of(j * bn, bn)
    o_ref[:, pl.ds(col, bn)] = h

    @pl.when(j == nj - 1)
    def _():
        ms = jnp.sum(ss_ref[...], axis=-1, keepdims=True) * (1.0 / D_MODEL)
        o_ref[...] = o_ref[...] * lax.rsqrt(ms + EPS) * gf_ref[...]


def _ple(hb, wg, bg, p, wp, h, gf, bm=512, bn=1024):
    m, d = h.shape
    return pl.pallas_call(
        _ple_kernel,
        grid=(m // bm, d // bn),
        in_specs=[pl.BlockSpec((bm, d), lambda i, j: (i, 0)),
                  pl.BlockSpec((d, bn), lambda i, j: (0, j)),
                  pl.BlockSpec((1, bn), lambda i, j: (0, j)),
                  pl.BlockSpec((bm, PLE_DIM), lambda i, j: (i, 0)),
                  pl.BlockSpec((PLE_DIM, bn), lambda i, j: (0, j)),
                  pl.BlockSpec((bm, bn), lambda i, j: (i, j)),
                  pl.BlockSpec((1, d), lambda i, j: (0, 0))],
        out_specs=pl.BlockSpec((bm, d), lambda i, j: (i, 0)),
        out_shape=jax.ShapeDtypeStruct((m, d), F32),
        scratch_shapes=[pltpu.VMEM((bm, LANES), F32)],
        compiler_params=_params(("arbitrary", "arbitrary"), vmem=VMEM_LIMIT_HIGH),
        name="ple",
    )(hb, wg, bg, p, wp, h, gf)


def kernel(x, p, positions, norm_mix_g, w_in, attn_sinks, ssm_conv_w, ssm_conv_b, ssm_dt_bias,
           ssm_a_log, ssm_d, ssm_norm_g, w_o, norm_ffn_g, w_up, ffn_conv_w, ffn_conv_b, w_down,
           ple_gate_w, ple_gate_b, ple_proj, norm_final_g):
    b, s, d = x.shape
    assert (b, s, d) == (1, SEQ, D_MODEL) and w_in.shape[0] == 1
    h = x.reshape(s, d)
    pos = positions.reshape(s, 1)
    half = HEAD_DIM // 2
    inv_freq = ROPE_THETA ** (-jnp.arange(half, dtype=F32) / half)
    freq = jnp.tile(inv_freq, LANES // half).reshape(1, LANES)
    w_dt = lax.slice(w_in, (0, 0, DT_COL0), (1, d, DT_COL0 + N_SSM_HEADS)).reshape(d, N_SSM_HEADS)
    w_dt = jnp.pad(w_dt, ((0, 0), (0, LANES - N_SSM_HEADS)))

    hn = _norm_cast(h, norm_mix_g[0])
    w_qkv, w_z, w_xbc = _split_w_in(jnp.swapaxes(w_in, 1, 2))
    qkv, dt = _proj(hn, w_qkv, QKV_TILE, "in_proj_qkv", w_dt=w_dt.astype(BF16))
    z, = _proj(hn, w_z, SSM_TILE, "in_proj_z")
    xbc, w_o_bf = _proj(hn, w_xbc, SSM_TILE, "in_proj_xbc", side=w_o[0])
    y_attn = _swa(qkv, pos, freq, attn_sinks[0])
    y_ssm = _ssd(xbc, z, dt, ssm_conv_w[0], ssm_conv_b[0], ssm_dt_bias[0], ssm_a_log[0], ssm_d[0],
                 ssm_norm_g[0])
    h1, hg, r = _out_proj(y_attn, y_ssm, w_o_bf, h, norm_ffn_g[0])
    act, w_down_bf = _ffn_up(hg, r, w_up[0], ffn_conv_w[0], ffn_conv_b[0].reshape(1, 2 * D_FF), w_down[0])
    h2, h2b, w_gate_bf = _ffn_down(act, w_down_bf, h1, ple_gate_w[0])
    out = _ple(h2b, w_gate_bf, ple_gate_b[0].reshape(1, d), p[0].reshape(s, PLE_DIM),
               ple_proj[0].astype(BF16), h2, norm_final_g.reshape(1, d))
    return out.reshape(b, s, d)
```

```python
import jax
import jax.numpy as jnp
from jax import lax
from jax.experimental import pallas as pl
from jax.experimental.pallas import tpu as pltpu

F32 = jnp.float32
BF16 = jnp.bfloat16

D_MODEL = 4096
SEQ = 16384
PLE_DIM = 256
D_ATTN = 2048
D_SSM = 2048
HEAD_DIM = 64
N_Q_HEADS = 32
N_KV_HEADS = 4
Q_PER_KV = 8
WINDOW = 128
ROPE_THETA = 10000.0
N_SSM_HEADS = 32
SSM_STATE = 128
SSM_GROUPS = 8
HEADS_PER_GROUP = 4
SSM_CONV = 4
CHUNK = 128
D_FF = 11008
FFN_CONV = 3
EPS = 1e-6
KV_COLS = N_KV_HEADS * HEAD_DIM
QKV_COLS = D_ATTN + 2 * KV_COLS
XBC_COLS = D_SSM + 2 * SSM_GROUPS * SSM_STATE
DT_COL0 = QKV_COLS + D_SSM + XBC_COLS

LANES = 128
SUBLANES = 8
VMEM_LIMIT = 56 * 1024 * 1024
VMEM_LIMIT_HIGH = 62 * 1024 * 1024

QKV_TILE = QKV_COLS // 2
SSM_TILE = 1024
FFN_TILE = 256
N_FFN_TILES = D_FF // FFN_TILE


def _params(sem, vmem=VMEM_LIMIT, flags=None):
    return pltpu.CompilerParams(dimension_semantics=sem, vmem_limit_bytes=vmem, flags=flags)


def _norm_cast_kernel(x_ref, g_ref, o_ref):
    x = x_ref[...]
    ms = jnp.mean(x * x, axis=-1, keepdims=True)
    o_ref[...] = (x * lax.rsqrt(ms + EPS) * g_ref[...]).astype(o_ref.dtype)


def _norm_cast(x, g, bm=512):
    m, d = x.shape
    return pl.pallas_call(
        _norm_cast_kernel,
        grid=(m // bm,),
        in_specs=[pl.BlockSpec((bm, d), lambda i: (i, 0)),
                  pl.BlockSpec((1, d), lambda i: (0, 0))],
        out_specs=pl.BlockSpec((bm, d), lambda i: (i, 0)),
        out_shape=jax.ShapeDtypeStruct((m, d), BF16),
        compiler_params=_params(("arbitrary",)),
        name="norm_cast",
    )(x, g.reshape(1, d))


W_IN_CAST_TILE = 512
W_IN_QKV_TILES = QKV_COLS // W_IN_CAST_TILE
W_IN_Z_TILES = D_SSM // W_IN_CAST_TILE
W_IN_XBC_TILES = XBC_COLS // W_IN_CAST_TILE


def _split_w_in_kernel(wt_ref, qkv_ref, z_ref, xbc_ref):
    j = pl.program_id(0)

    @pl.when(j < W_IN_QKV_TILES)
    def _():
        qkv_ref[...] = wt_ref[...].T.astype(qkv_ref.dtype)

    @pl.when((j >= W_IN_QKV_TILES) & (j < W_IN_QKV_TILES + W_IN_Z_TILES))
    def _():
        z_ref[...] = wt_ref[...].T.astype(z_ref.dtype)

    @pl.when(j >= W_IN_QKV_TILES + W_IN_Z_TILES)
    def _():
        xbc_ref[...] = wt_ref[...].T.astype(xbc_ref.dtype)


def _split_w_in(wt):
    k = wt.shape[2]
    bn = W_IN_CAST_TILE
    q, z = W_IN_QKV_TILES, W_IN_Z_TILES
    return pl.pallas_call(
        _split_w_in_kernel,
        grid=(q + z + W_IN_XBC_TILES,),
        in_specs=[pl.BlockSpec((None, bn, k), lambda j: (0, j, 0))],
        out_specs=[pl.BlockSpec((k, bn), lambda j: (0, jnp.minimum(j, q - 1))),
                   pl.BlockSpec((k, bn), lambda j: (0, jnp.clip(j - q, 0, z - 1))),
                   pl.BlockSpec((k, bn), lambda j: (0, jnp.clip(j - q - z, 0, W_IN_XBC_TILES - 1)))],
        out_shape=[jax.ShapeDtypeStruct((k, QKV_COLS), BF16),
                   jax.ShapeDtypeStruct((k, D_SSM), BF16),
                   jax.ShapeDtypeStruct((k, XBC_COLS), BF16)],
        compiler_params=_params(("arbitrary",)),
        name="split_w_in",
    )(wt)


def _proj_kernel(x_ref, w_ref, o_ref):
    o_ref[...] = jnp.dot(x_ref[...], w_ref[...], preferred_element_type=F32)


def _proj_dt_kernel(x_ref, w_ref, wdt_ref, o_ref, dt_ref):
    o_ref[...] = jnp.dot(x_ref[...], w_ref[...], preferred_element_type=F32)

    @pl.when(pl.program_id(1) == 0)
    def _():
        dt_ref[...] = jnp.dot(x_ref[...], wdt_ref[...], preferred_element_type=F32)


def _proj_side_kernel(x_ref, w_ref, side_ref, o_ref, side_bf_ref):
    o_ref[...] = jnp.dot(x_ref[...], w_ref[...], preferred_element_type=F32)
    side_bf_ref[...] = side_ref[...].astype(side_bf_ref.dtype)


def _proj(hn, w, bn, name, w_dt=None, side=None, bm=1024):
    m, k = hn.shape
    n = w.shape[1]
    nj = n // bn
    in_specs = [pl.BlockSpec((bm, k), lambda i, j: (i, 0)),
                pl.BlockSpec((k, bn), lambda i, j: (0, j))]
    out_specs = [pl.BlockSpec((bm, bn), lambda i, j: (i, j))]
    out_shape = [jax.ShapeDtypeStruct((m, n), F32)]
    args = [hn, w]
    body = _proj_kernel
    if w_dt is not None:
        in_specs.append(pl.BlockSpec((k, LANES), lambda i, j: (0, 0)))
        out_specs.append(pl.BlockSpec((bm, LANES), lambda i, j: (i, 0)))
        out_shape.append(jax.ShapeDtypeStruct((m, LANES), F32))
        args.append(w_dt)
        body = _proj_dt_kernel
    if side is not None:
        rows, cols = side.shape
        slab = rows // ((m // bm) * nj)
        assert slab * (m // bm) * nj == rows and w_dt is None
        in_specs.append(pl.BlockSpec((slab, cols), lambda i, j: (i * nj + j, 0)))
        out_specs.append(pl.BlockSpec((slab, cols), lambda i, j: (i * nj + j, 0)))
        out_shape.append(jax.ShapeDtypeStruct((rows, cols), BF16))
        args.append(side)
        body = _proj_side_kernel
    return pl.pallas_call(
        body,
        grid=(m // bm, nj),
        in_specs=in_specs,
        out_specs=out_specs,
        out_shape=out_shape,
        compiler_params=_params(("arbitrary", "arbitrary")),
        name=name,
    )(*args)


def _rope_tile(x, cos, sin_signed, first_half):
    rot = jnp.where(first_half, pltpu.roll(x, LANES - HEAD_DIM // 2, 1),
                    pltpu.roll(x, HEAD_DIM // 2, 1))
    return x * cos + rot * sin_signed


LOG2E = 1.4426950408889634
VT_ROWS = 80


SWA_BLOCKS_PER_STEP = 4


def _swa_kernel(pos_ref, freq_ref, sink_ref, q_ref, k_ref, v_ref, o_ref, kband_ref, vt_ref):
    n = pl.program_id(0)
    w = WINDOW
    nb = SWA_BLOCKS_PER_STEP

    lane = lax.broadcasted_iota(jnp.int32, (1, LANES), 1)
    low_half = lane < HEAD_DIM
    first_half = (lane % HEAD_DIM) < (HEAD_DIM // 2)

    @pl.when(n == 0)
    def _():
        kband_ref[...] = jnp.zeros_like(kband_ref)
        vt_ref[...] = jnp.zeros_like(vt_ref)
        ones_row = jnp.where(lax.broadcasted_iota(jnp.int32, (VT_ROWS - HEAD_DIM, (1 + nb) * w), 0) == 0, 1.0, 0.0)
        for kh in range(N_KV_HEADS):
            vt_ref[kh, HEAD_DIM:VT_ROWS, :] = ones_row.astype(BF16)

    for kh in range(N_KV_HEADS):
        kband_ref[kh, 0:w, :] = kband_ref[kh, nb * w:(nb + 1) * w, :]
        vt_ref[kh, 0:HEAD_DIM, 0:w] = vt_ref[kh, 0:HEAD_DIM, nb * w:(nb + 1) * w]

    trig = []
    for blk in range(nb):
        rows = slice(blk * w, (blk + 1) * w)
        ang = pos_ref[rows, :].astype(F32) * freq_ref[...]
        cos = jnp.cos(ang)
        sin_signed = jnp.where(first_half, -jnp.sin(ang), jnp.sin(ang))
        trig.append((cos, sin_signed))
        dst = slice((1 + blk) * w, (2 + blk) * w)
        for t in range(KV_COLS // LANES):
            k_rot = _rope_tile(k_ref[rows, t * LANES:(t + 1) * LANES], cos, sin_signed, first_half)
            k_sw = pltpu.roll(k_rot, HEAD_DIM, 1)
            kband_ref[2 * t, dst, :] = jnp.where(low_half, k_rot, k_sw).astype(BF16)
            kband_ref[2 * t + 1, dst, :] = jnp.where(low_half, k_sw, k_rot).astype(BF16)
            v_t = v_ref[rows, t * LANES:(t + 1) * LANES].T
            vt_ref[2 * t, 0:HEAD_DIM, dst] = v_t[0:HEAD_DIM, :].astype(BF16)
            vt_ref[2 * t + 1, 0:HEAD_DIM, dst] = v_t[HEAD_DIM:2 * HEAD_DIM, :].astype(BF16)

    ki = lax.broadcasted_iota(jnp.int32, (2 * w, w), 0)
    qi = lax.broadcasted_iota(jnp.int32, (2 * w, w), 1)
    band = ((ki < w) & (ki > qi)) | ((ki >= w) & (ki - w <= qi))
    first_block_band = band & ((ki >= w) | (n > 0))
    qscale = HEAD_DIM ** -0.5 * LOG2E
    pairs = Q_PER_KV // 2
    eye = (lax.broadcasted_iota(jnp.int32, (LANES, LANES), 0)
           == lax.broadcasted_iota(jnp.int32, (LANES, LANES), 1)).astype(BF16)

    scores = []
    for blk in range(nb):
        rows = slice(blk * w, (blk + 1) * w)
        cos, sin_signed = trig[blk]
        for kh in range(N_KV_HEADS):
            lhs = []
            for j in range(pairs):
                t = kh * pairs + j
                q_rot = _rope_tile(q_ref[rows, t * LANES:(t + 1) * LANES], cos, sin_signed, first_half) * qscale
                lhs.append(jnp.where(low_half, q_rot, 0.0).astype(BF16))
                lhs.append(jnp.where(low_half, 0.0, q_rot).astype(BF16))
            scores.append(lax.dot_general(kband_ref[kh, blk * w:(blk + 2) * w, :], jnp.concatenate(lhs, axis=0),
                                          (((1,), (1,)), ((), ())), preferred_element_type=F32))
    outs_t = []
    sink_terms_all = []
    for blk in range(nb):
        valid = first_block_band if blk == 0 else band
        for kh in range(N_KV_HEADS):
            st = scores[blk * N_KV_HEADS + kh]
            probs = []
            sink_terms = []
            for hq in range(Q_PER_KV):
                s = jnp.where(valid, st[:, hq * w:(hq + 1) * w], -jnp.inf)
                sink = sink_ref[kh * Q_PER_KV + hq] * LOG2E
                m = jnp.maximum(jnp.max(s, axis=0, keepdims=True), sink)
                probs.append(jnp.exp2(s - m).astype(BF16))
                sink_terms.append(jnp.exp2(sink - m))
            outs_t.append(jnp.dot(vt_ref[kh, :, blk * w:(blk + 2) * w], jnp.concatenate(probs, axis=1),
                                  preferred_element_type=F32))
            sink_terms_all.append(sink_terms)
    for blk in range(nb):
        rows = slice(blk * w, (blk + 1) * w)
        for kh in range(N_KV_HEADS):
            ot = outs_t[blk * N_KV_HEADS + kh]
            sink_terms = sink_terms_all[blk * N_KV_HEADS + kh]
            for j in range(pairs):
                halves = []
                for r in range(2):
                    hq = 2 * j + r
                    cols = slice(hq * w, (hq + 1) * w)
                    denom = ot[HEAD_DIM:HEAD_DIM + 1, cols] + sink_terms[hq]
                    halves.append(ot[0:HEAD_DIM, cols] / denom)
                pair_t = jnp.concatenate(halves, axis=0).astype(BF16)
                out = lax.dot_general(eye, pair_t, (((1,), (1,)), ((), ())), preferred_element_type=F32)
                t = kh * pairs + j
                o_ref[rows, t * LANES:(t + 1) * LANES] = out.astype(o_ref.dtype)


def _swa(qkv, pos, freq, sinks):
    s = qkv.shape[0]
    w = WINDOW
    nb = SWA_BLOCKS_PER_STEP
    rows = nb * w
    return pl.pallas_call(
        _swa_kernel,
        grid=(s // rows,),
        in_specs=[pl.BlockSpec((rows, 1), lambda n: (n, 0)),
                  pl.BlockSpec((1, LANES), lambda n: (0, 0)),
                  pl.BlockSpec(memory_space=pltpu.SMEM),
                  pl.BlockSpec((rows, D_ATTN), lambda n: (n, 0)),
                  pl.BlockSpec((rows, KV_COLS), lambda n: (n, D_ATTN // KV_COLS)),
                  pl.BlockSpec((rows, KV_COLS), lambda n: (n, D_ATTN // KV_COLS + 1))],
        out_specs=pl.BlockSpec((rows, D_ATTN), lambda n: (n, 0)),
        out_shape=jax.ShapeDtypeStruct((s, D_ATTN), BF16),
        scratch_shapes=[pltpu.VMEM((N_KV_HEADS, (1 + nb) * w, LANES), BF16),
                        pltpu.VMEM((N_KV_HEADS, VT_ROWS, (1 + nb) * w), BF16)],
        compiler_params=_params(("arbitrary",)),
        name="swa",
    )(pos, freq, sinks, qkv, qkv, qkv)


def _split3(v):
    hi = v.astype(BF16)
    r = v - hi.astype(F32)
    mid = r.astype(BF16)
    lo = (r - mid.astype(F32)).astype(BF16)
    return hi, mid, lo


def _dot_exact_rhs(a, b_parts):
    acc = jnp.dot(a, b_parts[0], preferred_element_type=F32)
    for part in b_parts[1:]:
        acc = acc + jnp.dot(a, part, preferred_element_type=F32)
    return acc


def _dot_exact_lhs(a_parts, b):
    acc = jnp.dot(a_parts[0], b, preferred_element_type=F32)
    for part in a_parts[1:]:
        acc = acc + jnp.dot(part, b, preferred_element_type=F32)
    return acc


SSD_CHUNKS_PER_STEP = 2


def _ssd_kernel(xbc_ref, z_ref, dt_ref, cw_ref, cb_ref, dtb_ref, alog_ref, dexp_ref,
                ng_ref, tri_ref, expand_ref, o_ref, xpad_ref, xc_ref, y_ref, state_ref):
    c = pl.program_id(0)
    L = CHUNK
    NCH = SSD_CHUNKS_PER_STEP
    LB = NCH * L
    G, R, P, N = SSM_GROUPS, HEADS_PER_GROUP, HEAD_DIM, SSM_STATE
    GP = R * P

    @pl.when(c == 0)
    def _():
        state_ref[...] = jnp.zeros_like(state_ref)
        xpad_ref[:, LB:LB + SUBLANES, :] = jnp.zeros((XBC_COLS // LANES, SUBLANES, LANES), F32)

    xpad_ref[:, 0:SUBLANES, :] = xpad_ref[:, LB:LB + SUBLANES, :]

    for p in range(XBC_COLS // LANES):
        cs = slice(p * LANES, (p + 1) * LANES)
        xpad_ref[p, SUBLANES:SUBLANES + LB, :] = xbc_ref[:, cs]
        acc = xpad_ref[p, SUBLANES - 3:SUBLANES - 3 + LB, :] * cw_ref[0:1, cs]
        for k in range(1, SSM_CONV):
            acc = acc + xpad_ref[p, SUBLANES - 3 + k:SUBLANES - 3 + k + LB, :] * cw_ref[k:k + 1, cs]
        acc = acc + cb_ref[:, cs]
        xc_ref[:, cs] = acc * jax.nn.sigmoid(acc)

    dt_raw = dt_ref[...] + dtb_ref[...]
    dt_all = jnp.maximum(dt_raw, 0.0) + jnp.log1p(jnp.exp(-jnp.abs(dt_raw)))
    a = -jnp.exp(alog_ref[...]) * LOG2E
    expand = expand_ref[...]
    li = lax.broadcasted_iota(jnp.int32, (L, L), 0)
    si = lax.broadcasted_iota(jnp.int32, (L, L), 1)
    causal = li >= si
    lane = lax.broadcasted_iota(jnp.int32, (1, LANES), 1)
    low_half = lane < P

    per_chunk = []
    for ch in range(NCH):
        rows = slice(ch * L, (ch + 1) * L)
        dt = dt_all[rows, :]
        a_cs = _dot_exact_rhs(tri_ref[...], _split3(dt * a))
        per_chunk.append((rows, a_cs, a_cs.T, _dot_exact_lhs(_split3(a_cs), expand),
                          _dot_exact_lhs(_split3(dt), expand)))

    groups = []
    for ch in range(NCH):
        rows, a_cs, a_cs_t, acs_x, dt_x = per_chunk[ch]
        for g in range(G):
            xs = slice(g * GP, (g + 1) * GP)
            x_g = xc_ref[rows, xs]
            b_g = xc_ref[rows, D_SSM + g * N:D_SSM + (g + 1) * N].astype(BF16)
            c_g = xc_ref[rows, D_SSM + G * N + g * N:D_SSM + G * N + (g + 1) * N].astype(BF16)
            acs_g = acs_x[:, xs]
            xdt_g = x_g * dt_x[:, xs]
            cb = lax.dot_general(c_g, b_g, (((1,), (1,)), ((), ())), preferred_element_type=F32)
            a_last = acs_g[L - 1:L, :]
            xdte = (xdt_g * jnp.exp2(a_last - acs_g)).astype(BF16)
            new_state = lax.dot_general(b_g, xdte, (((0,), (0,)), ((), ())),
                                        preferred_element_type=F32)
            prev = state_ref[g]
            y_off = jnp.dot(c_g, prev.astype(BF16), preferred_element_type=F32) * jnp.exp2(acs_g)
            state_ref[g] = prev * jnp.exp2(a_last) + new_state
            groups.append((rows, xs, x_g, xdt_g, cb, y_off, a_cs, a_cs_t))

    for idx in range(NCH * G):
        rows, xs, x_g, xdt_g, cb, y_off, a_cs, a_cs_t = groups[idx]
        g = idx % G
        y_pairs = []
        for j in range(R // 2):
            ws = []
            for r in range(2):
                h = g * R + 2 * j + r
                seg = a_cs[:, h:h + 1] - a_cs_t[h:h + 1, :]
                decay = jnp.exp2(jnp.where(causal, seg, -jnp.inf))
                ws.append((cb * decay).astype(BF16))
            wcat = jnp.concatenate(ws, axis=1)
            xp = xdt_g[:, j * LANES:(j + 1) * LANES]
            rhs = jnp.concatenate([jnp.where(low_half, xp, 0.0), jnp.where(low_half, 0.0, xp)],
                                  axis=0).astype(BF16)
            y_pairs.append(jnp.dot(wcat, rhs, preferred_element_type=F32))
        y_diag = jnp.concatenate(y_pairs, axis=1)
        y_ref[rows, xs] = y_diag + y_off + x_g * dexp_ref[:, xs]

    z = z_ref[...]
    y = y_ref[...] * (z * jax.nn.sigmoid(z))
    ms = jnp.mean(y * y, axis=-1, keepdims=True)
    o_ref[...] = (y * lax.rsqrt(ms + EPS) * ng_ref[...]).astype(o_ref.dtype)


def _ssd(xbc, z, dt, conv_w, conv_b, dt_bias, a_log, d_skip, norm_g):
    s = xbc.shape[0]
    L = CHUNK
    LB = SSD_CHUNKS_PER_STEP * L
    pad = LANES - N_SSM_HEADS
    dtb = jnp.pad(dt_bias, (0, pad)).reshape(1, LANES)
    alog = jnp.pad(a_log, (0, pad)).reshape(1, LANES)
    dexp = jnp.repeat(d_skip, HEAD_DIM).reshape(1, D_SSM)
    tri = (jnp.arange(L)[:, None] >= jnp.arange(L)[None, :]).astype(BF16)
    expand = (jnp.arange(LANES)[:, None] == (jnp.arange(D_SSM)[None, :] // HEAD_DIM)).astype(BF16)
    const = lambda c: (0, 0)
    return pl.pallas_call(
        _ssd_kernel,
        grid=(s // LB,),
        in_specs=[pl.BlockSpec((LB, XBC_COLS), lambda c: (c, 0)),
                  pl.BlockSpec((LB, D_SSM), lambda c: (c, 0)),
                  pl.BlockSpec((LB, LANES), lambda c: (c, 0)),
                  pl.BlockSpec((SSM_CONV, XBC_COLS), const),
                  pl.BlockSpec((1, XBC_COLS), const),
                  pl.BlockSpec((1, LANES), const),
                  pl.BlockSpec((1, LANES), const),
                  pl.BlockSpec((1, D_SSM), const),
                  pl.BlockSpec((1, D_SSM), const),
                  pl.BlockSpec((L, L), const),
                  pl.BlockSpec((LANES, D_SSM), const)],
        out_specs=pl.BlockSpec((LB, D_SSM), lambda c: (c, 0)),
        out_shape=jax.ShapeDtypeStruct((s, D_SSM), BF16),
        scratch_shapes=[pltpu.VMEM((XBC_COLS // LANES, SUBLANES + LB, LANES), F32),
                        pltpu.VMEM((LB, XBC_COLS), F32),
                        pltpu.VMEM((LB, D_SSM), F32),
                        pltpu.VMEM((SSM_GROUPS, SSM_STATE, HEADS_PER_GROUP * HEAD_DIM), F32)],
        compiler_params=_params(("arbitrary",)),
        name="ssd",
    )(xbc, z, dt, conv_w, conv_b.reshape(1, XBC_COLS), dtb, alog, dexp,
      norm_g.reshape(1, D_SSM), tri, expand)


def _out_proj_kernel(ya_ref, ys_ref, w_ref, res_ref, g_ref, o_ref, hg_ref, r_ref, ss_ref):
    j = pl.program_id(1)
    bn = w_ref.shape[1]

    @pl.when(j == 0)
    def _():
        ss_ref[...] = jnp.zeros_like(ss_ref)

    acc = jnp.dot(ya_ref[...], w_ref[0:D_ATTN, :], preferred_element_type=F32)
    acc = acc + jnp.dot(ys_ref[...], w_ref[D_ATTN:D_ATTN + D_SSM, :], preferred_element_type=F32)
    h = res_ref[...] + acc
    o_ref[...] = h
    hg_ref[...] = (h * g_ref[...]).astype(hg_ref.dtype)
    sq = h * h
    part = sq[:, 0:LANES]
    for c in range(1, bn // LANES):
        part = part + sq[:, c * LANES:(c + 1) * LANES]
    ss_ref[...] += part

    @pl.when(j == pl.num_programs(1) - 1)
    def _():
        ms = jnp.sum(ss_ref[...], axis=-1, keepdims=True) * (1.0 / D_MODEL)
        r_ref[...] = lax.rsqrt(ms + EPS)


def _out_proj(ya, ys, w, res, g, bm=1024, bn=1024):
    m = ya.shape[0]
    k, n = w.shape
    return pl.pallas_call(
        _out_proj_kernel,
        grid=(m // bm, n // bn),
        in_specs=[pl.BlockSpec((bm, D_ATTN), lambda i, j: (i, 0)),
                  pl.BlockSpec((bm, D_SSM), lambda i, j: (i, 0)),
                  pl.BlockSpec((k, bn), lambda i, j: (0, j)),
                  pl.BlockSpec((bm, bn), lambda i, j: (i, j)),
                  pl.BlockSpec((1, bn), lambda i, j: (0, j))],
        out_specs=[pl.BlockSpec((bm, bn), lambda i, j: (i, j)),
                   pl.BlockSpec((bm, bn), lambda i, j: (i, j)),
                   pl.BlockSpec((bm, 1), lambda i, j: (i, 0))],
        out_shape=[jax.ShapeDtypeStruct((m, n), F32),
                   jax.ShapeDtypeStruct((m, n), BF16),
                   jax.ShapeDtypeStruct((m, 1), F32)],
        scratch_shapes=[pltpu.VMEM((bm, LANES), F32)],
        compiler_params=_params(("arbitrary", "arbitrary"), vmem=VMEM_LIMIT_HIGH),
        name="out_proj",
    )(ya, ys, w, res, g.reshape(1, n))


FFN_ROW_SPLIT = 8


def _ffn_up_kernel(x_ref, r_ref, wg_ref, wu_ref, cw_ref, cb_ref, side_ref, o_ref, side_bf_ref, ubuf_ref, wbf_ref):
    t = pl.program_id(0)
    n_steps = pl.num_programs(0)
    n_row_blocks = (n_steps - 1) // N_FFN_TILES
    bm = x_ref.shape[0]
    ft = FFN_TILE
    planes_per_half = ft // LANES
    row_block_new = jnp.minimum(t, n_steps - 2) % n_row_blocks
    side_bf_ref[...] = side_ref[...].astype(side_bf_ref.dtype)

    @pl.when(t == 0)
    def _():
        ubuf_ref[...] = jnp.zeros_like(ubuf_ref)

    @pl.when((row_block_new == 0) & (t < n_steps - 1))
    def _():
        wbf_ref[0] = wg_ref[...].astype(BF16)
        wbf_ref[1] = wu_ref[...].astype(BF16)

    f_done = jnp.maximum(t - 1, 0) // n_row_blocks
    for q in range(planes_per_half):
        conv = []
        for half in range(2):
            plane = half * planes_per_half + q
            col = pl.multiple_of(half * D_FF + f_done * ft + q * LANES, LANES)
            acc = ubuf_ref[plane, SUBLANES - 2:SUBLANES - 2 + bm, :] * cw_ref[0:1, pl.ds(col, LANES)]
            acc = acc + ubuf_ref[plane, SUBLANES - 1:SUBLANES - 1 + bm, :] * cw_ref[1:2, pl.ds(col, LANES)]
            acc = acc + ubuf_ref[plane, SUBLANES:SUBLANES + bm, :] * cw_ref[2:3, pl.ds(col, LANES)]
            conv.append(acc + cb_ref[:, pl.ds(col, LANES)])
        gate, up = conv
        o_ref[:, q * LANES:(q + 1) * LANES] = (gate * jax.nn.sigmoid(gate) * up).astype(o_ref.dtype)
    last_rows = ubuf_ref[:, bm:bm + SUBLANES, :]
    ubuf_ref[:, 0:SUBLANES, :] = jnp.where(row_block_new == 0, jnp.zeros_like(last_rows), last_rows)

    hb = bm // FFN_ROW_SPLIT
    for sb in range(FFN_ROW_SPLIT):
        rows = slice(sb * hb, (sb + 1) * hb)
        x = x_ref[rows, :]
        r = r_ref[rows, :]
        for half in range(2):
            u = jnp.dot(x, wbf_ref[half], preferred_element_type=F32) * r
            for q in range(planes_per_half):
                ubuf_ref[half * planes_per_half + q, SUBLANES + sb * hb:SUBLANES + (sb + 1) * hb, :] = \
                    u[:, q * LANES:(q + 1) * LANES]


def _ffn_up(hg, r, w, cw, cb, side, bm=1024):
    m, k = hg.shape
    nf = N_FFN_TILES
    nrb = m // bm
    n_tiles = nrb * nf
    n_planes = 2 * FFN_TILE // LANES
    side_rows, side_cols = side.shape
    slab = side_rows // n_tiles
    assert slab * n_tiles == side_rows
    cur = lambda t: jnp.minimum(t, n_tiles - 1)
    done = lambda t: jnp.maximum(t - 1, 0)
    return pl.pallas_call(
        _ffn_up_kernel,
        grid=(n_tiles + 1,),
        in_specs=[pl.BlockSpec((bm, k), lambda t: (cur(t) % nrb, 0)),
                  pl.BlockSpec((bm, 1), lambda t: (cur(t) % nrb, 0)),
                  pl.BlockSpec((k, FFN_TILE), lambda t: (0, cur(t) // nrb)),
                  pl.BlockSpec((k, FFN_TILE), lambda t: (0, nf + cur(t) // nrb)),
                  pl.BlockSpec((FFN_CONV, 2 * D_FF), lambda t: (0, 0)),
                  pl.BlockSpec((1, 2 * D_FF), lambda t: (0, 0)),
                  pl.BlockSpec((slab, side_cols), lambda t: (cur(t), 0))],
        out_specs=[pl.BlockSpec((bm, FFN_TILE), lambda t: (done(t) % nrb, done(t) // nrb)),
                   pl.BlockSpec((slab, side_cols), lambda t: (cur(t), 0))],
        out_shape=[jax.ShapeDtypeStruct((m, D_FF), BF16),
                   jax.ShapeDtypeStruct((side_rows, side_cols), BF16)],
        scratch_shapes=[pltpu.VMEM((n_planes, SUBLANES + bm, LANES), F32),
                        pltpu.VMEM((2, k, FFN_TILE), BF16)],
        compiler_params=_params(("arbitrary",)),
        name="ffn_up",
    )(hg, r, w, w, cw, cb, side)


def _ffn_down_kernel(a_ref, w_ref, res_ref, side_ref, o_ref, obf_ref, side_bf_ref):
    h = res_ref[...] + jnp.dot(a_ref[...], w_ref[...], preferred_element_type=F32)
    o_ref[...] = h
    obf_ref[...] = h.astype(obf_ref.dtype)
    side_bf_ref[...] = side_ref[...].astype(side_bf_ref.dtype)


def _ffn_down(act, w, res, side, bm=512, bn=1024):
    m, k = act.shape
    n = w.shape[1]
    ni = m // bm
    side_rows, side_cols = side.shape
    slab = side_rows // ((n // bn) * ni)
    assert slab * (n // bn) * ni == side_rows
    return pl.pallas_call(
        _ffn_down_kernel,
        grid=(n // bn, m // bm),
        in_specs=[pl.BlockSpec((bm, k), lambda j, i: (i, 0)),
                  pl.BlockSpec((k, bn), lambda j, i: (0, j), pipeline_mode=pl.Buffered(1)),
                  pl.BlockSpec((bm, bn), lambda j, i: (i, j)),
                  pl.BlockSpec((slab, side_cols), lambda j, i: (j * ni + i, 0))],
        out_specs=[pl.BlockSpec((bm, bn), lambda j, i: (i, j)),
                   pl.BlockSpec((bm, bn), lambda j, i: (i, j)),
                   pl.BlockSpec((slab, side_cols), lambda j, i: (j * ni + i, 0))],
        out_shape=[jax.ShapeDtypeStruct((m, n), F32), jax.ShapeDtypeStruct((m, n), BF16),
                   jax.ShapeDtypeStruct((side_rows, side_cols), BF16)],
        compiler_params=_params(("arbitrary", "arbitrary"), vmem=VMEM_LIMIT_HIGH),
        name="ffn_down",
    )(act, w, res, side)


def _ple_kernel(hb_ref, wg_ref, bg_ref, p_ref, wp_ref, h_ref, gf_ref, o_ref, ss_ref):
    j = pl.program_id(1)
    nj = pl.num_programs(1)
    bn = wg_ref.shape[1]

    @pl.when(j == 0)
    def _():
        ss_ref[...] = jnp.zeros_like(ss_ref)

    gate = jax.nn.sigmoid(jnp.dot(hb_ref[...], wg_ref[...], preferred_element_type=F32) + bg_ref[...])
    emb = jnp.dot(p_ref[...].astype(BF16), wp_ref[...], preferred_element_type=F32)
    h = h_ref[...] + gate * emb
    sq = h * h
    part = sq[:, 0:LANES]
    for c in range(1, bn // LANES):
        part = part + sq[:, c * LANES:(c + 1) * LANES]
    ss_ref[...] += part
    col = pl.multiple_of(j * bn, bn)
    o_ref[:, pl.ds(col, bn)] = h

    @pl.when(j == nj - 1)
    def _():
        ms = jnp.sum(ss_ref[...], axis=-1, keepdims=True) * (1.0 / D_MODEL)
        o_ref[...] = o_ref[...] * lax.rsqrt(ms + EPS) * gf_ref[...]


def _ple(hb, wg, bg, p, wp, h, gf, bm=512, bn=1024):
    m, d = h.shape
    return pl.pallas_call(
        _ple_kernel,
        grid=(m // bm, d // bn),
        in_specs=[pl.BlockSpec((bm, d), lambda i, j: (i, 0)),
                  pl.BlockSpec((d, bn), lambda i, j: (0, j)),
                  pl.BlockSpec((1, bn), lambda i, j: (0, j)),
                  pl.BlockSpec((bm, PLE_DIM), lambda i, j: (i, 0)),
                  pl.BlockSpec((PLE_DIM, bn), lambda i, j: (0, j)),
                  pl.BlockSpec((bm, bn), lambda i, j: (i, j)),
                  pl.BlockSpec((1, d), lambda i, j: (0, 0))],
        out_specs=pl.BlockSpec((bm, d), lambda i, j: (i, 0)),
        out_shape=jax.ShapeDtypeStruct((m, d), F32),
        scratch_shapes=[pltpu.VMEM((bm, LANES), F32)],
        compiler_params=_params(("arbitrary", "arbitrary"), vmem=VMEM_LIMIT_HIGH),
        name="ple",
    )(hb, wg, bg, p, wp, h, gf)


def kernel(x, p, positions, norm_mix_g, w_in, attn_sinks, ssm_conv_w, ssm_conv_b, ssm_dt_bias,
           ssm_a_log, ssm_d, ssm_norm_g, w_o, norm_ffn_g, w_up, ffn_conv_w, ffn_conv_b, w_down,
           ple_gate_w, ple_gate_b, ple_proj, norm_final_g):
    b, s, d = x.shape
    assert (b, s, d) == (1, SEQ, D_MODEL) and w_in.shape[0] == 1
    h = x.reshape(s, d)
    pos = positions.reshape(s, 1)
    half = HEAD_DIM // 2
    inv_freq = ROPE_THETA ** (-jnp.arange(half, dtype=F32) / half)
    freq = jnp.tile(inv_freq, LANES // half).reshape(1, LANES)
    w_dt = lax.slice(w_in, (0, 0, DT_COL0), (1, d, DT_COL0 + N_SSM_HEADS)).reshape(d, N_SSM_HEADS)
    w_dt = jnp.pad(w_dt, ((0, 0), (0, LANES - N_SSM_HEADS)))

    hn = _norm_cast(h, norm_mix_g[0])
    w_qkv, w_z, w_xbc = _split_w_in(jnp.swapaxes(w_in, 1, 2))
    qkv, dt = _proj(hn, w_qkv, QKV_TILE, "in_proj_qkv", w_dt=w_dt.astype(BF16))
    z, = _proj(hn, w_z, SSM_TILE, "in_proj_z")
    xbc, w_o_bf = _proj(hn, w_xbc, SSM_TILE, "in_proj_xbc", side=w_o[0])
    y_attn = _swa(qkv, pos, freq, attn_sinks[0])
    y_ssm = _ssd(xbc, z, dt, ssm_conv_w[0], ssm_conv_b[0], ssm_dt_bias[0], ssm_a_log[0], ssm_d[0],
                 ssm_norm_g[0])
    h1, hg, r = _out_proj(y_attn, y_ssm, w_o_bf, h, norm_ffn_g[0])
    act, w_down_bf = _ffn_up(hg, r, w_up[0], ffn_conv_w[0], ffn_conv_b[0].reshape(1, 2 * D_FF), w_down[0])
    h2, h2b, w_gate_bf = _ffn_down(act, w_down_bf, h1, ple_gate_w[0])
    out = _ple(h2b, w_gate_bf, ple_gate_b[0].reshape(1, d), p[0].reshape(s, PLE_DIM),
               ple_proj[0].astype(BF16), h2, norm_final_g.reshape(1, d))
    return out.reshape(b, s, d)
```
